```python
import jax, jax.numpy as jnp
from jax import lax
import numpy as np

D_MODEL = 1024
BATCH = 8
SEQ = 2048
DEPTH = 2
DEC_BATCH = 128
DEC_SEQ = 8
PAST_LEN = 16384
PAGE_SIZE = 128

N_EVEN = (DEPTH + 1) // 2
N_ODD = DEPTH // 2
MIX_HALF = D_MODEL // 2
A_WIDTH = MIX_HALF
CONV_W = 3
B_KDIM = 128
B_VDIM = 128
B_HEADS = MIX_HALF // B_VDIM
C_KDIM = 128
C_VDIM = 128
C_HEADS = MIX_HALF // C_VDIM
D_HDIM = 64
D_HEADS = MIX_HALF // D_HDIM
D_WIDTH = D_HEADS * D_HDIM
D_W_RANK = 64
D_A_RANK = 64
D_G_RANK = 128
D_FF = ((8 * D_MODEL // 3 + 127) // 128) * 128
N_MEM = 256
X_HEADS = 4
X_HDIM = D_MODEL // X_HEADS

CHUNK = 64
NORM_EPS = 1e-6
RWKV_GN_EPS = 64e-5
ROPE_BASE = 10000.0

EVEN_SIZES = (A_WIDTH, A_WIDTH, A_WIDTH, B_HEADS * B_KDIM, B_HEADS * B_KDIM, B_HEADS * B_VDIM, B_HEADS * B_VDIM)
C_SIZES = (C_HEADS * C_KDIM, C_HEADS * C_KDIM, C_HEADS * C_VDIM, C_HEADS * C_VDIM)
D_SIZES = (D_WIDTH, D_WIDTH, D_WIDTH, D_W_RANK, D_A_RANK, D_G_RANK)
EVEN_IN = sum(EVEN_SIZES)
C_IN = sum(C_SIZES)
D_IN = sum(D_SIZES)
ODD_IN = C_IN + D_IN
EVEN_OUT = A_WIDTH + B_HEADS * B_VDIM
ODD_OUT = C_HEADS * C_VDIM + D_WIDTH

kernel_name = "hybrid_conv_hgrn2_retnet_rwkv7_macaron_step"


def _split(t, sizes):
    idx = np.cumsum(np.array(sizes))[:-1].tolist()
    return jnp.split(t, idx, axis=-1)


def _heads(t, n):
    return t.reshape(t.shape[:-1] + (n, t.shape[-1] // n))


def _merge(t):
    return t.reshape(t.shape[:-2] + (t.shape[-2] * t.shape[-1],))


def rmsnorm(x, g):
    xf = x.astype(jnp.float32)
    y = xf * lax.rsqrt(jnp.mean(xf * xf, axis=-1, keepdims=True) + NORM_EPS)
    return (y * g.astype(jnp.float32)).astype(x.dtype)


def _head_rmsnorm(o, g):
    o = o * lax.rsqrt(jnp.mean(o * o, axis=-1, keepdims=True) + NORM_EPS)
    return _merge(o) * g.astype(jnp.float32)


def swiglu(x, w_gu, w_down):
    gate, up = jnp.split(x @ w_gu, 2, axis=-1)
    return (jax.nn.silu(gate) * up) @ w_down


def _rope(t, pos):
    half = t.shape[-1] // 2
    inv = ROPE_BASE ** (-jnp.arange(half, dtype=jnp.float32) / half)
    ang = pos.astype(jnp.float32)[:, None] * inv[None, :]
    cos = jnp.cos(ang)[None, :, None, :]
    sin = jnp.sin(ang)[None, :, None, :]
    t1, t2 = t[..., :half], t[..., half:]
    return jnp.concatenate([t1 * cos - t2 * sin, t1 * sin + t2 * cos], axis=-1)


def _chunk_len(T):
    return CHUNK if T % CHUNK == 0 else T


def _to_chunks(t, C):
    Bn, T, H, d = t.shape
    return t.reshape(Bn, T // C, C, H, d).transpose(1, 0, 3, 2, 4)


def _from_chunks(o):
    n, Bn, H, C, d = o.shape
    return o.transpose(1, 0, 3, 2, 4).reshape(Bn, n * C, H, d)


def gla_chunked(q, k, v, logf, S0):
    C = _chunk_len(q.shape[1])
    mask = jnp.tril(jnp.ones((C, C), dtype=bool))[:, :, None]

    def step(S, inp):
        qc, kc, vc, gc = inp
        b = jnp.cumsum(gc, axis=2)
        inter = jnp.einsum('bhck,bhkv->bhcv', qc * jnp.exp(b), S)
        diff = b[:, :, :, None, :] - b[:, :, None, :, :]
        dec = jnp.exp(jnp.where(mask, diff, -jnp.inf))
        att = jnp.einsum('bhik,bhjk,bhijk->bhij', qc, kc, dec)
        intra = jnp.einsum('bhij,bhjv->bhiv', att, vc)
        b_last = b[:, :, -1:, :]
        S = jnp.exp(b_last[:, :, 0, :])[..., None] * S + jnp.einsum(
            'bhck,bhcv->bhkv', kc * jnp.exp(b_last - b), vc)
        return S, inter + intra

    S, o = lax.scan(step, S0, (_to_chunks(q, C), _to_chunks(k, C), _to_chunks(v, C), _to_chunks(logf, C)))
    return _from_chunks(o), S


def retention_chunked(q, k, v, S0):
    C = _chunk_len(q.shape[1])
    lg = jnp.log1p(-jnp.exp2(-5.0 - jnp.arange(C_HEADS, dtype=jnp.float32)))[:, None]
    idx = jnp.arange(C, dtype=jnp.float32)
    q_dec = jnp.exp(lg * (idx + 1.0))[None, :, :, None]
    k_dec = jnp.exp(lg * (C - 1.0 - idx))[None, :, :, None]
    rel = idx[:, None] - idx[None, :]
    dmask = jnp.where(rel >= 0, jnp.exp(lg[:, :, None] * jnp.maximum(rel, 0.0)), 0.0)
    chunk_dec = jnp.exp(lg * C)[None, :, :, None]

    def step(S, inp):
        qc, kc, vc = inp
        inter = jnp.einsum('bhck,bhkv->bhcv', qc, S) * q_dec
        att = jnp.einsum('bhik,bhjk->bhij', qc, kc) * dmask
        intra = jnp.einsum('bhij,bhjv->bhiv', att, vc)
        S = chunk_dec * S + jnp.einsum('bhck,bhcv->bhkv', kc * k_dec, vc)
        return S, inter + intra

    S, o = lax.scan(step, S0, (_to_chunks(q, C), _to_chunks(k, C), _to_chunks(v, C)))
    return _from_chunks(o), S


def rwkv7_scan(r, w, k, v, a_vec, b_vec, S0):
    def step(S, inp):
        rt, wt, kt, vt, at, bt = inp
        sa = jnp.einsum('bhvk,bhk->bhv', S, at)
        S = S * wt[:, :, None, :] + sa[..., None] * bt[:, :, None, :] + vt[..., None] * kt[:, :, None, :]
        return S, jnp.einsum('bhvk,bhk->bhv', S, rt)

    xs = tuple(t.transpose(1, 0, 2, 3) for t in (r, w, k, v, a_vec, b_vec))
    S, o = lax.scan(step, S0, xs)
    return o.transpose(1, 0, 2, 3), S


def even_mixer(h, conv_buf, s_hgrn, w_in, w_out, conv_w, lb, gnorm):
    f32 = jnp.float32
    p = (h @ w_in).astype(f32)
    v_a, b_a, c_a, q_b, f_b, i_b, g_b = _split(p, EVEN_SIZES)
    u = c_a * v_a
    T = u.shape[1]
    ext = jnp.concatenate([conv_buf.astype(f32), u], axis=1)
    cw = conv_w.astype(f32)
    conv = sum(ext[:, j:j + T] * cw[j] for j in range(CONV_W))
    y_a = b_a * conv
    new_buf = ext[:, T:]
    lbf = lb.astype(f32)
    f = lbf + (1.0 - lbf) * jax.nn.sigmoid(f_b)
    q = _heads(jax.nn.silu(q_b), B_HEADS)
    k = _heads(1.0 - f, B_HEADS)
    logf = _heads(jnp.log(f), B_HEADS)
    v = _heads(i_b, B_HEADS)
    o, s_new = gla_chunked(q, k, v, logf, s_hgrn.astype(f32))
    y_b = _head_rmsnorm(o, gnorm) * jax.nn.silu(g_b)
    y = jnp.concatenate([y_a, y_b], axis=-1).astype(h.dtype) @ w_out
    return y, new_buf, s_new


def odd_mixer(h, pos, s_ret, s_rwkv, shift_prev, w_in, w_out, ret_gnorm, mu, w0, w2, a0, a2, g2,
              k_k, k_a, r_k, lnx_g, lnx_b):
    f32 = jnp.float32
    p = (h @ w_in).astype(f32)
    pc, pd = p[..., :C_IN], p[..., C_IN:]
    q, k, v, g = _split(pc, C_SIZES)
    q = _rope(_heads(q, C_HEADS), pos)
    k = _rope(_heads(k, C_HEADS), pos) * (C_KDIM ** -0.5)
    o_c, s_ret_new = retention_chunked(q, k, _heads(v, C_HEADS), s_ret.astype(f32))
    y_c = _head_rmsnorm(o_c, ret_gnorm) * jax.nn.silu(g)
    prev = jnp.concatenate([shift_prev.astype(f32)[:, None], pd[:, :-1]], axis=1)
    pm = pd + mu.astype(f32) * (prev - pd)
    r, kd, vd, w_dn, a_dn, g_dn = _split(pm, D_SIZES)
    w_log = -jax.nn.softplus(-(w0.astype(f32) + jnp.tanh(w_dn) @ w2.astype(f32))) - 0.5
    decay = jnp.exp(-jnp.exp(w_log))
    a = jax.nn.sigmoid(a0.astype(f32) + a_dn @ a2.astype(f32))
    gate = jax.nn.sigmoid(g_dn) @ g2.astype(f32)
    kk = _heads(kd * k_k.astype(f32), D_HEADS)
    kk = kk / jnp.maximum(jnp.linalg.norm(kk, axis=-1, keepdims=True), 1e-12)
    kd = kd * (1.0 + (a - 1.0) * k_a.astype(f32))
    r_h, k_h, v_h, a_h = (_heads(t, D_HEADS) for t in (r, kd, vd, a))
    o_d, s_rwkv_new = rwkv7_scan(r_h, _heads(decay, D_HEADS), k_h, v_h, -kk, kk * a_h, s_rwkv.astype(f32))
    mean = jnp.mean(o_d, axis=-1, keepdims=True)
    var = jnp.mean(jnp.square(o_d - mean), axis=-1, keepdims=True)
    on = _merge((o_d - mean) * lax.rsqrt(var + RWKV_GN_EPS)) * lnx_g.astype(f32) + lnx_b.astype(f32)
    bonus = _merge(jnp.sum(r_h * k_h * r_k.astype(f32), axis=-1, keepdims=True) * v_h)
    y_d = (on + bonus) * gate
    y = jnp.concatenate([y_c, y_d], axis=-1).astype(h.dtype) @ w_out
    return y, s_ret_new, s_rwkv_new, pd[:, -1]


def mem_kv(mem, mem_g, w_kv):
    k, v = jnp.split(rmsnorm(mem, mem_g) @ w_kv, 2, axis=-1)
    return _heads(k, X_HEADS), _heads(v, X_HEADS)


def cross_attend(h, mk, mv, wq, wo):
    q = _heads(h @ wq, X_HEADS)
    s = jnp.einsum('bthd,bmhd->bhtm', q, mk.astype(q.dtype)).astype(jnp.float32) * (X_HDIM ** -0.5)
    pr = jax.nn.softmax(s, axis=-1).astype(h.dtype)
    o = jnp.einsum('bhtm,bmhd->bthd', pr, mv.astype(h.dtype))
    return _merge(o) @ wo


def trunk(x, pos0, conv_buf, s_hgrn, s_ret, s_rwkv, s_shift, mem_k, mem_v, prm):
    T = x.shape[1]
    pos = pos0 + jnp.arange(T, dtype=jnp.int32)
    lb_all = jnp.cumsum(jax.nn.softmax(prm['hgrn_lb'].astype(jnp.float32), axis=0), axis=0)
    n_conv, n_hgrn, n_ret, n_rwkv, n_shift = [], [], [], [], []
    for l in range(DEPTH):
        x = x + 0.5 * swiglu(rmsnorm(x, prm['ffn1_norm'][l]), prm['ffn1_w_gu'][l], prm['ffn1_w_down'][l])
        h = rmsnorm(x, prm['mix_norm'][l])
        j = l // 2
        if l % 2 == 0:
            y, cb, sh = even_mixer(h, conv_buf[j], s_hgrn[j], prm['even_w_in'][j], prm['even_w_out'][j],
                                   prm['conv_w'][j], lb_all[j], prm['hgrn_gnorm'][j])
            n_conv.append(cb.astype(x.dtype))
            n_hgrn.append(sh.astype(x.dtype))
        else:
            y, sr, sw, ss = odd_mixer(h, pos, s_ret[j], s_rwkv[j], s_shift[j], prm['odd_w_in'][j],
                                      prm['odd_w_out'][j], prm['ret_gnorm'][j], prm['rwkv_mu'][j],
                                      prm['rwkv_w0'][j], prm['rwkv_w2'][j], prm['rwkv_a0'][j],
                                      prm['rwkv_a2'][j], prm['rwkv_g2'][j], prm['rwkv_k_k'][j],
                                      prm['rwkv_k_a'][j], prm['rwkv_r_k'][j], prm['rwkv_lnx_g'][j],
                                      prm['rwkv_lnx_b'][j])
            n_ret.append(sr.astype(x.dtype))
            n_rwkv.append(sw.astype(x.dtype))
            n_shift.append(ss.astype(x.dtype))
        x = x + y.astype(x.dtype)
        x = x + cross_attend(rmsnorm(x, prm['xattn_norm'][l]), mem_k[l], mem_v[l],
                             prm['xattn_wq'][l], prm['xattn_wo'][l])
        x = x + 0.5 * swiglu(rmsnorm(x, prm['ffn2_norm'][l]), prm['ffn2_w_gu'][l], prm['ffn2_w_down'][l])
    y = rmsnorm(x, prm['final_norm'])
    return (y, jnp.stack(n_conv), jnp.stack(n_hgrn), jnp.stack(n_ret), jnp.stack(n_rwkv), jnp.stack(n_shift))


def setup_inputs(seed: int = 0) -> dict:
    key = jax.random.key(seed)
    keys = jax.random.split(key, 64)
    counter = iter(range(64))
    f32 = jnp.float32

    def nrm(shape, scale=1.0):
        return jax.random.normal(keys[next(counter)], shape, f32) * scale

    def gain(shape):
        return 1.0 + 0.05 * jax.random.normal(keys[next(counter)], shape, f32)

    def unif(shape, lo, hi):
        return jax.random.uniform(keys[next(counter)], shape, f32, lo, hi)

    d = D_MODEL
    return {
        'x_prompt': nrm((BATCH, SEQ, d)),
        'x_sample': nrm((DEC_BATCH, DEC_SEQ, d)),
        'state_conv': nrm((N_EVEN, DEC_BATCH, CONV_W - 1, A_WIDTH)),
        'state_hgrn': nrm((N_EVEN, DEC_BATCH, B_HEADS, B_KDIM, B_VDIM), 0.5),
        'state_ret': nrm((N_ODD, DEC_BATCH, C_HEADS, C_KDIM, C_VDIM)),
        'state_rwkv': nrm((N_ODD, DEC_BATCH, D_HEADS, D_HDIM, D_HDIM), 0.5),
        'state_shift': nrm((N_ODD, DEC_BATCH, D_IN)),
        'cache_mem_k': nrm((DEPTH, DEC_BATCH, N_MEM, X_HEADS, X_HDIM)),
        'cache_mem_v': nrm((DEPTH, DEC_BATCH, N_MEM, X_HEADS, X_HDIM)),
        'mem_prompt': nrm((BATCH, N_MEM, d)),
        'ffn1_norm': gain((DEPTH, d)),
        'ffn1_w_gu': nrm((DEPTH, d, 2 * D_FF), d ** -0.5),
        'ffn1_w_down': nrm((DEPTH, D_FF, d), D_FF ** -0.5),
        'mix_norm': gain((DEPTH, d)),
        'even_w_in': nrm((N_EVEN, d, EVEN_IN), d ** -0.5),
        'even_w_out': nrm((N_EVEN, EVEN_OUT, d), EVEN_OUT ** -0.5),
        'conv_w': nrm((N_EVEN, CONV_W, A_WIDTH), CONV_W ** -0.5),
        'hgrn_lb': nrm((N_EVEN + 1, B_HEADS * B_KDIM), 0.5),
        'hgrn_gnorm': gain((N_EVEN, B_HEADS * B_VDIM)),
        'odd_w_in': nrm((N_ODD, d, ODD_IN), d ** -0.5),
        'odd_w_out': nrm((N_ODD, ODD_OUT, d), ODD_OUT ** -0.5),
        'ret_gnorm': gain((N_ODD, C_HEADS * C_VDIM)),
        'rwkv_mu': unif((N_ODD, D_IN), 0.0, 1.0),
        'rwkv_w0': unif((N_ODD, D_WIDTH), -5.0, 0.5),
        'rwkv_w2': nrm((N_ODD, D_W_RANK, D_WIDTH), D_W_RANK ** -0.5),
        'rwkv_a0': nrm((N_ODD, D_WIDTH), 0.1),
        'rwkv_a2': nrm((N_ODD, D_A_RANK, D_WIDTH), D_A_RANK ** -0.5),
        'rwkv_g2': nrm((N_ODD, D_G_RANK, D_WIDTH), D_G_RANK ** -0.5),
        'rwkv_k_k': 0.85 + nrm((N_ODD, D_WIDTH), 0.05),
        'rwkv_k_a': gain((N_ODD, D_WIDTH)),
        'rwkv_r_k': nrm((N_ODD, D_HEADS, D_HDIM), 0.1),
        'rwkv_lnx_g': gain((N_ODD, D_WIDTH)),
        'rwkv_lnx_b': nrm((N_ODD, D_WIDTH), 0.01),
        'xattn_norm': gain((DEPTH, d)),
        'mem_norm': gain((DEPTH, d)),
        'xattn_wq': nrm((DEPTH, d, d), d ** -0.5),
        'xattn_wkv': nrm((DEPTH, d, 2 * d), d ** -0.5),
        'xattn_wo': nrm((DEPTH, d, d), d ** -0.5),
        'ffn2_norm': gain((DEPTH, d)),
        'ffn2_w_gu': nrm((DEPTH, d, 2 * D_FF), d ** -0.5),
        'ffn2_w_down': nrm((DEPTH, D_FF, d), D_FF ** -0.5),
        'final_norm': gain((d,)),
    }


def reference(x_prompt, x_sample, state_conv, state_hgrn, state_ret, state_rwkv, state_shift,
              cache_mem_k, cache_mem_v, mem_prompt, ffn1_norm, ffn1_w_gu, ffn1_w_down, mix_norm,
              even_w_in, even_w_out, conv_w, hgrn_lb, hgrn_gnorm, odd_w_in, odd_w_out, ret_gnorm,
              rwkv_mu, rwkv_w0, rwkv_w2, rwkv_a0, rwkv_a2, rwkv_g2, rwkv_k_k, rwkv_k_a, rwkv_r_k,
              rwkv_lnx_g, rwkv_lnx_b, xattn_norm, mem_norm, xattn_wq, xattn_wkv, xattn_wo,
              ffn2_norm, ffn2_w_gu, ffn2_w_down, final_norm):
    prm = {
        'ffn1_norm': ffn1_norm, 'ffn1_w_gu': ffn1_w_gu, 'ffn1_w_down': ffn1_w_down,
        'mix_norm': mix_norm, 'even_w_in': even_w_in, 'even_w_out': even_w_out, 'conv_w': conv_w,
        'hgrn_lb': hgrn_lb, 'hgrn_gnorm': hgrn_gnorm, 'odd_w_in': odd_w_in, 'odd_w_out': odd_w_out,
        'ret_gnorm': ret_gnorm, 'rwkv_mu': rwkv_mu, 'rwkv_w0': rwkv_w0, 'rwkv_w2': rwkv_w2,
        'rwkv_a0': rwkv_a0, 'rwkv_a2': rwkv_a2, 'rwkv_g2': rwkv_g2, 'rwkv_k_k': rwkv_k_k,
        'rwkv_k_a': rwkv_k_a, 'rwkv_r_k': rwkv_r_k, 'rwkv_lnx_g': rwkv_lnx_g, 'rwkv_lnx_b': rwkv_lnx_b,
        'xattn_norm': xattn_norm, 'xattn_wq': xattn_wq, 'xattn_wo': xattn_wo,
        'ffn2_norm': ffn2_norm, 'ffn2_w_gu': ffn2_w_gu, 'ffn2_w_down': ffn2_w_down,
        'final_norm': final_norm,
    }
    mks, mvs = [], []
    for l in range(DEPTH):
        mk, mv = mem_kv(mem_prompt, mem_norm[l], xattn_wkv[l])
        mks.append(mk)
        mvs.append(mv)
    mem_k_p = jnp.stack(mks)
    mem_v_p = jnp.stack(mvs)
    dt = x_prompt.dtype
    zc = jnp.zeros((N_EVEN, BATCH, CONV_W - 1, A_WIDTH), dt)
    zh = jnp.zeros((N_EVEN, BATCH, B_HEADS, B_KDIM, B_VDIM), dt)
    zr = jnp.zeros((N_ODD, BATCH, C_HEADS, C_KDIM, C_VDIM), dt)
    zw = jnp.zeros((N_ODD, BATCH, D_HEADS, D_HDIM, D_HDIM), dt)
    zs = jnp.zeros((N_ODD, BATCH, D_IN), dt)
    y_prompt, conv_p, hgrn_p, ret_p, rwkv_p, shift_p = trunk(
        x_prompt, 0, zc, zh, zr, zw, zs, mem_k_p, mem_v_p, prm)
    y_sample, conv_s, hgrn_s, ret_s, rwkv_s, shift_s = trunk(
        x_sample, PAST_LEN, state_conv, state_hgrn, state_ret, state_rwkv, state_shift,
        cache_mem_k, cache_mem_v, prm)
    return (y_prompt, y_sample, conv_p, hgrn_p, ret_p, rwkv_p, shift_p, mem_k_p, mem_v_p,
            conv_s, hgrn_s, ret_s, rwkv_s, shift_s)
```

```python
import functools
import math

import jax
import jax.numpy as jnp
from jax import lax
from jax.experimental import pallas as pl
from jax.experimental.pallas import tpu as pltpu

F32 = jnp.float32
_MXU_DTYPE = jnp.bfloat16

NORM_EPS = 1e-6
RWKV_GN_EPS = 64e-5
ROPE_BASE = 10000.0

_VMEM_LIMIT_BYTES = 56 * 1024 * 1024
_SUBLANES = 8
_PACKED_ROWS = 16

A_WIDTH = 512
B_HEADS, B_DIM = 4, 128
C_HEADS, C_DIM = 4, 128
D_HEADS, D_HDIM = 8, 64
D_WIDTH = D_HEADS * D_HDIM
EVEN_IN = 7 * 512
C_IN = 4 * 512
D_IN = 3 * 512 + 64 + 64 + 128
ODD_IN = C_IN + D_IN
X_HEADS = 4


def _mm(a, b):
    return jnp.dot(a.astype(_MXU_DTYPE), b.astype(_MXU_DTYPE), preferred_element_type=F32)


def _mm_nt(a, b):
    return lax.dot_general(a.astype(_MXU_DTYPE), b.astype(_MXU_DTYPE), (((1,), (1,)), ((), ())),
                           preferred_element_type=F32)


def _mm_tn(a, b):
    k = a.shape[0]
    if k % _PACKED_ROWS:
        pad = _PACKED_ROWS - k % _PACKED_ROWS
        a = jnp.concatenate([a, jnp.zeros((pad, a.shape[1]), a.dtype)], axis=0)
        b = jnp.concatenate([b, jnp.zeros((pad, b.shape[1]), b.dtype)], axis=0)
    return lax.dot_general(a.astype(_MXU_DTYPE), b.astype(_MXU_DTYPE), (((0,), (0,)), ((), ())),
                           preferred_element_type=F32)


def _split2(a):
    hi = a.astype(_MXU_DTYPE)
    lo = (a - hi.astype(F32)).astype(_MXU_DTYPE)
    return hi, lo


def _mm_hi(a, b):
    ah, al = _split2(a)
    bh, bl = _split2(b)
    d = functools.partial(jnp.dot, preferred_element_type=F32)
    return d(ah, bh) + d(ah, bl) + d(al, bh)


def _mm_exact_lhs(m01, x):
    m = m01.astype(_MXU_DTYPE)
    x0 = x.astype(_MXU_DTYPE)
    r1 = x - x0.astype(F32)
    x1 = r1.astype(_MXU_DTYPE)
    x2 = (r1 - x1.astype(F32)).astype(_MXU_DTYPE)
    d = functools.partial(jnp.dot, preferred_element_type=F32)
    return d(m, x0) + d(m, x1) + d(m, x2)


def _mm_exact_rhs(x, m01):
    m = m01.astype(_MXU_DTYPE)
    x0 = x.astype(_MXU_DTYPE)
    r1 = x - x0.astype(F32)
    x1 = r1.astype(_MXU_DTYPE)
    x2 = (r1 - x1.astype(F32)).astype(_MXU_DTYPE)
    d = functools.partial(jnp.dot, preferred_element_type=F32)
    return d(x0, m) + d(x1, m) + d(x2, m)


def _rms(x, g):
    return x * lax.rsqrt(jnp.mean(x * x, axis=-1, keepdims=True) + NORM_EPS) * g


def _sigmoid(x):
    return 1.0 / (1.0 + jnp.exp(-x))


def _silu(x):
    return x * _sigmoid(x)


def _softplus(x):
    return jnp.maximum(x, 0.0) + jnp.log1p(jnp.exp(-jnp.abs(x)))


def _block_tri(n, blk, strict=False):
    r = lax.broadcasted_iota(jnp.int32, (n, n), 0)
    c = lax.broadcasted_iota(jnp.int32, (n, n), 1)
    same = (r // blk) == (c // blk)
    low = (c < r) if strict else (c <= r)
    return same & low


def _head_ones(width, hd):
    r = lax.broadcasted_iota(jnp.int32, (width, width), 0)
    c = lax.broadcasted_iota(jnp.int32, (width, width), 1)
    return ((r // hd) == (c // hd)).astype(F32)


def _ffn_kernel(x_ref, g_ref, wg_ref, wu_ref, wd_ref, fg_ref, o_ref, h_scr, acc_scr, *, n_chunks, final):
    x = x_ref[...]
    h_scr[...] = _rms(x, g_ref[...]).astype(_MXU_DTYPE)
    acc_scr[...] = jnp.zeros_like(acc_scr)

    def body(c, carry):
        h = h_scr[...]
        gate = jnp.dot(h, wg_ref[c], preferred_element_type=F32)
        up = jnp.dot(h, wu_ref[c], preferred_element_type=F32)
        act = _silu(gate) * up
        acc_scr[...] += jnp.dot(act.astype(_MXU_DTYPE), wd_ref[c], preferred_element_type=F32)
        return carry

    lax.fori_loop(0, n_chunks, body, 0)
    y = x + 0.5 * acc_scr[...]
    if final:
        y = _rms(y, fg_ref[...])
    o_ref[...] = y


def _const_spec(shape):
    nd = len(shape)
    return pl.BlockSpec(shape, lambda *_: (0,) * nd, pipeline_mode=pl.Buffered(1))


def _ffn(x2d, g, wg, wu, wd, final_g, tm):
    n, d = x2d.shape
    n_chunks, _, tf = wg.shape
    final = final_g is not None
    fg = final_g if final else g
    return pl.pallas_call(
        functools.partial(_ffn_kernel, n_chunks=n_chunks, final=final),
        out_shape=jax.ShapeDtypeStruct((n, d), F32),
        grid=(n // tm,),
        in_specs=[
            pl.BlockSpec((tm, d), lambda i: (i, 0)),
            _const_spec((1, d)),
            _const_spec(wg.shape),
            _const_spec(wu.shape),
            _const_spec(wd.shape),
            _const_spec((1, d)),
        ],
        out_specs=pl.BlockSpec((tm, d), lambda i: (i, 0)),
        scratch_shapes=[pltpu.VMEM((tm, d), _MXU_DTYPE), pltpu.VMEM((tm, d), F32)],
        compiler_params=pltpu.CompilerParams(dimension_semantics=("parallel",),
                                             vmem_limit_bytes=_VMEM_LIMIT_BYTES),
        name="ffn",
    )(x2d, g.reshape(1, d), wg, wu, wd, fg.reshape(1, d))


def _norm_matmul_kernel(x_ref, g_ref, w_ref, o_ref):
    o_ref[...] = _mm(_rms(x_ref[...], g_ref[...]), w_ref[...])


def _norm_matmul(x2d, g, w, tm):
    n, d = x2d.shape
    nout = w.shape[1]
    return pl.pallas_call(
        _norm_matmul_kernel,
        out_shape=jax.ShapeDtypeStruct((n, nout), F32),
        grid=(n // tm,),
        in_specs=[pl.BlockSpec((tm, d), lambda i: (i, 0)), _const_spec((1, d)), _const_spec(w.shape)],
        out_specs=pl.BlockSpec((tm, nout), lambda i: (i, 0)),
        compiler_params=pltpu.CompilerParams(dimension_semantics=("parallel",),
                                             vmem_limit_bytes=_VMEM_LIMIT_BYTES),
        name="mem_kv",
    )(x2d, g.reshape(1, d), w)


def _xattn_kernel(x_ref, g_ref, wq_ref, wo_ref, mk_ref, mv_ref, o_ref, q_scr, a_scr, *, nb, tt, n_heads):
    d = x_ref.shape[-1]
    hd = d // n_heads
    rows = nb * tt
    x = x_ref[...].reshape(rows, d)
    q_scr[...] = _mm(_rms(x, g_ref[...]), wq_ref[...])
    scale = hd ** -0.5

    def seq_body(s, carry):
        r0 = pl.multiple_of(s * tt, _SUBLANES)
        for h in range(n_heads):
            hs = slice(h * hd, (h + 1) * hd)
            qh = q_scr[pl.ds(r0, tt), hs]
            sc = _mm_nt(qh, mk_ref[s, :, hs]) * scale
            e = jnp.exp(sc - jnp.max(sc, axis=-1, keepdims=True))
            pr = e / jnp.sum(e, axis=-1, keepdims=True)
            a_scr[pl.ds(r0, tt), hs] = _mm(pr, mv_ref[s, :, hs])
        return carry

    lax.fori_loop(0, nb, seq_body, 0)
    o_ref[...] = (x + _mm(a_scr[...], wo_ref[...])).reshape(nb, tt, d)


def _xattn(x, g, wq, wo, mk, mv, nb, tt):
    b, t, d = x.shape
    n_mem = mk.shape[1]
    rows = nb * tt
    return pl.pallas_call(
        functools.partial(_xattn_kernel, nb=nb, tt=tt, n_heads=X_HEADS),
        out_shape=jax.ShapeDtypeStruct((b, t, d), F32),
        grid=(b // nb, t // tt),
        in_specs=[
            pl.BlockSpec((nb, tt, d), lambda i, j: (i, j, 0)),
            _const_spec((1, d)),
            _const_spec(wq.shape),
            _const_spec(wo.shape),
            pl.BlockSpec((nb, n_mem, d), lambda i, j: (i, 0, 0)),
            pl.BlockSpec((nb, n_mem, d), lambda i, j: (i, 0, 0)),
        ],
        out_specs=pl.BlockSpec((nb, tt, d), lambda i, j: (i, j, 0)),
        scratch_shapes=[pltpu.VMEM((rows, d), F32), pltpu.VMEM((rows, d), F32)],
        compiler_params=pltpu.CompilerParams(dimension_semantics=("parallel", "parallel"),
                                             vmem_limit_bytes=_VMEM_LIMIT_BYTES),
        name="xattn",
    )(x, g.reshape(1, d), wq, wo, mk, mv)


def _even_kernel(x_ref, g_ref, win_ref, wout_ref, cw_ref, lb_ref, gn_ref, conv_in_ref, hg_in_ref,
                 xo_ref, conv_out_ref, hg_out_ref,
                 p_scr, y_scr, ext_scr, b_scr, qg_scr, kk_scr, st_scr, *, nb, tt, blk):
    j = pl.program_id(1)
    nj = pl.num_programs(1)
    d = x_ref.shape[-1]
    rows = nb * tt
    aw = A_WIDTH
    hd = B_DIM

    @pl.when(j == 0)
    def _():
        conv_out_ref[...] = conv_in_ref[...]

        def init(s, carry):
            for h in range(B_HEADS):
                st_scr[s, h] = hg_in_ref[s, h].T
            return carry

        lax.fori_loop(0, nb, init, 0)

    x = x_ref[...].reshape(rows, d)
    p_scr[...] = _mm(_rms(x, g_ref[...]), win_ref[...])

    cw = cw_ref[...]

    def conv_body(s, carry):
        r0 = pl.multiple_of(s * tt, _SUBLANES)
        u = p_scr[pl.ds(r0, tt), 2 * aw:3 * aw] * p_scr[pl.ds(r0, tt), 0:aw]
        ext_scr[_SUBLANES:_SUBLANES + tt, :] = u
        ext_scr[_SUBLANES - 2:_SUBLANES, :] = conv_out_ref[s]
        conv = (cw[0:1] * ext_scr[_SUBLANES - 2:_SUBLANES - 2 + tt, :]
                + cw[1:2] * ext_scr[_SUBLANES - 1:_SUBLANES - 1 + tt, :]
                + cw[2:3] * ext_scr[_SUBLANES:_SUBLANES + tt, :])
        y_scr[pl.ds(r0, tt), 0:aw] = p_scr[pl.ds(r0, tt), aw:2 * aw] * conv
        conv_out_ref[s] = ext_scr[_SUBLANES + tt - 2:_SUBLANES + tt, :]
        return carry

    lax.fori_loop(0, nb, conv_body, 0)

    lb = lb_ref[...]
    f = lb + (1.0 - lb) * _sigmoid(p_scr[:, 4 * aw:5 * aw])
    kk_scr[...] = 1.0 - f
    b_scr[...] = _mm_exact_lhs(_block_tri(rows, blk).astype(F32), jnp.log(f))
    qg_scr[...] = _silu(p_scr[:, 3 * aw:4 * aw])
    rowi = lax.broadcasted_iota(jnp.int32, (blk, hd), 0)
    nblk = tt // blk

    def blk_body(i, carry):
        s = i // nblk
        r0 = pl.multiple_of(i * blk, _SUBLANES)
        rs = pl.ds(r0, blk)
        for h in range(B_HEADS):
            hs = slice(h * hd, (h + 1) * hd)
            bb = b_scr[rs, hs]
            blast = bb[blk - 1:blk]
            qg = qg_scr[rs, hs]
            kkb = kk_scr[rs, hs]
            vb = p_scr[rs, 5 * aw + h * hd:5 * aw + (h + 1) * hd]
            st = st_scr[s, h]
            o = _mm_nt(qg * jnp.exp(bb), st)
            for jj in range(blk):
                dec = jnp.exp(jnp.where(rowi >= jj, bb - bb[jj:jj + 1], -jnp.inf))
                att = jnp.sum(qg * kkb[jj:jj + 1] * dec, axis=-1, keepdims=True)
                o = o + att * vb[jj:jj + 1]
            kd = kkb * jnp.exp(blast - bb)
            st_scr[s, h] = st * jnp.exp(blast) + _mm_tn(vb, kd)
            on = o * lax.rsqrt(jnp.mean(o * o, axis=-1, keepdims=True) + NORM_EPS) * gn_ref[:, hs]
            gb = p_scr[rs, 6 * aw + h * hd:6 * aw + (h + 1) * hd]
            y_scr[rs, aw + h * hd:aw + (h + 1) * hd] = on * _silu(gb)
        return carry

    lax.fori_loop(0, nb * nblk, blk_body, 0)

    xo_ref[...] = (x + _mm(y_scr[...], wout_ref[...])).reshape(nb, tt, d)

    @pl.when(j == nj - 1)
    def _():
        def fin(s, carry):
            for h in range(B_HEADS):
                hg_out_ref[s, h] = st_scr[s, h].T
            return carry

        lax.fori_loop(0, nb, fin, 0)


def _even_layer(x, g, win, wout, cw, lb, gn, conv_in, hg_in, nb, tt, blk):
    b, t, d = x.shape
    rows = nb * tt
    return pl.pallas_call(
        functools.partial(_even_kernel, nb=nb, tt=tt, blk=blk),
        out_shape=(jax.ShapeDtypeStruct((b, t, d), F32),
                   jax.ShapeDtypeStruct(conv_in.shape, F32),
                   jax.ShapeDtypeStruct(hg_in.shape, F32)),
        grid=(b // nb, t // tt),
        in_specs=[
            pl.BlockSpec((nb, tt, d), lambda i, j: (i, j, 0)),
            _const_spec((1, d)),
            _const_spec(win.shape),
            _const_spec(wout.shape),
            _const_spec(cw.shape),
            _const_spec((1, A_WIDTH)),
            _const_spec((1, B_HEADS * B_DIM)),
            pl.BlockSpec((nb,) + conv_in.shape[1:], lambda i, j: (i, 0, 0)),
            pl.BlockSpec((nb,) + hg_in.shape[1:], lambda i, j: (i, 0, 0, 0)),
        ],
        out_specs=(
            pl.BlockSpec((nb, tt, d), lambda i, j: (i, j, 0)),
            pl.BlockSpec((nb,) + conv_in.shape[1:], lambda i, j: (i, 0, 0)),
            pl.BlockSpec((nb,) + hg_in.shape[1:], lambda i, j: (i, 0, 0, 0)),
        ),
        scratch_shapes=[
            pltpu.VMEM((rows, EVEN_IN), F32),
            pltpu.VMEM((rows, 2 * A_WIDTH), F32),
            pltpu.VMEM((tt + _SUBLANES, A_WIDTH), F32),
            pltpu.VMEM((rows, A_WIDTH), F32),
            pltpu.VMEM((rows, A_WIDTH), F32),
            pltpu.VMEM((rows, A_WIDTH), F32),
            pltpu.VMEM((nb, B_HEADS, B_DIM, B_DIM), F32),
        ],
        compiler_params=pltpu.CompilerParams(dimension_semantics=("parallel", "arbitrary"),
                                             vmem_limit_bytes=_VMEM_LIMIT_BYTES),
        name="even_layer",
    )(x, g.reshape(1, d), win, wout, cw, lb.reshape(1, -1), gn.reshape(1, -1), conv_in, hg_in)


def _odd_kernel(x_ref, g_ref, win_ref, wout_ref, cos_ref, sin_ref, qdec_ref, kdec_ref, cdec_ref, dmask_ref,
                rgn_ref, mu_ref, w0_ref, w2_ref, a0_ref, a2_ref, g2_ref, kk_ref, ka_ref, rk_ref, lng_ref,
                lnb_ref, ret_in_ref, rw_in_ref, sh_in_ref,
                xo_ref, ret_out_ref, rw_out_ref, sh_out_ref,
                p_scr, y_scr, ext_scr, prev_scr, qr_scr, kr_scr, ir_scr,
                rt_scr, at_scr, bt_scr, kt_scr, vd_scr, cum_scr, as_scr, rs_scr, u_scr, od_scr,
                *, nb, tt, ch, gl):
    j = pl.program_id(1)
    d = x_ref.shape[-1]
    rows = nb * tt
    cw = C_HEADS * C_DIM
    hd = C_DIM
    nd = D_HDIM
    n_groups = rows // gl
    cpg = gl // ch
    cps = tt // ch

    @pl.when(j == 0)
    def _():
        ret_out_ref[...] = ret_in_ref[...]
        rw_out_ref[...] = rw_in_ref[...]
        sh_out_ref[...] = sh_in_ref[...]

    x = x_ref[...].reshape(rows, d)
    p_scr[...] = _mm(_rms(x, g_ref[...]), win_ref[...])

    cos = cos_ref[0]
    sin = sin_ref[0]
    for h in range(C_HEADS):
        hs = slice(h * hd, (h + 1) * hd)
        qh = p_scr[:, h * hd:(h + 1) * hd]
        kh = p_scr[:, cw + h * hd:cw + (h + 1) * hd]
        qr_scr[:, hs] = qh * cos + pltpu.roll(qh, hd // 2, axis=1) * sin
        kr_scr[:, hs] = (kh * cos + pltpu.roll(kh, hd // 2, axis=1) * sin) * (C_DIM ** -0.5)

    def ret_group(gi, carry):
        g0 = pl.multiple_of(gi * gl, _SUBLANES)
        gs = pl.ds(g0, gl)

        def ret_chunk(q, c2):
            c = gi * cpg + q
            s = c // cps
            r0 = pl.multiple_of(c * ch, _SUBLANES)
            rs = pl.ds(r0, ch)
            l0 = pl.multiple_of(q * ch, _SUBLANES)
            ls = pl.ds(l0, ch)
            for h in range(C_HEADS):
                hs = slice(h * hd, (h + 1) * hd)
                qc = qr_scr[rs, hs]
                kc = kr_scr[rs, hs]
                vc = p_scr[rs, 2 * cw + h * hd:2 * cw + (h + 1) * hd]
                st = ret_out_ref[s, h]
                ir_scr[rs, hs] = _mm(qc, st) * qdec_ref[ls, hs]
                ret_out_ref[s, h] = cdec_ref[:, hs] * st + _mm_tn(kc * kdec_ref[ls, hs], vc)
            return c2

        lax.fori_loop(0, cpg, ret_chunk, 0)
        for h in range(C_HEADS):
            hs = slice(h * hd, (h + 1) * hd)
            att = _mm_nt(qr_scr[gs, hs], kr_scr[gs, hs]) * dmask_ref[h]
            o = ir_scr[gs, hs] + _mm(att, p_scr[gs, 2 * cw + h * hd:2 * cw + (h + 1) * hd])
            on = o * lax.rsqrt(jnp.mean(o * o, axis=-1, keepdims=True) + NORM_EPS) * rgn_ref[:, hs]
            y_scr[gs, hs] = on * _silu(p_scr[gs, 3 * cw + h * hd:3 * cw + (h + 1) * hd])
        return carry

    lax.fori_loop(0, n_groups, ret_group, 0)

    def shift_body(s, carry):
        r0 = pl.multiple_of(s * tt, _SUBLANES)
        ext_scr[_SUBLANES:_SUBLANES + tt, :] = p_scr[pl.ds(r0, tt), C_IN:ODD_IN]
        ext_scr[_SUBLANES - 1:_SUBLANES, :] = sh_out_ref[s]
        prev_scr[pl.ds(r0, tt), :] = ext_scr[_SUBLANES - 1:_SUBLANES - 1 + tt, :]
        sh_out_ref[s] = ext_scr[_SUBLANES + tt - 1:_SUBLANES + tt, :]
        return carry

    lax.fori_loop(0, nb, shift_body, 0)

    pd = p_scr[:, C_IN:ODD_IN]
    pm = pd + mu_ref[...] * (prev_scr[...] - pd)
    dw = D_WIDTH
    r = pm[:, 0:dw]
    kd = pm[:, dw:2 * dw]
    vd = pm[:, 2 * dw:3 * dw]
    w_dn = pm[:, 3 * dw:3 * dw + 64]
    a_dn = pm[:, 3 * dw + 64:3 * dw + 128]
    g_dn = pm[:, 3 * dw + 128:3 * dw + 256]
    w_log = -_softplus(-(w0_ref[...] + _mm(jnp.tanh(w_dn), w2_ref[...]))) - 0.5
    logdec = -jnp.exp(w_log)
    a = _sigmoid(a0_ref[...] + _mm(a_dn, a2_ref[...]))
    gate = _mm(_sigmoid(g_dn), g2_ref[...])
    ones_h = _head_ones(dw, nd)
    kk0 = kd * kk_ref[...]
    kk = kk0 / jnp.maximum(jnp.sqrt(_mm_exact_rhs(kk0 * kk0, ones_h)), 1e-12)
    kd2 = kd * (1.0 + (a - 1.0) * ka_ref[...])
    cum = _mm_exact_lhs(_block_tri(rows, ch).astype(F32), logdec)
    cum_scr[...] = cum
    rt_scr[...] = r * jnp.exp(cum)
    at_scr[...] = -kk * jnp.exp(cum - logdec)
    e_neg = jnp.exp(-cum)
    bt_scr[...] = kk * a * e_neg
    kt_scr[...] = kd2 * e_neg
    vd_scr[...] = vd
    bonus = _mm_exact_rhs(r * kd2 * rk_ref[...], ones_h) * vd

    strict = _block_tri(gl, ch, strict=True)
    incl = _block_tri(gl, ch)
    n_dbl = int(math.log2(ch))

    def rw_group(gi, carry):
        g0 = pl.multiple_of(gi * gl, _SUBLANES)
        gs = pl.ds(g0, gl)

        def read_state(q, c2):
            c = gi * cpg + q
            s = c // cps
            rs = pl.ds(pl.multiple_of(c * ch, _SUBLANES), ch)
            for h in range(D_HEADS):
                hs = slice(h * nd, (h + 1) * nd)
                ar = jnp.concatenate([at_scr[rs, hs], rt_scr[rs, hs]], axis=0)
                ars = _mm_nt(ar, rw_out_ref[s, h])
                as_scr[rs, hs] = ars[0:ch]
                rs_scr[rs, hs] = ars[ch:2 * ch]
            return c2

        lax.fori_loop(0, cpg, read_state, 0)

        for h in range(D_HEADS):
            hs = slice(h * nd, (h + 1) * nd)
            at = at_scr[gs, hs]
            rt = rt_scr[gs, hs]
            bt = bt_scr[gs, hs]
            kt = kt_scr[gs, hs]
            vv = vd_scr[gs, hs]
            n_ab = jnp.where(strict, _mm_nt(at, bt), 0.0)
            a_ak = jnp.where(strict, _mm_nt(at, kt), 0.0)
            u = as_scr[gs, hs] + _mm(a_ak, vv)
            pw = n_ab
            for it in range(n_dbl):
                u = u + _mm_hi(pw, u)
                if it + 1 < n_dbl:
                    pw = _mm_hi(pw, pw)
            u_scr[gs, hs] = u
            r_rb = jnp.where(incl, _mm_nt(rt, bt), 0.0)
            r_rk = jnp.where(incl, _mm_nt(rt, kt), 0.0)
            od_scr[gs, hs] = rs_scr[gs, hs] + _mm(r_rb, u) + _mm(r_rk, vv)

        def write_state(q, c2):
            c = gi * cpg + q
            s = c // cps
            r0 = pl.multiple_of(c * ch, _SUBLANES)
            rs = pl.ds(r0, ch)
            for h in range(D_HEADS):
                hs = slice(h * nd, (h + 1) * nd)
                uv = jnp.concatenate([u_scr[rs, hs], vd_scr[rs, hs]], axis=0)
                bk = jnp.concatenate([bt_scr[rs, hs], kt_scr[rs, hs]], axis=0)
                glast = cum_scr[pl.ds(pl.multiple_of(r0 + ch - _SUBLANES, _SUBLANES), _SUBLANES), hs][_SUBLANES - 1:_SUBLANES]
                rw_out_ref[s, h] = (rw_out_ref[s, h] + _mm_tn(uv, bk)) * jnp.exp(glast)
            return c2

        lax.fori_loop(0, cpg, write_state, 0)
        return carry

    lax.fori_loop(0, n_groups, rw_group, 0)

    o_d = od_scr[...]
    mean = _mm_exact_rhs(o_d, ones_h) * (1.0 / nd)
    xc = o_d - mean
    var = _mm_exact_rhs(xc * xc, ones_h) * (1.0 / nd)
    on = xc * lax.rsqrt(var + RWKV_GN_EPS) * lng_ref[...] + lnb_ref[...]
    y_scr[:, cw:cw + dw] = (on + bonus) * gate

    xo_ref[...] = (x + _mm(y_scr[...], wout_ref[...])).reshape(nb, tt, d)


def _odd_layer(x, g, win, wout, tabs, rgn, rw, ret_in, rw_in, sh_in, nb, tt, ch, gl):
    b, t, d = x.shape
    rows = nb * tt
    cos, sin, qdec, kdec, cdec, dmask = tabs
    sh3 = sh_in.reshape(b, 1, D_IN)
    vecs = [rw[k].reshape(1, -1) for k in ("mu", "w0")] + [rw["w2"]] + [rw["a0"].reshape(1, -1), rw["a2"], rw["g2"]] + \
           [rw[k].reshape(1, -1) for k in ("k_k", "k_a", "r_k", "lnx_g", "lnx_b")]
    w512 = pltpu.VMEM((rows, D_WIDTH), F32)
    outs = pl.pallas_call(
        functools.partial(_odd_kernel, nb=nb, tt=tt, ch=ch, gl=gl),
        out_shape=(jax.ShapeDtypeStruct((b, t, d), F32),
                   jax.ShapeDtypeStruct(ret_in.shape, F32),
                   jax.ShapeDtypeStruct(rw_in.shape, F32),
                   jax.ShapeDtypeStruct(sh3.shape, F32)),
        grid=(b // nb, t // tt),
        in_specs=[
            pl.BlockSpec((nb, tt, d), lambda i, j: (i, j, 0)),
            _const_spec((1, d)),
            _const_spec(win.shape),
            _const_spec(wout.shape),
            pl.BlockSpec((1, rows, C_DIM), lambda i, j: (j, 0, 0)),
            pl.BlockSpec((1, rows, C_DIM), lambda i, j: (j, 0, 0)),
            _const_spec(qdec.shape),
            _const_spec(kdec.shape),
            _const_spec(cdec.shape),
            _const_spec(dmask.shape),
            _const_spec((1, C_HEADS * C_DIM)),
        ] + [_const_spec(v.shape) for v in vecs] + [
            pl.BlockSpec((nb,) + ret_in.shape[1:], lambda i, j: (i, 0, 0, 0)),
            pl.BlockSpec((nb,) + rw_in.shape[1:], lambda i, j: (i, 0, 0, 0)),
            pl.BlockSpec((nb, 1, D_IN), lambda i, j: (i, 0, 0)),
        ],
        out_specs=(
            pl.BlockSpec((nb, tt, d), lambda i, j: (i, j, 0)),
            pl.BlockSpec((nb,) + ret_in.shape[1:], lambda i, j: (i, 0, 0, 0)),
            pl.BlockSpec((nb,) + rw_in.shape[1:], lambda i, j: (i, 0, 0, 0)),
            pl.BlockSpec((nb, 1, D_IN), lambda i, j: (i, 0, 0)),
        ),
        scratch_shapes=[
            pltpu.VMEM((rows, ODD_IN), F32),
            pltpu.VMEM((rows, C_HEADS * C_DIM + D_WIDTH), F32),
            pltpu.VMEM((tt + _SUBLANES, D_IN), F32),
            pltpu.VMEM((rows, D_IN), F32),
        ] + [w512] * 13,
        compiler_params=pltpu.CompilerParams(dimension_semantics=("parallel", "arbitrary"),
                                             vmem_limit_bytes=_VMEM_LIMIT_BYTES),
        name="odd_layer",
    )(x, g.reshape(1, d), win, wout, cos, sin, qdec, kdec, cdec, dmask, rgn.reshape(1, -1), *vecs,
      ret_in, rw_in, sh3)
    xo, ret_o, rw_o, sh_o = outs
    return xo, ret_o, rw_o, sh_o.reshape(b, D_IN)


def _odd_tables(pos0, t, nb, tt, ch, gl):
    half = C_DIM // 2
    inv = ROPE_BASE ** (-jnp.arange(half, dtype=F32) / half)
    pos = pos0 + jnp.arange(t, dtype=jnp.int32)
    ang = pos.astype(F32)[:, None] * inv[None, :]
    cos = jnp.cos(ang)
    sin = jnp.sin(ang)
    cosf = jnp.concatenate([cos, cos], axis=-1).reshape(t // tt, tt, C_DIM)
    sinf = jnp.concatenate([-sin, sin], axis=-1).reshape(t // tt, tt, C_DIM)
    cosf = jnp.tile(cosf, (1, nb, 1))
    sinf = jnp.tile(sinf, (1, nb, 1))
    lg = jnp.log1p(-jnp.exp2(-5.0 - jnp.arange(C_HEADS, dtype=F32)))[:, None]
    idx = jnp.arange(ch, dtype=F32)
    q_dec = jnp.exp(lg * (idx + 1.0))
    k_dec = jnp.exp(lg * (ch - 1.0 - idx))
    rel = idx[:, None] - idx[None, :]
    dm = jnp.where(rel >= 0, jnp.exp(lg[:, :, None] * jnp.maximum(rel, 0.0)), 0.0)
    c_dec = jnp.exp(lg * ch)
    cpg = gl // ch
    expand = lambda v: jnp.repeat(jnp.tile(v.T, (cpg, 1)), C_DIM, axis=1)
    qdec = expand(q_dec)
    kdec = expand(k_dec)
    cdec = jnp.repeat(c_dec.T, C_DIM, axis=1)
    gi = jnp.arange(gl)
    same = (gi[:, None] // ch) == (gi[None, :] // ch)
    dmask = jnp.where(same[None], jnp.tile(dm, (1, cpg, cpg)), 0.0)
    return cosf, sinf, qdec, kdec, cdec, dmask


def _prep_weights(ffn1_w_gu, ffn1_w_down, ffn2_w_gu, ffn2_w_down, even_w_in, even_w_out, odd_w_in, odd_w_out,
                  xattn_wq, xattn_wkv, xattn_wo, tf):
    bf = lambda w: w.astype(_MXU_DTYPE)

    def ffn_w(w_gu, w_down):
        depth, d, two_ff = w_gu.shape
        dff = two_ff // 2
        nc = dff // tf
        wg = bf(w_gu[:, :, :dff]).reshape(depth, d, nc, tf).transpose(0, 2, 1, 3)
        wu = bf(w_gu[:, :, dff:]).reshape(depth, d, nc, tf).transpose(0, 2, 1, 3)
        wd = bf(w_down).reshape(depth, nc, tf, d)
        return wg, wu, wd

    return dict(ffn1=ffn_w(ffn1_w_gu, ffn1_w_down), ffn2=ffn_w(ffn2_w_gu, ffn2_w_down),
                even_in=bf(even_w_in), even_out=bf(even_w_out), odd_in=bf(odd_w_in), odd_out=bf(odd_w_out),
                wq=bf(xattn_wq), wkv=bf(xattn_wkv), wo=bf(xattn_wo))


def _trunk(x, pos0, conv_in, hg_in, ret_in, rw_in, sh_in, mem_k, mem_v, W, P, cfg):
    b, t, d = x.shape
    depth = P["ffn1_norm"].shape[0]
    lb_all = jnp.cumsum(jax.nn.softmax(P["hgrn_lb"].astype(F32), axis=0), axis=0)
    convs, hgs, rets, rws, shs = [], [], [], [], []
    for l in range(depth):
        jl = l // 2
        wg, wu, wd = W["ffn1"]
        x = _ffn(x.reshape(b * t, d), P["ffn1_norm"][l], wg[l], wu[l], wd[l], None, cfg["tm"]).reshape(b, t, d)
        if l % 2 == 0:
            x, cb, sh = _even_layer(x, P["mix_norm"][l], W["even_in"][jl], W["even_out"][jl], P["conv_w"][jl],
                                    lb_all[jl], P["hgrn_gnorm"][jl], conv_in[jl], hg_in[jl],
                                    cfg["nb"], cfg["tt"], cfg["blk"])
            convs.append(cb)
            hgs.append(sh)
        else:
            tabs = _odd_tables(pos0, t, cfg["nb"], cfg["tt"], cfg["ch"], cfg["gl"])
            rw = {k: P["rwkv_" + k][jl] for k in ("mu", "w0", "w2", "a0", "a2", "g2", "k_k", "k_a", "r_k",
                                                  "lnx_g", "lnx_b")}
            x, sr, sw, ss = _odd_layer(x, P["mix_norm"][l], W["odd_in"][jl], W["odd_out"][jl], tabs,
                                       P["ret_gnorm"][jl], rw, ret_in[jl], rw_in[jl], sh_in[jl],
                                       cfg["nb"], cfg["tt"], cfg["ch"], cfg["gl"])
            rets.append(sr)
            rws.append(sw)
            shs.append(ss)
        n_mem = mem_k.shape[2]
        x = _xattn(x, P["xattn_norm"][l], W["wq"][l], W["wo"][l], mem_k[l].reshape(b, n_mem, d),
                   mem_v[l].reshape(b, n_mem, d), cfg["xnb"], cfg["xtt"])
        wg, wu, wd = W["ffn2"]
        fin = P["final_norm"] if l == depth - 1 else None
        x = _ffn(x.reshape(b * t, d), P["ffn2_norm"][l], wg[l], wu[l], wd[l], fin, cfg["tm"]).reshape(b, t, d)
    return x, jnp.stack(convs), jnp.stack(hgs), jnp.stack(rets), jnp.stack(rws), jnp.stack(shs)


def _configs(b, t):
    if t >= 256:
        tt = 256
        return dict(tm=512, nb=1, tt=tt, blk=16, ch=64, gl=64, xnb=1, xtt=min(t, 512))
    nb = min(b, 128 // t)
    return dict(tm=min(b * t, 512), nb=nb, tt=t, blk=t, ch=t, gl=nb * t, xnb=min(b, 8), xtt=t)


def kernel(x_prompt, x_sample, state_conv, state_hgrn, state_ret, state_rwkv, state_shift, cache_mem_k, cache_mem_v, mem_prompt, ffn1_norm, ffn1_w_gu, ffn1_w_down, mix_norm, even_w_in, even_w_out, conv_w, hgrn_lb, hgrn_gnorm, odd_w_in, odd_w_out, ret_gnorm, rwkv_mu, rwkv_w0, rwkv_w2, rwkv_a0, rwkv_a2, rwkv_g2, rwkv_k_k, rwkv_k_a, rwkv_r_k, rwkv_lnx_g, rwkv_lnx_b, xattn_norm, mem_norm, xattn_wq, xattn_wkv, xattn_wo, ffn2_norm, ffn2_w_gu, ffn2_w_down, final_norm):
    P = dict(ffn1_norm=ffn1_norm, mix_norm=mix_norm, conv_w=conv_w, hgrn_lb=hgrn_lb, hgrn_gnorm=hgrn_gnorm,
             ret_gnorm=ret_gnorm, rwkv_mu=rwkv_mu, rwkv_w0=rwkv_w0, rwkv_w2=rwkv_w2, rwkv_a0=rwkv_a0,
             rwkv_a2=rwkv_a2, rwkv_g2=rwkv_g2, rwkv_k_k=rwkv_k_k, rwkv_k_a=rwkv_k_a,
             rwkv_r_k=rwkv_r_k.reshape(rwkv_r_k.shape[0], -1), rwkv_lnx_g=rwkv_lnx_g, rwkv_lnx_b=rwkv_lnx_b,
             xattn_norm=xattn_norm, ffn2_norm=ffn2_norm, final_norm=final_norm)
    W = _prep_weights(ffn1_w_gu, ffn1_w_down, ffn2_w_gu, ffn2_w_down, even_w_in, even_w_out, odd_w_in,
                      odd_w_out, xattn_wq, xattn_wkv, xattn_wo, tf=256)
    bp, tp, d = x_prompt.shape
    bs, ts, _ = x_sample.shape
    depth = ffn1_norm.shape[0]
    n_mem = mem_prompt.shape[1]
    hdx = d // X_HEADS

    mem2d = mem_prompt.reshape(bp * n_mem, d)
    mks, mvs = [], []
    for l in range(depth):
        kv = _norm_matmul(mem2d, mem_norm[l], W["wkv"][l], 256)
        mks.append(kv[:, :d].reshape(bp, n_mem, X_HEADS, hdx))
        mvs.append(kv[:, d:].reshape(bp, n_mem, X_HEADS, hdx))
    mem_k_p = jnp.stack(mks)
    mem_v_p = jnp.stack(mvs)

    z = lambda ref: jnp.zeros((ref.shape[0], bp) + ref.shape[2:], F32)
    y_p, conv_p, hg_p, ret_p, rw_p, sh_p = _trunk(
        x_prompt, 0, z(state_conv), z(state_hgrn), z(state_ret), z(state_rwkv), z(state_shift),
        mem_k_p, mem_v_p, W, P, _configs(bp, tp))
    past_len = 16384
    y_s, conv_s, hg_s, ret_s, rw_s, sh_s = _trunk(
        x_sample, past_len, state_conv, state_hgrn, state_ret, state_rwkv, state_shift,
        cache_mem_k, cache_mem_v, W, P, _configs(bs, ts))
    return (y_p, y_s, conv_p, hg_p, ret_p, rw_p, sh_p, mem_k_p, mem_v_p, conv_s, hg_s, ret_s, rw_s, sh_s)
```

```python
import functools
import math

import jax
import jax.numpy as jnp
from jax import lax
from jax.experimental import pallas as pl
from jax.experimental.pallas import tpu as pltpu

F32 = jnp.float32
_MXU_DTYPE = jnp.bfloat16

NORM_EPS = 1e-6
RWKV_GN_EPS = 64e-5
ROPE_BASE = 10000.0

_VMEM_LIMIT_BYTES = 56 * 1024 * 1024
_SUBLANES = 8
_PACKED_ROWS = 16

A_WIDTH = 512
B_HEADS, B_DIM = 4, 128
C_HEADS, C_DIM = 4, 128
D_HEADS, D_HDIM = 8, 64
D_WIDTH = D_HEADS * D_HDIM
EVEN_IN = 7 * 512
C_IN = 4 * 512
D_IN = 3 * 512 + 64 + 64 + 128
ODD_IN = C_IN + D_IN
X_HEADS = 4


def _mm(a, b):
    return jnp.dot(a.astype(_MXU_DTYPE), b.astype(_MXU_DTYPE), preferred_element_type=F32)


def _mm_nt(a, b):
    return lax.dot_general(a.astype(_MXU_DTYPE), b.astype(_MXU_DTYPE), (((1,), (1,)), ((), ())),
                           preferred_element_type=F32)


def _mm_tn(a, b):
    k = a.shape[0]
    if k % _PACKED_ROWS:
        pad = _PACKED_ROWS - k % _PACKED_ROWS
        a = jnp.concatenate([a, jnp.zeros((pad, a.shape[1]), a.dtype)], axis=0)
        b = jnp.concatenate([b, jnp.zeros((pad, b.shape[1]), b.dtype)], axis=0)
    return lax.dot_general(a.astype(_MXU_DTYPE), b.astype(_MXU_DTYPE), (((0,), (0,)), ((), ())),
                           preferred_element_type=F32)


def _split2(a):
    hi = a.astype(_MXU_DTYPE)
    lo = (a - hi.astype(F32)).astype(_MXU_DTYPE)
    return hi, lo


def _mm_hi(a, b):
    ah, al = _split2(a)
    bh, bl = _split2(b)
    d = functools.partial(jnp.dot, preferred_element_type=F32)
    return d(ah, bh) + d(ah, bl) + d(al, bh)


_BATCH_NN = (((2,), (1,)), ((0,), (0,)))
_BATCH_NT = (((2,), (2,)), ((0,), (0,)))


def _bmm(a, b):
    return lax.dot_general(a.astype(_MXU_DTYPE), b.astype(_MXU_DTYPE), _BATCH_NN, preferred_element_type=F32)


def _bmm_nt(a, b):
    return lax.dot_general(a.astype(_MXU_DTYPE), b.astype(_MXU_DTYPE), _BATCH_NT, preferred_element_type=F32)


def _bmm_hi(a, b):
    ah, al = _split2(a)
    bh, bl = _split2(b)
    d = functools.partial(lax.dot_general, dimension_numbers=_BATCH_NN, preferred_element_type=F32)
    return d(ah, bh) + d(ah, bl) + d(al, bh)


def _mm_exact_lhs(m01, x):
    m = m01.astype(_MXU_DTYPE)
    x0 = x.astype(_MXU_DTYPE)
    r1 = x - x0.astype(F32)
    x1 = r1.astype(_MXU_DTYPE)
    x2 = (r1 - x1.astype(F32)).astype(_MXU_DTYPE)
    d = functools.partial(jnp.dot, preferred_element_type=F32)
    return d(m, x0) + d(m, x1) + d(m, x2)


def _mm_exact_rhs(x, m01):
    m = m01.astype(_MXU_DTYPE)
    x0 = x.astype(_MXU_DTYPE)
    r1 = x - x0.astype(F32)
    x1 = r1.astype(_MXU_DTYPE)
    x2 = (r1 - x1.astype(F32)).astype(_MXU_DTYPE)
    d = functools.partial(jnp.dot, preferred_element_type=F32)
    return d(x0, m) + d(x1, m) + d(x2, m)


def _rms(x, g):
    return x * lax.rsqrt(jnp.mean(x * x, axis=-1, keepdims=True) + NORM_EPS) * g


def _sigmoid(x):
    return 1.0 / (1.0 + jnp.exp(-x))


def _silu(x):
    return x * _sigmoid(x)


def _softplus(x):
    return jnp.maximum(x, 0.0) + jnp.log1p(jnp.exp(-jnp.abs(x)))


def _block_tri(n, blk, strict=False):
    r = lax.broadcasted_iota(jnp.int32, (n, n), 0)
    c = lax.broadcasted_iota(jnp.int32, (n, n), 1)
    same = (r // blk) == (c // blk)
    low = (c < r) if strict else (c <= r)
    return same & low


def _head_ones(width, hd):
    r = lax.broadcasted_iota(jnp.int32, (width, width), 0)
    c = lax.broadcasted_iota(jnp.int32, (width, width), 1)
    return ((r // hd) == (c // hd)).astype(F32)


def _ffn_kernel(x_ref, g_ref, wgu_ref, wd_ref, fg_ref, o_ref, h_scr, acc_scr, *, tf, final):
    dff = wd_ref.shape[0]
    x = x_ref[...]
    h_scr[...] = _rms(x, g_ref[...]).astype(_MXU_DTYPE)
    for c in range(dff // tf):
        h = h_scr[...]
        gate = jnp.dot(h, wgu_ref[:, c * tf:(c + 1) * tf], preferred_element_type=F32)
        up = jnp.dot(h, wgu_ref[:, dff + c * tf:dff + (c + 1) * tf], preferred_element_type=F32)
        act = (_silu(gate) * up).astype(_MXU_DTYPE)
        part = jnp.dot(act, wd_ref[c * tf:(c + 1) * tf, :], preferred_element_type=F32)
        if c == 0:
            acc_scr[...] = part
        else:
            acc_scr[...] += part
    y = x + 0.5 * acc_scr[...]
    if final:
        y = _rms(y, fg_ref[...])
    o_ref[...] = y


def _const_spec(shape):
    nd = len(shape)
    return pl.BlockSpec(shape, lambda *_: (0,) * nd, pipeline_mode=pl.Buffered(1))


def _layer_spec(stacked_shape, layer):
    nd = len(stacked_shape) - 1
    return pl.BlockSpec((None,) + tuple(stacked_shape[1:]), lambda *_: (layer,) + (0,) * nd,
                        pipeline_mode=pl.Buffered(1))


def _ffn(x2d, g, wgu, wd, layer, final_g, tm, tf):
    n, d = x2d.shape
    final = final_g is not None
    fg = final_g if final else g
    return pl.pallas_call(
        functools.partial(_ffn_kernel, tf=tf, final=final),
        out_shape=jax.ShapeDtypeStruct((n, d), F32),
        grid=(n // tm,),
        in_specs=[
            pl.BlockSpec((tm, d), lambda i: (i, 0)),
            _const_spec((1, d)),
            _layer_spec(wgu.shape, layer),
            _layer_spec(wd.shape, layer),
            _const_spec((1, d)),
        ],
        out_specs=pl.BlockSpec((tm, d), lambda i: (i, 0)),
        scratch_shapes=[pltpu.VMEM((tm, d), _MXU_DTYPE), pltpu.VMEM((tm, d), F32)],
        compiler_params=pltpu.CompilerParams(dimension_semantics=("parallel",),
                                             vmem_limit_bytes=_VMEM_LIMIT_BYTES),
        name="ffn",
    )(x2d, g.reshape(1, d), wgu, wd, fg.reshape(1, d))


def _mem_kv_kernel(x_ref, g_ref, w_ref, k_ref, v_ref):
    d = x_ref.shape[-1]
    kv = _mm(_rms(x_ref[...], g_ref[...]), w_ref[...])
    k_ref[...] = kv[:, :d]
    v_ref[...] = kv[:, d:]


def _mem_kv(x2d, g, w, layer, tm):
    n, d = x2d.shape
    row_spec = pl.BlockSpec((tm, d), lambda i: (i, 0))
    return pl.pallas_call(
        _mem_kv_kernel,
        out_shape=(jax.ShapeDtypeStruct((n, d), F32), jax.ShapeDtypeStruct((n, d), F32)),
        grid=(n // tm,),
        in_specs=[row_spec, _const_spec((1, d)), _layer_spec(w.shape, layer)],
        out_specs=(row_spec, row_spec),
        compiler_params=pltpu.CompilerParams(dimension_semantics=("parallel",),
                                             vmem_limit_bytes=_VMEM_LIMIT_BYTES),
        name="mem_kv",
    )(x2d, g.reshape(1, d), w)


def _xattn_kernel(x_ref, g_ref, wq_ref, wo_ref, mk_ref, mv_ref, o_ref, q_scr, a_scr, *, nb, tt, n_heads):
    d = x_ref.shape[-1]
    hd = d // n_heads
    rows = nb * tt
    x = x_ref[...].reshape(rows, d)
    q_scr[...] = _mm(_rms(x, g_ref[...]), wq_ref[...])
    scale = hd ** -0.5

    def seq_body(s, carry):
        r0 = pl.multiple_of(s * tt, _SUBLANES)
        for h in range(n_heads):
            hs = slice(h * hd, (h + 1) * hd)
            qh = q_scr[pl.ds(r0, tt), hs]
            sc = _mm_nt(qh, mk_ref[s, :, hs]) * scale
            e = jnp.exp(sc - jnp.max(sc, axis=-1, keepdims=True))
            pr = e / jnp.sum(e, axis=-1, keepdims=True)
            a_scr[pl.ds(r0, tt), hs] = _mm(pr, mv_ref[s, :, hs])
        return carry

    lax.fori_loop(0, nb, seq_body, 0)
    o_ref[...] = (x + _mm(a_scr[...], wo_ref[...])).reshape(nb, tt, d)


def _xattn(x, g, wq, wo, mk, mv, layer, nb, tt):
    b, t, d = x.shape
    n_mem = mk.shape[2]
    rows = nb * tt
    mem_spec = pl.BlockSpec((None, nb, n_mem, d), lambda i, j: (layer, i, 0, 0))
    return pl.pallas_call(
        functools.partial(_xattn_kernel, nb=nb, tt=tt, n_heads=X_HEADS),
        out_shape=jax.ShapeDtypeStruct((b, t, d), F32),
        grid=(b // nb, t // tt),
        in_specs=[
            pl.BlockSpec((nb, tt, d), lambda i, j: (i, j, 0)),
            _const_spec((1, d)),
            _layer_spec(wq.shape, layer),
            _layer_spec(wo.shape, layer),
            mem_spec,
            mem_spec,
        ],
        out_specs=pl.BlockSpec((nb, tt, d), lambda i, j: (i, j, 0)),
        scratch_shapes=[pltpu.VMEM((rows, d), F32), pltpu.VMEM((rows, d), F32)],
        compiler_params=pltpu.CompilerParams(dimension_semantics=("parallel", "parallel"),
                                             vmem_limit_bytes=_VMEM_LIMIT_BYTES),
        name="xattn",
    )(x, g.reshape(1, d), wq, wo, mk, mv)


def _even_kernel(x_ref, g_ref, win_ref, wout_ref, cw_ref, lb_ref, gn_ref, conv_in_ref, hg_in_ref,
                 xo_ref, conv_out_ref, hg_out_ref,
                 p_scr, y_scr, ext_scr, b_scr, qg_scr, kk_scr, st_scr, *, nb, tt, blk):
    j = pl.program_id(1)
    nj = pl.num_programs(1)
    d = x_ref.shape[-1]
    rows = nb * tt
    aw = A_WIDTH
    hd = B_DIM

    @pl.when(j == 0)
    def _():
        conv_out_ref[...] = conv_in_ref[...]

        def init(s, carry):
            for h in range(B_HEADS):
                st_scr[s, h] = hg_in_ref[s, h].T
            return carry

        lax.fori_loop(0, nb, init, 0)

    x = x_ref[...].reshape(rows, d)
    p_scr[...] = _mm(_rms(x, g_ref[...]), win_ref[...])

    cw = cw_ref[...]

    def conv_body(s, carry):
        r0 = pl.multiple_of(s * tt, _SUBLANES)
        u = p_scr[pl.ds(r0, tt), 2 * aw:3 * aw] * p_scr[pl.ds(r0, tt), 0:aw]
        ext_scr[_SUBLANES:_SUBLANES + tt, :] = u
        ext_scr[_SUBLANES - 2:_SUBLANES, :] = conv_out_ref[s]
        conv = (cw[0:1] * ext_scr[_SUBLANES - 2:_SUBLANES - 2 + tt, :]
                + cw[1:2] * ext_scr[_SUBLANES - 1:_SUBLANES - 1 + tt, :]
                + cw[2:3] * ext_scr[_SUBLANES:_SUBLANES + tt, :])
        y_scr[pl.ds(r0, tt), 0:aw] = p_scr[pl.ds(r0, tt), aw:2 * aw] * conv
        conv_out_ref[s] = ext_scr[_SUBLANES + tt - 2:_SUBLANES + tt, :]
        return carry

    lax.fori_loop(0, nb, conv_body, 0)

    lb = lb_ref[...]
    f = lb + (1.0 - lb) * _sigmoid(p_scr[:, 4 * aw:5 * aw])
    kk_scr[...] = 1.0 - f
    b_scr[...] = _mm_exact_lhs(_block_tri(rows, blk).astype(F32), jnp.log(f))
    qg_scr[...] = _silu(p_scr[:, 3 * aw:4 * aw])
    rowi = lax.broadcasted_iota(jnp.int32, (blk, hd), 0)
    nblk = tt // blk

    def blk_body(i, carry):
        s = i // nblk
        r0 = pl.multiple_of(i * blk, _SUBLANES)
        rs = pl.ds(r0, blk)
        for h in range(B_HEADS):
            hs = slice(h * hd, (h + 1) * hd)
            bb = b_scr[rs, hs]
            blast = bb[blk - 1:blk]
            qg = qg_scr[rs, hs]
            kkb = kk_scr[rs, hs]
            vb = p_scr[rs, 5 * aw + h * hd:5 * aw + (h + 1) * hd]
            st = st_scr[s, h]
            o = _mm_nt(qg * jnp.exp(bb), st)
            for jj in range(blk):
                dec = jnp.exp(jnp.where(rowi >= jj, bb - bb[jj:jj + 1], -jnp.inf))
                att = jnp.sum(qg * kkb[jj:jj + 1] * dec, axis=-1, keepdims=True)
                o = o + att * vb[jj:jj + 1]
            kd = kkb * jnp.exp(blast - bb)
            st_scr[s, h] = st * jnp.exp(blast) + _mm_tn(vb, kd)
            on = o * lax.rsqrt(jnp.mean(o * o, axis=-1, keepdims=True) + NORM_EPS) * gn_ref[:, hs]
            gb = p_scr[rs, 6 * aw + h * hd:6 * aw + (h + 1) * hd]
            y_scr[rs, aw + h * hd:aw + (h + 1) * hd] = on * _silu(gb)
        return carry

    lax.fori_loop(0, nb * nblk, blk_body, 0)

    xo_ref[...] = (x + _mm(y_scr[...], wout_ref[...])).reshape(nb, tt, d)

    @pl.when(j == nj - 1)
    def _():
        def fin(s, carry):
            for h in range(B_HEADS):
                hg_out_ref[s, h] = st_scr[s, h].T
            return carry

        lax.fori_loop(0, nb, fin, 0)


def _even_layer(x, g, win, wout, cw, lb, gn, conv_in, hg_in, nb, tt, blk):
    b, t, d = x.shape
    rows = nb * tt
    return pl.pallas_call(
        functools.partial(_even_kernel, nb=nb, tt=tt, blk=blk),
        out_shape=(jax.ShapeDtypeStruct((b, t, d), F32),
                   jax.ShapeDtypeStruct(conv_in.shape, F32),
                   jax.ShapeDtypeStruct(hg_in.shape, F32)),
        grid=(b // nb, t // tt),
        in_specs=[
            pl.BlockSpec((nb, tt, d), lambda i, j: (i, j, 0)),
            _const_spec((1, d)),
            _const_spec(win.shape),
            _const_spec(wout.shape),
            _const_spec(cw.shape),
            _const_spec((1, A_WIDTH)),
            _const_spec((1, B_HEADS * B_DIM)),
            pl.BlockSpec((nb,) + conv_in.shape[1:], lambda i, j: (i, 0, 0)),
            pl.BlockSpec((nb,) + hg_in.shape[1:], lambda i, j: (i, 0, 0, 0), pipeline_mode=pl.Buffered(1)),
        ],
        out_specs=(
            pl.BlockSpec((nb, tt, d), lambda i, j: (i, j, 0)),
            pl.BlockSpec((nb,) + conv_in.shape[1:], lambda i, j: (i, 0, 0)),
            pl.BlockSpec((nb,) + hg_in.shape[1:], lambda i, j: (i, 0, 0, 0)),
        ),
        scratch_shapes=[
            pltpu.VMEM((rows, EVEN_IN), F32),
            pltpu.VMEM((rows, 2 * A_WIDTH), F32),
            pltpu.VMEM((tt + _SUBLANES, A_WIDTH), F32),
            pltpu.VMEM((rows, A_WIDTH), F32),
            pltpu.VMEM((rows, A_WIDTH), F32),
            pltpu.VMEM((rows, A_WIDTH), F32),
            pltpu.VMEM((nb, B_HEADS, B_DIM, B_DIM), F32),
        ],
        compiler_params=pltpu.CompilerParams(dimension_semantics=("parallel", "arbitrary"),
                                             vmem_limit_bytes=_VMEM_LIMIT_BYTES),
        name="even_layer",
    )(x, g.reshape(1, d), win, wout, cw, lb.reshape(1, -1), gn.reshape(1, -1), conv_in, hg_in)


def _odd_kernel(x_ref, g_ref, win_ref, wout_ref, cos_ref, sin_ref, qdec_ref, kdec_ref, cdec_ref, dmask_ref,
                rgn_ref, mu_ref, w0_ref, w2_ref, a0_ref, a2_ref, g2_ref, kk_ref, ka_ref, rk_ref, lng_ref,
                lnb_ref, ret_in_ref, rw_in_ref, sh_in_ref,
                xo_ref, ret_out_ref, rw_out_ref, sh_out_ref,
                p_scr, y_scr, ext_scr, prev_scr, qr_scr, kr_scr, ir_scr,
                rt_scr, at_scr, bt_scr, kt_scr, vd_scr, cum_scr, od_scr, ah_scr, rh_scr, u0_scr, o0_scr,
                *, nb, tt, ch, gl):
    j = pl.program_id(1)
    d = x_ref.shape[-1]
    rows = nb * tt
    cw = C_HEADS * C_DIM
    hd = C_DIM
    nd = D_HDIM
    n_groups = rows // gl
    cpg = gl // ch
    cps = tt // ch

    @pl.when(j == 0)
    def _():
        ret_out_ref[...] = ret_in_ref[...]
        rw_out_ref[...] = rw_in_ref[...]
        sh_out_ref[...] = sh_in_ref[...]

    x = x_ref[...].reshape(rows, d)
    p_scr[...] = _mm(_rms(x, g_ref[...]), win_ref[...])

    cos = cos_ref[0]
    sin = sin_ref[0]
    for h in range(C_HEADS):
        hs = slice(h * hd, (h + 1) * hd)
        qh = p_scr[:, h * hd:(h + 1) * hd]
        kh = p_scr[:, cw + h * hd:cw + (h + 1) * hd]
        qr_scr[:, hs] = qh * cos + pltpu.roll(qh, hd // 2, axis=1) * sin
        kr_scr[:, hs] = (kh * cos + pltpu.roll(kh, hd // 2, axis=1) * sin) * (C_DIM ** -0.5)

    def ret_group(gi, carry):
        g0 = pl.multiple_of(gi * gl, _SUBLANES)
        gs = pl.ds(g0, gl)

        def ret_chunk(q, c2):
            c = gi * cpg + q
            s = c // cps
            r0 = pl.multiple_of(c * ch, _SUBLANES)
            rs = pl.ds(r0, ch)
            l0 = pl.multiple_of(q * ch, _SUBLANES)
            ls = pl.ds(l0, ch)
            for h in range(C_HEADS):
                hs = slice(h * hd, (h + 1) * hd)
                qc = qr_scr[rs, hs]
                kc = kr_scr[rs, hs]
                vc = p_scr[rs, 2 * cw + h * hd:2 * cw + (h + 1) * hd]
                st = ret_out_ref[s, h]
                ir_scr[rs, hs] = _mm(qc, st) * qdec_ref[ls, hs]
                ret_out_ref[s, h] = cdec_ref[:, hs] * st + _mm_tn(kc * kdec_ref[ls, hs], vc)
            return c2

        lax.fori_loop(0, cpg, ret_chunk, 0)
        for h in range(C_HEADS):
            hs = slice(h * hd, (h + 1) * hd)
            att = _mm_nt(qr_scr[gs, hs], kr_scr[gs, hs]) * dmask_ref[h]
            o = ir_scr[gs, hs] + _mm(att, p_scr[gs, 2 * cw + h * hd:2 * cw + (h + 1) * hd])
            on = o * lax.rsqrt(jnp.mean(o * o, axis=-1, keepdims=True) + NORM_EPS) * rgn_ref[:, hs]
            y_scr[gs, hs] = on * _silu(p_scr[gs, 3 * cw + h * hd:3 * cw + (h + 1) * hd])
        return carry

    lax.fori_loop(0, n_groups, ret_group, 0)

    def shift_body(s, carry):
        r0 = pl.multiple_of(s * tt, _SUBLANES)
        ext_scr[_SUBLANES:_SUBLANES + tt, :] = p_scr[pl.ds(r0, tt), C_IN:ODD_IN]
        ext_scr[_SUBLANES - 1:_SUBLANES, :] = sh_out_ref[s]
        prev_scr[pl.ds(r0, tt), :] = ext_scr[_SUBLANES - 1:_SUBLANES - 1 + tt, :]
        sh_out_ref[s] = ext_scr[_SUBLANES + tt - 1:_SUBLANES + tt, :]
        return carry

    lax.fori_loop(0, nb, shift_body, 0)

    pd = p_scr[:, C_IN:ODD_IN]
    pm = pd + mu_ref[...] * (prev_scr[...] - pd)
    dw = D_WIDTH
    r = pm[:, 0:dw]
    kd = pm[:, dw:2 * dw]
    vd = pm[:, 2 * dw:3 * dw]
    w_dn = pm[:, 3 * dw:3 * dw + 64]
    a_dn = pm[:, 3 * dw + 64:3 * dw + 128]
    g_dn = pm[:, 3 * dw + 128:3 * dw + 256]
    w_log = -_softplus(-(w0_ref[...] + _mm(jnp.tanh(w_dn), w2_ref[...]))) - 0.5
    logdec = -jnp.exp(w_log)
    a = _sigmoid(a0_ref[...] + _mm(a_dn, a2_ref[...]))
    gate = _mm(_sigmoid(g_dn), g2_ref[...])
    ones_h = _head_ones(dw, nd)
    kk0 = kd * kk_ref[...]
    kk = kk0 / jnp.maximum(jnp.sqrt(_mm_exact_rhs(kk0 * kk0, ones_h)), 1e-12)
    kd2 = kd * (1.0 + (a - 1.0) * ka_ref[...])
    cum = _mm_exact_lhs(_block_tri(rows, ch).astype(F32), logdec)
    cum_scr[...] = cum
    rt_scr[...] = r * jnp.exp(cum)
    at_scr[...] = -kk * jnp.exp(cum - logdec)
    e_neg = jnp.exp(-cum)
    bt_scr[...] = kk * a * e_neg
    kt_scr[...] = kd2 * e_neg
    vd_scr[...] = vd
    bonus = _mm_exact_rhs(r * kd2 * rk_ref[...], ones_h) * vd

    strict = _block_tri(gl, ch, strict=True)
    incl = _block_tri(gl, ch)
    eye = (lax.broadcasted_iota(jnp.int32, (gl, gl), 0) == lax.broadcasted_iota(jnp.int32, (gl, gl), 1)).astype(F32)
    n_dbl = int(math.log2(ch))

    def rw_prepare(gi, carry):
        gs = pl.ds(pl.multiple_of(gi * gl, _SUBLANES), gl)
        heads = lambda ref: jnp.stack([ref[gs, h * nd:(h + 1) * nd] for h in range(D_HEADS)])
        at, rt, bt, kt, vv = (heads(ref) for ref in (at_scr, rt_scr, bt_scr, kt_scr, vd_scr))
        n_ab = jnp.where(strict, _bmm_nt(at, bt), 0.0)
        a_ak = jnp.where(strict, _bmm_nt(at, kt), 0.0)
        r_rb = jnp.where(incl, _bmm_nt(rt, bt), 0.0)
        r_rk = jnp.where(incl, _bmm_nt(rt, kt), 0.0)
        xinv = eye + n_ab
        pw = n_ab
        for _ in range(n_dbl - 1):
            pw = _bmm_hi(pw, pw)
            xinv = xinv + _bmm_hi(pw, xinv)
        ah = _bmm_hi(xinv, at)
        u0 = _bmm_hi(xinv, _bmm(a_ak, vv))
        ah_scr[gi] = ah
        u0_scr[gi] = u0
        rh_scr[gi] = rt + _bmm(r_rb, ah)
        o0_scr[gi] = _bmm(r_rb, u0) + _bmm(r_rk, vv)
        return carry

    lax.fori_loop(0, n_groups, rw_prepare, 0)

    def rw_chunk(c, carry):
        gi = c // cpg
        s = c // cps
        r0 = pl.multiple_of(c * ch, _SUBLANES)
        rs = pl.ds(r0, ch)
        ls = pl.ds(pl.multiple_of((c % cpg) * ch, _SUBLANES), ch)
        glast = cum_scr[pl.ds(pl.multiple_of(r0 + ch - _SUBLANES, _SUBLANES), _SUBLANES), :][_SUBLANES - 1:_SUBLANES]
        states = [rw_out_ref[s, h] for h in range(D_HEADS)]
        ars = [_mm_nt(jnp.concatenate([ah_scr[gi, h, ls, :], rh_scr[gi, h, ls, :]], axis=0), states[h])
               for h in range(D_HEADS)]
        us = [ars[h][0:ch] + u0_scr[gi, h, ls, :] for h in range(D_HEADS)]
        ds = []
        for h in range(D_HEADS):
            hs = slice(h * nd, (h + 1) * nd)
            uv = jnp.concatenate([us[h], vd_scr[rs, hs]], axis=0)
            bk = jnp.concatenate([bt_scr[rs, hs], kt_scr[rs, hs]], axis=0)
            ds.append(_mm_tn(uv, bk))
        for h in range(D_HEADS):
            hs = slice(h * nd, (h + 1) * nd)
            rw_out_ref[s, h] = (states[h] + ds[h]) * jnp.exp(glast[:, hs])
            od_scr[rs, hs] = ars[h][ch:2 * ch] + o0_scr[gi, h, ls, :]
        return carry

    lax.fori_loop(0, n_groups * cpg, rw_chunk, 0)

    o_d = od_scr[...]
    mean = _mm_exact_rhs(o_d, ones_h) * (1.0 / nd)
    xc = o_d - mean
    var = _mm_exact_rhs(xc * xc, ones_h) * (1.0 / nd)
    on = xc * lax.rsqrt(var + RWKV_GN_EPS) * lng_ref[...] + lnb_ref[...]
    y_scr[:, cw:cw + dw] = (on + bonus) * gate

    xo_ref[...] = (x + _mm(y_scr[...], wout_ref[...])).reshape(nb, tt, d)


def _odd_layer(x, g, win, wout, tabs, rgn, rw, ret_in, rw_in, sh_in, nb, tt, ch, gl):
    b, t, d = x.shape
    rows = nb * tt
    cos, sin, qdec, kdec, cdec, dmask = tabs
    sh3 = sh_in.reshape(b, 1, D_IN)
    vecs = [rw[k].reshape(1, -1) for k in ("mu", "w0")] + [rw["w2"]] + [rw["a0"].reshape(1, -1), rw["a2"], rw["g2"]] + \
           [rw[k].reshape(1, -1) for k in ("k_k", "k_a", "r_k", "lnx_g", "lnx_b")]
    w512 = pltpu.VMEM((rows, D_WIDTH), F32)
    outs = pl.pallas_call(
        functools.partial(_odd_kernel, nb=nb, tt=tt, ch=ch, gl=gl),
        out_shape=(jax.ShapeDtypeStruct((b, t, d), F32),
                   jax.ShapeDtypeStruct(ret_in.shape, F32),
                   jax.ShapeDtypeStruct(rw_in.shape, F32),
                   jax.ShapeDtypeStruct(sh3.shape, F32)),
        grid=(b // nb, t // tt),
        in_specs=[
            pl.BlockSpec((nb, tt, d), lambda i, j: (i, j, 0)),
            _const_spec((1, d)),
            _const_spec(win.shape),
            _const_spec(wout.shape),
            pl.BlockSpec((1, rows, C_DIM), lambda i, j: (j, 0, 0)),
            pl.BlockSpec((1, rows, C_DIM), lambda i, j: (j, 0, 0)),
            _const_spec(qdec.shape),
            _const_spec(kdec.shape),
            _const_spec(cdec.shape),
            _const_spec(dmask.shape),
            _const_spec((1, C_HEADS * C_DIM)),
        ] + [_const_spec(v.shape) for v in vecs] + [
            pl.BlockSpec((nb,) + ret_in.shape[1:], lambda i, j: (i, 0, 0, 0), pipeline_mode=pl.Buffered(1)),
            pl.BlockSpec((nb,) + rw_in.shape[1:], lambda i, j: (i, 0, 0, 0), pipeline_mode=pl.Buffered(1)),
            pl.BlockSpec((nb, 1, D_IN), lambda i, j: (i, 0, 0)),
        ],
        out_specs=(
            pl.BlockSpec((nb, tt, d), lambda i, j: (i, j, 0)),
            pl.BlockSpec((nb,) + ret_in.shape[1:], lambda i, j: (i, 0, 0, 0)),
            pl.BlockSpec((nb,) + rw_in.shape[1:], lambda i, j: (i, 0, 0, 0)),
            pl.BlockSpec((nb, 1, D_IN), lambda i, j: (i, 0, 0)),
        ),
        scratch_shapes=[
            pltpu.VMEM((rows, ODD_IN), F32),
            pltpu.VMEM((rows, C_HEADS * C_DIM + D_WIDTH), F32),
            pltpu.VMEM((tt + _SUBLANES, D_IN), F32),
            pltpu.VMEM((rows, D_IN), F32),
        ] + [w512] * 10 + [pltpu.VMEM((rows // gl, D_HEADS, gl, D_HDIM), F32)] * 4,
        compiler_params=pltpu.CompilerParams(dimension_semantics=("parallel", "arbitrary"),
                                             vmem_limit_bytes=_VMEM_LIMIT_BYTES),
        name="odd_layer",
    )(x, g.reshape(1, d), win, wout, cos, sin, qdec, kdec, cdec, dmask, rgn.reshape(1, -1), *vecs,
      ret_in, rw_in, sh3)
    xo, ret_o, rw_o, sh_o = outs
    return xo, ret_o, rw_o, sh_o.reshape(b, D_IN)


def _odd_tables(pos0, t, nb, tt, ch, gl):
    half = C_DIM // 2
    inv = ROPE_BASE ** (-jnp.arange(half, dtype=F32) / half)
    pos = pos0 + jnp.arange(t, dtype=jnp.int32)
    ang = pos.astype(F32)[:, None] * inv[None, :]
    cos = jnp.cos(ang)
    sin = jnp.sin(ang)
    cosf = jnp.concatenate([cos, cos], axis=-1).reshape(t // tt, tt, C_DIM)
    sinf = jnp.concatenate([-sin, sin], axis=-1).reshape(t // tt, tt, C_DIM)
    cosf = jnp.tile(cosf, (1, nb, 1))
    sinf = jnp.tile(sinf, (1, nb, 1))
    lg = jnp.log1p(-jnp.exp2(-5.0 - jnp.arange(C_HEADS, dtype=F32)))[:, None]
    idx = jnp.arange(ch, dtype=F32)
    q_dec = jnp.exp(lg * (idx + 1.0))
    k_dec = jnp.exp(lg * (ch - 1.0 - idx))
    rel = idx[:, None] - idx[None, :]
    dm = jnp.where(rel >= 0, jnp.exp(lg[:, :, None] * jnp.maximum(rel, 0.0)), 0.0)
    c_dec = jnp.exp(lg * ch)
    cpg = gl // ch
    expand = lambda v: jnp.repeat(jnp.tile(v.T, (cpg, 1)), C_DIM, axis=1)
    qdec = expand(q_dec)
    kdec = expand(k_dec)
    cdec = jnp.repeat(c_dec.T, C_DIM, axis=1)
    gi = jnp.arange(gl)
    same = (gi[:, None] // ch) == (gi[None, :] // ch)
    dmask = jnp.where(same[None], jnp.tile(dm, (1, cpg, cpg)), 0.0)
    return cosf, sinf, qdec, kdec, cdec, dmask


def _prep_weights(ffn1_w_gu, ffn1_w_down, ffn2_w_gu, ffn2_w_down, even_w_in, even_w_out, odd_w_in, odd_w_out,
                  xattn_wq, xattn_wkv, xattn_wo):
    bf = lambda w: w.astype(_MXU_DTYPE)
    return dict(ffn1=(bf(ffn1_w_gu), bf(ffn1_w_down)), ffn2=(bf(ffn2_w_gu), bf(ffn2_w_down)),
                even_in=bf(even_w_in), even_out=bf(even_w_out), odd_in=bf(odd_w_in), odd_out=bf(odd_w_out),
                wq=bf(xattn_wq), wkv=bf(xattn_wkv), wo=bf(xattn_wo))


def _trunk(x, pos0, conv_in, hg_in, ret_in, rw_in, sh_in, mem_k, mem_v, W, P, cfg):
    b, t, d = x.shape
    depth = P["ffn1_norm"].shape[0]
    lb_all = jnp.cumsum(jax.nn.softmax(P["hgrn_lb"].astype(F32), axis=0), axis=0)
    convs, hgs, rets, rws, shs = [], [], [], [], []
    for l in range(depth):
        jl = l // 2
        wgu, wd = W["ffn1"]
        x = _ffn(x.reshape(b * t, d), P["ffn1_norm"][l], wgu, wd, l, None, cfg["tm"], cfg["tf"]).reshape(b, t, d)
        if l % 2 == 0:
            x, cb, sh = _even_layer(x, P["mix_norm"][l], W["even_in"][jl], W["even_out"][jl], P["conv_w"][jl],
                                    lb_all[jl], P["hgrn_gnorm"][jl], conv_in[jl], hg_in[jl],
                                    cfg["nb"], cfg["tt"], cfg["blk"])
            convs.append(cb)
            hgs.append(sh)
        else:
            tabs = _odd_tables(pos0, t, cfg["nb"], cfg["tt"], cfg["ch"], cfg["gl"])
            rw = {k: P["rwkv_" + k][jl] for k in ("mu", "w0", "w2", "a0", "a2", "g2", "k_k", "k_a", "r_k",
                                                  "lnx_g", "lnx_b")}
            x, sr, sw, ss = _odd_layer(x, P["mix_norm"][l], W["odd_in"][jl], W["odd_out"][jl], tabs,
                                       P["ret_gnorm"][jl], rw, ret_in[jl], rw_in[jl], sh_in[jl],
                                       cfg["nb"], cfg["tt"], cfg["ch"], cfg["gl"])
            rets.append(sr)
            rws.append(sw)
            shs.append(ss)
        x = _xattn(x, P["xattn_norm"][l], W["wq"], W["wo"], mem_k, mem_v, l, cfg["xnb"], cfg["xtt"])
        wgu, wd = W["ffn2"]
        fin = P["final_norm"] if l == depth - 1 else None
        x = _ffn(x.reshape(b * t, d), P["ffn2_norm"][l], wgu, wd, l, fin, cfg["tm"], cfg["tf"]).reshape(b, t, d)
    return x, jnp.stack(convs), jnp.stack(hgs), jnp.stack(rets), jnp.stack(rws), jnp.stack(shs)


def _configs(b, t):
    if t >= 256:
        tt = 256
        return dict(tm=512, tf=256, nb=1, tt=tt, blk=16, ch=64, gl=64, xnb=1, xtt=min(t, 512))
    nb = min(b, 128 // t)
    return dict(tm=min(b * t, 512), tf=256, nb=nb, tt=t, blk=t, ch=t, gl=nb * t, xnb=min(b, 8), xtt=t)


def kernel(x_prompt, x_sample, state_conv, state_hgrn, state_ret, state_rwkv, state_shift, cache_mem_k, cache_mem_v, mem_prompt, ffn1_norm, ffn1_w_gu, ffn1_w_down, mix_norm, even_w_in, even_w_out, conv_w, hgrn_lb, hgrn_gnorm, odd_w_in, odd_w_out, ret_gnorm, rwkv_mu, rwkv_w0, rwkv_w2, rwkv_a0, rwkv_a2, rwkv_g2, rwkv_k_k, rwkv_k_a, rwkv_r_k, rwkv_lnx_g, rwkv_lnx_b, xattn_norm, mem_norm, xattn_wq, xattn_wkv, xattn_wo, ffn2_norm, ffn2_w_gu, ffn2_w_down, final_norm):
    P = dict(ffn1_norm=ffn1_norm, mix_norm=mix_norm, conv_w=conv_w, hgrn_lb=hgrn_lb, hgrn_gnorm=hgrn_gnorm,
             ret_gnorm=ret_gnorm, rwkv_mu=rwkv_mu, rwkv_w0=rwkv_w0, rwkv_w2=rwkv_w2, rwkv_a0=rwkv_a0,
             rwkv_a2=rwkv_a2, rwkv_g2=rwkv_g2, rwkv_k_k=rwkv_k_k, rwkv_k_a=rwkv_k_a,
             rwkv_r_k=rwkv_r_k.reshape(rwkv_r_k.shape[0], -1), rwkv_lnx_g=rwkv_lnx_g, rwkv_lnx_b=rwkv_lnx_b,
             xattn_norm=xattn_norm, ffn2_norm=ffn2_norm, final_norm=final_norm)
    W = _prep_weights(ffn1_w_gu, ffn1_w_down, ffn2_w_gu, ffn2_w_down, even_w_in, even_w_out, odd_w_in,
                      odd_w_out, xattn_wq, xattn_wkv, xattn_wo)
    bp, tp, d = x_prompt.shape
    bs, ts, _ = x_sample.shape
    depth = ffn1_norm.shape[0]
    n_mem = mem_prompt.shape[1]
    hdx = d // X_HEADS

    mem2d = mem_prompt.reshape(bp * n_mem, d)
    mks, mvs = [], []
    for l in range(depth):
        mk, mv = _mem_kv(mem2d, mem_norm[l], W["wkv"], l, 256)
        mks.append(mk.reshape(bp, n_mem, d))
        mvs.append(mv.reshape(bp, n_mem, d))
    mem_k_p = jnp.stack(mks)
    mem_v_p = jnp.stack(mvs)

    z = lambda ref: jnp.zeros((ref.shape[0], bp) + ref.shape[2:], F32)
    y_p, conv_p, hg_p, ret_p, rw_p, sh_p = _trunk(
        x_prompt, 0, z(state_conv), z(state_hgrn), z(state_ret), z(state_rwkv), z(state_shift),
        mem_k_p, mem_v_p, W, P, _configs(bp, tp))
    past_len = 16384
    flat_mem = lambda m: m.reshape(m.shape[:3] + (d,))
    y_s, conv_s, hg_s, ret_s, rw_s, sh_s = _trunk(
        x_sample, past_len, state_conv, state_hgrn, state_ret, state_rwkv, state_shift,
        flat_mem(cache_mem_k), flat_mem(cache_mem_v), W, P, _configs(bs, ts))
    heads = lambda m: m.reshape(m.shape[:3] + (X_HEADS, hdx))
    return (y_p, y_s, conv_p, hg_p, ret_p, rw_p, sh_p, heads(mem_k_p), heads(mem_v_p),
            conv_s, hg_s, ret_s, rw_s, sh_s)
```

```python
import functools
import math

import jax
import jax.numpy as jnp
from jax import lax
from jax.experimental import pallas as pl
from jax.experimental.pallas import tpu as pltpu

F32 = jnp.float32
_MXU_DTYPE = jnp.bfloat16

NORM_EPS = 1e-6
RWKV_GN_EPS = 64e-5
ROPE_BASE = 10000.0

_VMEM_LIMIT_BYTES = 56 * 1024 * 1024
_SUBLANES = 8
_PACKED_ROWS = 16
_MXU_WIDTH = 256

A_WIDTH = 512
B_HEADS, B_DIM = 4, 128
C_HEADS, C_DIM = 4, 128
D_HEADS, D_HDIM = 8, 64
D_WIDTH = D_HEADS * D_HDIM
EVEN_IN = 7 * 512
C_IN = 4 * 512
D_IN = 3 * 512 + 64 + 64 + 128
ODD_IN = C_IN + D_IN
X_HEADS = 4


def _mm(a, b):
    return jnp.dot(a.astype(_MXU_DTYPE), b.astype(_MXU_DTYPE), preferred_element_type=F32)


def _mm_nt(a, b):
    return lax.dot_general(a.astype(_MXU_DTYPE), b.astype(_MXU_DTYPE), (((1,), (1,)), ((), ())),
                           preferred_element_type=F32)


def _mm_tn(a, b):
    k = a.shape[0]
    if k % _PACKED_ROWS:
        pad = _PACKED_ROWS - k % _PACKED_ROWS
        a = jnp.concatenate([a, jnp.zeros((pad, a.shape[1]), a.dtype)], axis=0)
        b = jnp.concatenate([b, jnp.zeros((pad, b.shape[1]), b.dtype)], axis=0)
    return lax.dot_general(a.astype(_MXU_DTYPE), b.astype(_MXU_DTYPE), (((0,), (0,)), ((), ())),
                           preferred_element_type=F32)


def _split2(a):
    hi = a.astype(_MXU_DTYPE)
    lo = (a - hi.astype(F32)).astype(_MXU_DTYPE)
    return hi, lo


def _mm_hi(a, b):
    ah, al = _split2(a)
    bh, bl = _split2(b)
    d = functools.partial(jnp.dot, preferred_element_type=F32)
    return d(ah, bh) + d(ah, bl) + d(al, bh)


_BATCH_NN = (((2,), (1,)), ((0,), (0,)))
_BATCH_NT = (((2,), (2,)), ((0,), (0,)))


def _bmm(a, b):
    return lax.dot_general(a.astype(_MXU_DTYPE), b.astype(_MXU_DTYPE), _BATCH_NN, preferred_element_type=F32)


def _bmm_nt(a, b):
    return lax.dot_general(a.astype(_MXU_DTYPE), b.astype(_MXU_DTYPE), _BATCH_NT, preferred_element_type=F32)


def _bmm_hi(a, b):
    ah, al = _split2(a)
    bh, bl = _split2(b)
    d = functools.partial(lax.dot_general, dimension_numbers=_BATCH_NN, preferred_element_type=F32)
    return d(ah, bh) + d(ah, bl) + d(al, bh)


def _mm_exact_lhs(m01, x):
    m = m01.astype(_MXU_DTYPE)
    x0 = x.astype(_MXU_DTYPE)
    r1 = x - x0.astype(F32)
    x1 = r1.astype(_MXU_DTYPE)
    x2 = (r1 - x1.astype(F32)).astype(_MXU_DTYPE)
    d = functools.partial(jnp.dot, preferred_element_type=F32)
    return d(m, x0) + d(m, x1) + d(m, x2)


def _mm_exact_rhs(x, m01):
    m = m01.astype(_MXU_DTYPE)
    x0 = x.astype(_MXU_DTYPE)
    r1 = x - x0.astype(F32)
    x1 = r1.astype(_MXU_DTYPE)
    x2 = (r1 - x1.astype(F32)).astype(_MXU_DTYPE)
    d = functools.partial(jnp.dot, preferred_element_type=F32)
    return d(x0, m) + d(x1, m) + d(x2, m)


def _rms(x, g):
    return x * lax.rsqrt(jnp.mean(x * x, axis=-1, keepdims=True) + NORM_EPS) * g


def _sigmoid(x):
    return 1.0 / (1.0 + jnp.exp(-x))


def _silu(x):
    return x * _sigmoid(x)


def _softplus(x):
    return jnp.maximum(x, 0.0) + jnp.log1p(jnp.exp(-jnp.abs(x)))


def _block_tri(n, blk, strict=False):
    r = lax.broadcasted_iota(jnp.int32, (n, n), 0)
    c = lax.broadcasted_iota(jnp.int32, (n, n), 1)
    same = (r // blk) == (c // blk)
    low = (c < r) if strict else (c <= r)
    return same & low


def _head_sums(x, hd):
    width = min(_MXU_WIDTH, x.shape[1])
    r = lax.broadcasted_iota(jnp.int32, (width, width), 0)
    c = lax.broadcasted_iota(jnp.int32, (width, width), 1)
    ones = ((r // hd) == (c // hd)).astype(F32)
    parts = [_mm_exact_rhs(x[:, i:i + width], ones) for i in range(0, x.shape[1], width)]
    return parts[0] if len(parts) == 1 else jnp.concatenate(parts, axis=1)


def _ffn_kernel(x_ref, g_ref, wgu_ref, wd_ref, fg_ref, o_ref, h_scr, acc_scr, *, tf, final):
    dff = wd_ref.shape[0]
    x = x_ref[...]
    h_scr[...] = _rms(x, g_ref[...]).astype(_MXU_DTYPE)
    for c in range(dff // tf):
        h = h_scr[...]
        gate = jnp.dot(h, wgu_ref[:, c * tf:(c + 1) * tf], preferred_element_type=F32)
        up = jnp.dot(h, wgu_ref[:, dff + c * tf:dff + (c + 1) * tf], preferred_element_type=F32)
        act = (_silu(gate) * up).astype(_MXU_DTYPE)
        part = jnp.dot(act, wd_ref[c * tf:(c + 1) * tf, :], preferred_element_type=F32)
        if c == 0:
            acc_scr[...] = part
        else:
            acc_scr[...] += part
    y = x + 0.5 * acc_scr[...]
    if final:
        y = _rms(y, fg_ref[...])
    o_ref[...] = y


def _const_spec(shape):
    nd = len(shape)
    return pl.BlockSpec(shape, lambda *_: (0,) * nd, pipeline_mode=pl.Buffered(1))


def _layer_spec(stacked_shape, layer):
    nd = len(stacked_shape) - 1
    return pl.BlockSpec((None,) + tuple(stacked_shape[1:]), lambda *_: (layer,) + (0,) * nd,
                        pipeline_mode=pl.Buffered(1))


def _ffn(x2d, g, wgu, wd, layer, final_g, tm, tf):
    n, d = x2d.shape
    final = final_g is not None
    fg = final_g if final else g
    return pl.pallas_call(
        functools.partial(_ffn_kernel, tf=tf, final=final),
        out_shape=jax.ShapeDtypeStruct((n, d), F32),
        grid=(n // tm,),
        in_specs=[
            pl.BlockSpec((tm, d), lambda i: (i, 0)),
            _const_spec((1, d)),
            _layer_spec(wgu.shape, layer),
            _layer_spec(wd.shape, layer),
            _const_spec((1, d)),
        ],
        out_specs=pl.BlockSpec((tm, d), lambda i: (i, 0)),
        scratch_shapes=[pltpu.VMEM((tm, d), _MXU_DTYPE), pltpu.VMEM((tm, d), F32)],
        compiler_params=pltpu.CompilerParams(dimension_semantics=("parallel",),
                                             vmem_limit_bytes=_VMEM_LIMIT_BYTES),
        name="ffn",
    )(x2d, g.reshape(1, d), wgu, wd, fg.reshape(1, d))


def _mem_kv_kernel(x_ref, g_ref, w_ref, k_ref, v_ref):
    d = x_ref.shape[-1]
    kv = _mm(_rms(x_ref[...], g_ref[...]), w_ref[...])
    k_ref[...] = kv[:, :d]
    v_ref[...] = kv[:, d:]


def _mem_kv(x2d, g, w, layer, tm):
    n, d = x2d.shape
    row_spec = pl.BlockSpec((tm, d), lambda i: (i, 0))
    return pl.pallas_call(
        _mem_kv_kernel,
        out_shape=(jax.ShapeDtypeStruct((n, d), F32), jax.ShapeDtypeStruct((n, d), F32)),
        grid=(n // tm,),
        in_specs=[row_spec, _const_spec((1, d)), _layer_spec(w.shape, layer)],
        out_specs=(row_spec, row_spec),
        compiler_params=pltpu.CompilerParams(dimension_semantics=("parallel",),
                                             vmem_limit_bytes=_VMEM_LIMIT_BYTES),
        name="mem_kv",
    )(x2d, g.reshape(1, d), w)


def _xattn_kernel(x_ref, g_ref, wq_ref, wo_ref, mk_ref, mv_ref, o_ref, q_scr, a_scr, *cache_scr,
                  nb, tt, n_heads, layer, head_split):
    d = x_ref.shape[-1]
    hd = d // n_heads
    rows = nb * tt
    scale = hd ** -0.5

    if head_split:
        kbuf, vbuf, sem = cache_scr
        i = pl.program_id(0)
        slot = i % 2

        def slab_copies(step, to_slot):
            seqs = pl.ds(step * nb, nb)
            return [pltpu.make_async_copy(src.at[layer, seqs, :, h, :], buf.at[to_slot, h], sem.at[to_slot, kv, h])
                    for kv, (src, buf) in enumerate(((mk_ref, kbuf), (mv_ref, vbuf))) for h in range(n_heads)]

        @pl.when(i == 0)
        def _():
            for cp in slab_copies(0, 0):
                cp.start()

        @pl.when(i + 1 < pl.num_programs(0))
        def _():
            for cp in slab_copies(i + 1, 1 - slot):
                cp.start()

        keys = lambda s, h: kbuf[slot, h, s]
        vals = lambda s, h: vbuf[slot, h, s]
    else:
        keys = lambda s, h: mk_ref[s, :, h * hd:(h + 1) * hd]
        vals = lambda s, h: mv_ref[s, :, h * hd:(h + 1) * hd]

    x = x_ref[...].reshape(rows, d)
    q_scr[...] = _mm(_rms(x, g_ref[...]), wq_ref[...])

    if head_split:
        for cp in slab_copies(i, slot):
            cp.wait()

    def seq_body(s, carry):
        r0 = pl.multiple_of(s * tt, _SUBLANES)
        hs = [slice(h * hd, (h + 1) * hd) for h in range(n_heads)]
        scs = [_mm_nt(q_scr[pl.ds(r0, tt), hs[h]], keys(s, h)) * scale for h in range(n_heads)]
        es = [jnp.exp(sc - jnp.max(sc, axis=-1, keepdims=True)) for sc in scs]
        prs = [e / jnp.sum(e, axis=-1, keepdims=True) for e in es]
        for h in range(n_heads):
            a_scr[pl.ds(r0, tt), hs[h]] = _mm(prs[h], vals(s, h))
        return carry

    lax.fori_loop(0, nb, seq_body, 0)
    o_ref[...] = (x + _mm(a_scr[...], wo_ref[...])).reshape(nb, tt, d)


def _xattn(x, g, wq, wo, mk, mv, layer, nb, tt):
    b, t, d = x.shape
    n_mem = mk.shape[2]
    rows = nb * tt
    head_split = mk.ndim == 5
    if head_split:
        assert t == tt and mk.shape[3] == X_HEADS
        mem_spec = pl.BlockSpec(memory_space=pl.ANY)
        slab = pltpu.VMEM((2, X_HEADS, nb, n_mem, d // X_HEADS), F32)
        cache_scr = [slab, slab, pltpu.SemaphoreType.DMA((2, 2, X_HEADS))]
        semantics = ("arbitrary", "arbitrary")
    else:
        mem_spec = pl.BlockSpec((None, nb, n_mem, d), lambda i, j: (layer, i, 0, 0))
        cache_scr = []
        semantics = ("parallel", "parallel")
    return pl.pallas_call(
        functools.partial(_xattn_kernel, nb=nb, tt=tt, n_heads=X_HEADS, layer=layer, head_split=head_split),
        out_shape=jax.ShapeDtypeStruct((b, t, d), F32),
        grid=(b // nb, t // tt),
        in_specs=[
            pl.BlockSpec((nb, tt, d), lambda i, j: (i, j, 0)),
            _const_spec((1, d)),
            _layer_spec(wq.shape, layer),
            _layer_spec(wo.shape, layer),
            mem_spec,
            mem_spec,
        ],
        out_specs=pl.BlockSpec((nb, tt, d), lambda i, j: (i, j, 0)),
        scratch_shapes=[pltpu.VMEM((rows, d), F32), pltpu.VMEM((rows, d), F32)] + cache_scr,
        compiler_params=pltpu.CompilerParams(dimension_semantics=semantics,
                                             vmem_limit_bytes=_VMEM_LIMIT_BYTES),
        name="xattn",
    )(x, g.reshape(1, d), wq, wo, mk, mv)


def _even_kernel(x_ref, g_ref, win_ref, wout_ref, cw_ref, lb_ref, gn_ref, conv_in_ref, hg_in_ref,
                 xo_ref, conv_out_ref, hg_out_ref,
                 p_scr, y_scr, ext_scr, b_scr, qg_scr, kk_scr, st_scr, *, nb, tt, blk):
    j = pl.program_id(1)
    nj = pl.num_programs(1)
    d = x_ref.shape[-1]
    rows = nb * tt
    aw = A_WIDTH
    hd = B_DIM

    @pl.when(j == 0)
    def _():
        conv_out_ref[...] = conv_in_ref[...]

        def init(s, carry):
            for h in range(B_HEADS):
                st_scr[s, h] = hg_in_ref[s, h].T
            return carry

        lax.fori_loop(0, nb, init, 0)

    x = x_ref[...].reshape(rows, d)
    p_scr[...] = _mm(_rms(x, g_ref[...]), win_ref[...])

    cw = cw_ref[...]

    def conv_body(s, carry):
        r0 = pl.multiple_of(s * tt, _SUBLANES)
        u = p_scr[pl.ds(r0, tt), 2 * aw:3 * aw] * p_scr[pl.ds(r0, tt), 0:aw]
        ext_scr[_SUBLANES:_SUBLANES + tt, :] = u
        ext_scr[_SUBLANES - 2:_SUBLANES, :] = conv_out_ref[s]
        conv = (cw[0:1] * ext_scr[_SUBLANES - 2:_SUBLANES - 2 + tt, :]
                + cw[1:2] * ext_scr[_SUBLANES - 1:_SUBLANES - 1 + tt, :]
                + cw[2:3] * ext_scr[_SUBLANES:_SUBLANES + tt, :])
        y_scr[pl.ds(r0, tt), 0:aw] = p_scr[pl.ds(r0, tt), aw:2 * aw] * conv
        conv_out_ref[s] = ext_scr[_SUBLANES + tt - 2:_SUBLANES + tt, :]
        return carry

    lax.fori_loop(0, nb, conv_body, 0)

    lb = lb_ref[...]
    f = lb + (1.0 - lb) * _sigmoid(p_scr[:, 4 * aw:5 * aw])
    kk_scr[...] = 1.0 - f
    b_scr[...] = _mm_exact_lhs(_block_tri(rows, blk).astype(F32), jnp.log(f))
    qg_scr[...] = _silu(p_scr[:, 3 * aw:4 * aw])
    rowi = lax.broadcasted_iota(jnp.int32, (blk, hd), 0)
    nblk = tt // blk

    def blk_body(i, carry):
        s = i // nblk
        r0 = pl.multiple_of(i * blk, _SUBLANES)
        rs = pl.ds(r0, blk)
        for h in range(B_HEADS):
            hs = slice(h * hd, (h + 1) * hd)
            bb = b_scr[rs, hs]
            blast = bb[blk - 1:blk]
            qg = qg_scr[rs, hs]
            kkb = kk_scr[rs, hs]
            vb = p_scr[rs, 5 * aw + h * hd:5 * aw + (h + 1) * hd]
            st = st_scr[s, h]
            o = _mm_nt(qg * jnp.exp(bb), st)
            for jj in range(blk):
                dec = jnp.exp(jnp.where(rowi >= jj, bb - bb[jj:jj + 1], -jnp.inf))
                att = jnp.sum(qg * kkb[jj:jj + 1] * dec, axis=-1, keepdims=True)
                o = o + att * vb[jj:jj + 1]
            kd = kkb * jnp.exp(blast - bb)
            st_scr[s, h] = st * jnp.exp(blast) + _mm_tn(vb, kd)
            on = o * lax.rsqrt(jnp.mean(o * o, axis=-1, keepdims=True) + NORM_EPS) * gn_ref[:, hs]
            gb = p_scr[rs, 6 * aw + h * hd:6 * aw + (h + 1) * hd]
            y_scr[rs, aw + h * hd:aw + (h + 1) * hd] = on * _silu(gb)
        return carry

    lax.fori_loop(0, nb * nblk, blk_body, 0)

    xo_ref[...] = (x + _mm(y_scr[...], wout_ref[...])).reshape(nb, tt, d)

    @pl.when(j == nj - 1)
    def _():
        def fin(s, carry):
            for h in range(B_HEADS):
                hg_out_ref[s, h] = st_scr[s, h].T
            return carry

        lax.fori_loop(0, nb, fin, 0)


def _even_layer(x, g, win, wout, cw, lb, gn, conv_in, hg_in, nb, tt, blk):
    b, t, d = x.shape
    rows = nb * tt
    return pl.pallas_call(
        functools.partial(_even_kernel, nb=nb, tt=tt, blk=blk),
        out_shape=(jax.ShapeDtypeStruct((b, t, d), F32),
                   jax.ShapeDtypeStruct(conv_in.shape, F32),
                   jax.ShapeDtypeStruct(hg_in.shape, F32)),
        grid=(b // nb, t // tt),
        in_specs=[
            pl.BlockSpec((nb, tt, d), lambda i, j: (i, j, 0)),
            _const_spec((1, d)),
            _const_spec(win.shape),
            _const_spec(wout.shape),
            _const_spec(cw.shape),
            _const_spec((1, A_WIDTH)),
            _const_spec((1, B_HEADS * B_DIM)),
            pl.BlockSpec((nb,) + conv_in.shape[1:], lambda i, j: (i, 0, 0)),
            pl.BlockSpec((nb,) + hg_in.shape[1:], lambda i, j: (i, 0, 0, 0), pipeline_mode=pl.Buffered(1)),
        ],
        out_specs=(
            pl.BlockSpec((nb, tt, d), lambda i, j: (i, j, 0)),
            pl.BlockSpec((nb,) + conv_in.shape[1:], lambda i, j: (i, 0, 0)),
            pl.BlockSpec((nb,) + hg_in.shape[1:], lambda i, j: (i, 0, 0, 0)),
        ),
        scratch_shapes=[
            pltpu.VMEM((rows, EVEN_IN), F32),
            pltpu.VMEM((rows, 2 * A_WIDTH), F32),
            pltpu.VMEM((tt + _SUBLANES, A_WIDTH), F32),
            pltpu.VMEM((rows, A_WIDTH), F32),
            pltpu.VMEM((rows, A_WIDTH), F32),
            pltpu.VMEM((rows, A_WIDTH), F32),
            pltpu.VMEM((nb, B_HEADS, B_DIM, B_DIM), F32),
        ],
        compiler_params=pltpu.CompilerParams(dimension_semantics=("parallel", "arbitrary"),
                                             vmem_limit_bytes=_VMEM_LIMIT_BYTES),
        name="even_layer",
    )(x, g.reshape(1, d), win, wout, cw, lb.reshape(1, -1), gn.reshape(1, -1), conv_in, hg_in)


def _odd_kernel(x_ref, g_ref, win_ref, wout_ref, cos_ref, sin_ref, qdec_ref, kdec_ref, cdec_ref, dmask_ref,
                rgn_ref, mu_ref, w0_ref, w2_ref, a0_ref, a2_ref, g2_ref, kk_ref, ka_ref, rk_ref, lng_ref,
                lnb_ref, ret_in_ref, rw_in_ref, sh_in_ref,
                xo_ref, ret_out_ref, rw_out_ref, sh_out_ref,
                p_scr, y_scr, ext_scr, prev_scr, qr_scr, kr_scr, ir_scr,
                rt_scr, at_scr, bt_scr, kt_scr, vd_scr, cum_scr, od_scr, ah_scr, rh_scr, u0_scr, o0_scr,
                *, nb, tt, ch, gl):
    j = pl.program_id(1)
    d = x_ref.shape[-1]
    rows = nb * tt
    cw = C_HEADS * C_DIM
    hd = C_DIM
    nd = D_HDIM
    n_groups = rows // gl
    cpg = gl // ch
    cps = tt // ch

    @pl.when(j == 0)
    def _():
        ret_out_ref[...] = ret_in_ref[...]
        rw_out_ref[...] = rw_in_ref[...]
        sh_out_ref[...] = sh_in_ref[...]

    x = x_ref[...].reshape(rows, d)
    p_scr[...] = _mm(_rms(x, g_ref[...]), win_ref[...])

    cos = cos_ref[0]
    sin = sin_ref[0]
    for h in range(C_HEADS):
        hs = slice(h * hd, (h + 1) * hd)
        qh = p_scr[:, h * hd:(h + 1) * hd]
        kh = p_scr[:, cw + h * hd:cw + (h + 1) * hd]
        qr_scr[:, hs] = qh * cos + pltpu.roll(qh, hd // 2, axis=1) * sin
        kr_scr[:, hs] = (kh * cos + pltpu.roll(kh, hd // 2, axis=1) * sin) * (C_DIM ** -0.5)

    def ret_group(gi, carry):
        g0 = pl.multiple_of(gi * gl, _SUBLANES)
        gs = pl.ds(g0, gl)

        def ret_chunk(q, c2):
            c = gi * cpg + q
            s = c // cps
            r0 = pl.multiple_of(c * ch, _SUBLANES)
            rs = pl.ds(r0, ch)
            l0 = pl.multiple_of(q * ch, _SUBLANES)
            ls = pl.ds(l0, ch)
            for h in range(C_HEADS):
                hs = slice(h * hd, (h + 1) * hd)
                qc = qr_scr[rs, hs]
                kc = kr_scr[rs, hs]
                vc = p_scr[rs, 2 * cw + h * hd:2 * cw + (h + 1) * hd]
                st = ret_out_ref[s, h]
                ir_scr[rs, hs] = _mm(qc, st) * qdec_ref[ls, hs]
                ret_out_ref[s, h] = cdec_ref[:, hs] * st + _mm_tn(kc * kdec_ref[ls, hs], vc)
            return c2

        lax.fori_loop(0, cpg, ret_chunk, 0)
        for h in range(C_HEADS):
            hs = slice(h * hd, (h + 1) * hd)
            att = _mm_nt(qr_scr[gs, hs], kr_scr[gs, hs]) * dmask_ref[h]
            o = ir_scr[gs, hs] + _mm(att, p_scr[gs, 2 * cw + h * hd:2 * cw + (h + 1) * hd])
            on = o * lax.rsqrt(jnp.mean(o * o, axis=-1, keepdims=True) + NORM_EPS) * rgn_ref[:, hs]
            y_scr[gs, hs] = on * _silu(p_scr[gs, 3 * cw + h * hd:3 * cw + (h + 1) * hd])
        return carry

    lax.fori_loop(0, n_groups, ret_group, 0)

    def shift_body(s, carry):
        r0 = pl.multiple_of(s * tt, _SUBLANES)
        ext_scr[_SUBLANES:_SUBLANES + tt, :] = p_scr[pl.ds(r0, tt), C_IN:ODD_IN]
        ext_scr[_SUBLANES - 1:_SUBLANES, :] = sh_out_ref[s]
        prev_scr[pl.ds(r0, tt), :] = ext_scr[_SUBLANES - 1:_SUBLANES - 1 + tt, :]
        sh_out_ref[s] = ext_scr[_SUBLANES + tt - 1:_SUBLANES + tt, :]
        return carry

    lax.fori_loop(0, nb, shift_body, 0)

    pd = p_scr[:, C_IN:ODD_IN]
    pm = pd + mu_ref[...] * (prev_scr[...] - pd)
    dw = D_WIDTH
    r = pm[:, 0:dw]
    kd = pm[:, dw:2 * dw]
    vd = pm[:, 2 * dw:3 * dw]
    w_dn = pm[:, 3 * dw:3 * dw + 64]
    a_dn = pm[:, 3 * dw + 64:3 * dw + 128]
    g_dn = pm[:, 3 * dw + 128:3 * dw + 256]
    w_log = -_softplus(-(w0_ref[...] + _mm(jnp.tanh(w_dn), w2_ref[...]))) - 0.5
    logdec = -jnp.exp(w_log)
    a = _sigmoid(a0_ref[...] + _mm(a_dn, a2_ref[...]))
    gate = _mm(_sigmoid(g_dn), g2_ref[...])
    kk0 = kd * kk_ref[...]
    kk = kk0 / jnp.maximum(jnp.sqrt(_head_sums(kk0 * kk0, nd)), 1e-12)
    kd2 = kd * (1.0 + (a - 1.0) * ka_ref[...])
    cum = _mm_exact_lhs(_block_tri(rows, ch).astype(F32), logdec)
    cum_scr[...] = cum
    rt_scr[...] = r * jnp.exp(cum)
    at_scr[...] = -kk * jnp.exp(cum - logdec)
    e_neg = jnp.exp(-cum)
    bt_scr[...] = kk * a * e_neg
    kt_scr[...] = kd2 * e_neg
    vd_scr[...] = vd
    bonus = _head_sums(r * kd2 * rk_ref[...], nd) * vd

    strict = _block_tri(gl, ch, strict=True)
    incl = _block_tri(gl, ch)
    eye = (lax.broadcasted_iota(jnp.int32, (gl, gl), 0) == lax.broadcasted_iota(jnp.int32, (gl, gl), 1)).astype(F32)
    n_dbl = int(math.log2(ch))

    gpi = math.gcd(n_groups, 4)

    def rw_prepare(it, carry):
        def heads(ref):
            return jnp.stack([ref[pl.ds(pl.multiple_of((it * gpi + g) * gl, _SUBLANES), gl), h * nd:(h + 1) * nd]
                              for g in range(gpi) for h in range(D_HEADS)])

        at, rt, bt, kt, vv = (heads(ref) for ref in (at_scr, rt_scr, bt_scr, kt_scr, vd_scr))
        sc = _bmm_nt(jnp.concatenate([at, rt], axis=1), jnp.concatenate([bt, kt], axis=1))
        n_ab = jnp.where(strict, sc[:, 0:gl, 0:gl], 0.0)
        a_ak = jnp.where(strict, sc[:, 0:gl, gl:2 * gl], 0.0)
        r_rb = jnp.where(incl, sc[:, gl:2 * gl, 0:gl], 0.0)
        r_rk = jnp.where(incl, sc[:, gl:2 * gl, gl:2 * gl], 0.0)
        xinv = eye + n_ab
        pw = n_ab
        for _ in range(n_dbl - 1):
            pw = _bmm(pw, pw)
            xinv = xinv + _bmm(pw, xinv)
        resid = eye - (xinv - _bmm_hi(n_ab, xinv))
        xinv = xinv + _bmm(xinv, resid)
        xa = _bmm_hi(xinv, jnp.concatenate([at, _bmm(a_ak, vv)], axis=2))
        ah = xa[:, :, 0:nd]
        u0 = xa[:, :, nd:2 * nd]
        ra = _bmm(r_rb, xa)
        rh = rt + ra[:, :, 0:nd]
        o0 = ra[:, :, nd:2 * nd] + _bmm(r_rk, vv)
        for g in range(gpi):
            part = slice(g * D_HEADS, (g + 1) * D_HEADS)
            ah_scr[it * gpi + g] = ah[part]
            u0_scr[it * gpi + g] = u0[part]
            rh_scr[it * gpi + g] = rh[part]
            o0_scr[it * gpi + g] = o0[part]
        return carry

    lax.fori_loop(0, n_groups // gpi, rw_prepare, 0)

    def rw_chunk(c, carry):
        gi = c // cpg
        s = c // cps
        r0 = pl.multiple_of(c * ch, _SUBLANES)
        rs = pl.ds(r0, ch)
        ls = pl.ds(pl.multiple_of((c % cpg) * ch, _SUBLANES), ch)
        glast = cum_scr[pl.ds(pl.multiple_of(r0 + ch - _SUBLANES, _SUBLANES), _SUBLANES), :][_SUBLANES - 1:_SUBLANES]
        states = [rw_out_ref[s, h] for h in range(D_HEADS)]
        ars = [_mm_nt(jnp.concatenate([ah_scr[gi, h, ls, :], rh_scr[gi, h, ls, :]], axis=0), states[h])
               for h in range(D_HEADS)]
        us = [ars[h][0:ch] + u0_scr[gi, h, ls, :] for h in range(D_HEADS)]
        ds = []
        for h in range(D_HEADS):
            hs = slice(h * nd, (h + 1) * nd)
            uv = jnp.concatenate([us[h], vd_scr[rs, hs]], axis=0)
            bk = jnp.concatenate([bt_scr[rs, hs], kt_scr[rs, hs]], axis=0)
            ds.append(_mm_tn(uv, bk))
        for h in range(D_HEADS):
            hs = slice(h * nd, (h + 1) * nd)
            rw_out_ref[s, h] = (states[h] + ds[h]) * jnp.exp(glast[:, hs])
            od_scr[rs, hs] = ars[h][ch:2 * ch] + o0_scr[gi, h, ls, :]
        return carry

    lax.fori_loop(0, n_groups * cpg, rw_chunk, 0)

    o_d = od_scr[...]
    mean = _head_sums(o_d, nd) * (1.0 / nd)
    xc = o_d - mean
    var = _head_sums(xc * xc, nd) * (1.0 / nd)
    on = xc * lax.rsqrt(var + RWKV_GN_EPS) * lng_ref[...] + lnb_ref[...]
    y_scr[:, cw:cw + dw] = (on + bonus) * gate

    xo_ref[...] = (x + _mm(y_scr[...], wout_ref[...])).reshape(nb, tt, d)


def _odd_layer(x, g, win, wout, tabs, rgn, rw, ret_in, rw_in, sh_in, nb, tt, ch, gl):
    b, t, d = x.shape
    rows = nb * tt
    cos, sin, qdec, kdec, cdec, dmask = tabs
    sh3 = sh_in.reshape(b, 1, D_IN)
    vecs = [rw[k].reshape(1, -1) for k in ("mu", "w0")] + [rw["w2"]] + [rw["a0"].reshape(1, -1), rw["a2"], rw["g2"]] + \
           [rw[k].reshape(1, -1) for k in ("k_k", "k_a", "r_k", "lnx_g", "lnx_b")]
    w512 = pltpu.VMEM((rows, D_WIDTH), F32)
    outs = pl.pallas_call(
        functools.partial(_odd_kernel, nb=nb, tt=tt, ch=ch, gl=gl),
        out_shape=(jax.ShapeDtypeStruct((b, t, d), F32),
                   jax.ShapeDtypeStruct(ret_in.shape, F32),
                   jax.ShapeDtypeStruct(rw_in.shape, F32),
                   jax.ShapeDtypeStruct(sh3.shape, F32)),
        grid=(b // nb, t // tt),
        in_specs=[
            pl.BlockSpec((nb, tt, d), lambda i, j: (i, j, 0)),
            _const_spec((1, d)),
            _const_spec(win.shape),
            _const_spec(wout.shape),
            pl.BlockSpec((1, rows, C_DIM), lambda i, j: (j, 0, 0)),
            pl.BlockSpec((1, rows, C_DIM), lambda i, j: (j, 0, 0)),
            _const_spec(qdec.shape),
            _const_spec(kdec.shape),
            _const_spec(cdec.shape),
            _const_spec(dmask.shape),
            _const_spec((1, C_HEADS * C_DIM)),
        ] + [_const_spec(v.shape) for v in vecs] + [
            pl.BlockSpec((nb,) + ret_in.shape[1:], lambda i, j: (i, 0, 0, 0), pipeline_mode=pl.Buffered(1)),
            pl.BlockSpec((nb,) + rw_in.shape[1:], lambda i, j: (i, 0, 0, 0), pipeline_mode=pl.Buffered(1)),
            pl.BlockSpec((nb, 1, D_IN), lambda i, j: (i, 0, 0)),
        ],
        out_specs=(
            pl.BlockSpec((nb, tt, d), lambda i, j: (i, j, 0)),
            pl.BlockSpec((nb,) + ret_in.shape[1:], lambda i, j: (i, 0, 0, 0)),
            pl.BlockSpec((nb,) + rw_in.shape[1:], lambda i, j: (i, 0, 0, 0)),
            pl.BlockSpec((nb, 1, D_IN), lambda i, j: (i, 0, 0)),
        ),
        scratch_shapes=[
            pltpu.VMEM((rows, ODD_IN), F32),
            pltpu.VMEM((rows, C_HEADS * C_DIM + D_WIDTH), F32),
            pltpu.VMEM((tt + _SUBLANES, D_IN), F32),
            pltpu.VMEM((rows, D_IN), F32),
        ] + [w512] * 10 + [pltpu.VMEM((rows // gl, D_HEADS, gl, D_HDIM), F32)] * 4,
        compiler_params=pltpu.CompilerParams(dimension_semantics=("parallel", "arbitrary"),
                                             vmem_limit_bytes=_VMEM_LIMIT_BYTES),
        name="odd_layer",
    )(x, g.reshape(1, d), win, wout, cos, sin, qdec, kdec, cdec, dmask, rgn.reshape(1, -1), *vecs,
      ret_in, rw_in, sh3)
    xo, ret_o, rw_o, sh_o = outs
    return xo, ret_o, rw_o, sh_o.reshape(b, D_IN)


def _odd_tables(pos0, t, nb, tt, ch, gl):
    half = C_DIM // 2
    inv = ROPE_BASE ** (-jnp.arange(half, dtype=F32) / half)
    pos = pos0 + jnp.arange(t, dtype=jnp.int32)
    ang = pos.astype(F32)[:, None] * inv[None, :]
    cos = jnp.cos(ang)
    sin = jnp.sin(ang)
    cosf = jnp.concatenate([cos, cos], axis=-1).reshape(t // tt, tt, C_DIM)
    sinf = jnp.concatenate([-sin, sin], axis=-1).reshape(t // tt, tt, C_DIM)
    cosf = jnp.tile(cosf, (1, nb, 1))
    sinf = jnp.tile(sinf, (1, nb, 1))
    lg = jnp.log1p(-jnp.exp2(-5.0 - jnp.arange(C_HEADS, dtype=F32)))[:, None]
    idx = jnp.arange(ch, dtype=F32)
    q_dec = jnp.exp(lg * (idx + 1.0))
    k_dec = jnp.exp(lg * (ch - 1.0 - idx))
    rel = idx[:, None] - idx[None, :]
    dm = jnp.where(rel >= 0, jnp.exp(lg[:, :, None] * jnp.maximum(rel, 0.0)), 0.0)
    c_dec = jnp.exp(lg * ch)
    cpg = gl // ch
    expand = lambda v: jnp.repeat(jnp.tile(v.T, (cpg, 1)), C_DIM, axis=1)
    qdec = expand(q_dec)
    kdec = expand(k_dec)
    cdec = jnp.repeat(c_dec.T, C_DIM, axis=1)
    gi = jnp.arange(gl)
    same = (gi[:, None] // ch) == (gi[None, :] // ch)
    dmask = jnp.where(same[None], jnp.tile(dm, (1, cpg, cpg)), 0.0)
    return cosf, sinf, qdec, kdec, cdec, dmask


def _prep_weights(ffn1_w_gu, ffn1_w_down, ffn2_w_gu, ffn2_w_down, even_w_in, even_w_out, odd_w_in, odd_w_out,
                  xattn_wq, xattn_wkv, xattn_wo):
    bf = lambda w: w.astype(_MXU_DTYPE)
    return dict(ffn1=(bf(ffn1_w_gu), bf(ffn1_w_down)), ffn2=(bf(ffn2_w_gu), bf(ffn2_w_down)),
                even_in=bf(even_w_in), even_out=bf(even_w_out), odd_in=bf(odd_w_in), odd_out=bf(odd_w_out),
                wq=bf(xattn_wq), wkv=bf(xattn_wkv), wo=bf(xattn_wo))


def _trunk(x, pos0, conv_in, hg_in, ret_in, rw_in, sh_in, mem_k, mem_v, W, P, cfg):
    b, t, d = x.shape
    depth = P["ffn1_norm"].shape[0]
    lb_all = jnp.cumsum(jax.nn.softmax(P["hgrn_lb"].astype(F32), axis=0), axis=0)
    convs, hgs, rets, rws, shs = [], [], [], [], []
    for l in range(depth):
        jl = l // 2
        wgu, wd = W["ffn1"]
        x = _ffn(x.reshape(b * t, d), P["ffn1_norm"][l], wgu, wd, l, None, cfg["tm"], cfg["tf"]).reshape(b, t, d)
        if l % 2 == 0:
            x, cb, sh = _even_layer(x, P["mix_norm"][l], W["even_in"][jl], W["even_out"][jl], P["conv_w"][jl],
                                    lb_all[jl], P["hgrn_gnorm"][jl], conv_in[jl], hg_in[jl],
                                    cfg["nb"], cfg["tt"], cfg["blk"])
            convs.append(cb)
            hgs.append(sh)
        else:
            tabs = _odd_tables(pos0, t, cfg["nb"], cfg["tt"], cfg["ch"], cfg["gl"])
            rw = {k: P["rwkv_" + k][jl] for k in ("mu", "w0", "w2", "a0", "a2", "g2", "k_k", "k_a", "r_k",
                                                  "lnx_g", "lnx_b")}
            x, sr, sw, ss = _odd_layer(x, P["mix_norm"][l], W["odd_in"][jl], W["odd_out"][jl], tabs,
                                       P["ret_gnorm"][jl], rw, ret_in[jl], rw_in[jl], sh_in[jl],
                                       cfg["nb"], cfg["tt"], cfg["ch"], cfg["gl"])
            rets.append(sr)
            rws.append(sw)
            shs.append(ss)
        x = _xattn(x, P["xattn_norm"][l], W["wq"], W["wo"], mem_k, mem_v, l, cfg["xnb"], cfg["xtt"])
        wgu, wd = W["ffn2"]
        fin = P["final_norm"] if l == depth - 1 else None
        x = _ffn(x.reshape(b * t, d), P["ffn2_norm"][l], wgu, wd, l, fin, cfg["tm"], cfg["tf"]).reshape(b, t, d)
    return x, jnp.stack(convs), jnp.stack(hgs), jnp.stack(rets), jnp.stack(rws), jnp.stack(shs)


def _configs(b, t):
    if t >= 256:
        tt = 256
        return dict(tm=512, tf=256, nb=1, tt=tt, blk=16, ch=64, gl=64, xnb=1, xtt=min(t, 512))
    nb = min(b, 128 // t)
    return dict(tm=min(b * t, 512), tf=256, nb=nb, tt=t, blk=t, ch=t, gl=nb * t, xnb=min(b, 8), xtt=t)


def kernel(x_prompt, x_sample, state_conv, state_hgrn, state_ret, state_rwkv, state_shift, cache_mem_k, cache_mem_v, mem_prompt, ffn1_norm, ffn1_w_gu, ffn1_w_down, mix_norm, even_w_in, even_w_out, conv_w, hgrn_lb, hgrn_gnorm, odd_w_in, odd_w_out, ret_gnorm, rwkv_mu, rwkv_w0, rwkv_w2, rwkv_a0, rwkv_a2, rwkv_g2, rwkv_k_k, rwkv_k_a, rwkv_r_k, rwkv_lnx_g, rwkv_lnx_b, xattn_norm, mem_norm, xattn_wq, xattn_wkv, xattn_wo, ffn2_norm, ffn2_w_gu, ffn2_w_down, final_norm):
    P = dict(ffn1_norm=ffn1_norm, mix_norm=mix_norm, conv_w=conv_w, hgrn_lb=hgrn_lb, hgrn_gnorm=hgrn_gnorm,
             ret_gnorm=ret_gnorm, rwkv_mu=rwkv_mu, rwkv_w0=rwkv_w0, rwkv_w2=rwkv_w2, rwkv_a0=rwkv_a0,
             rwkv_a2=rwkv_a2, rwkv_g2=rwkv_g2, rwkv_k_k=rwkv_k_k, rwkv_k_a=rwkv_k_a,
             rwkv_r_k=rwkv_r_k.reshape(rwkv_r_k.shape[0], -1), rwkv_lnx_g=rwkv_lnx_g, rwkv_lnx_b=rwkv_lnx_b,
             xattn_norm=xattn_norm, ffn2_norm=ffn2_norm, final_norm=final_norm)
    W = _prep_weights(ffn1_w_gu, ffn1_w_down, ffn2_w_gu, ffn2_w_down, even_w_in, even_w_out, odd_w_in,
                      odd_w_out, xattn_wq, xattn_wkv, xattn_wo)
    bp, tp, d = x_prompt.shape
    bs, ts, _ = x_sample.shape
    depth = ffn1_norm.shape[0]
    n_mem = mem_prompt.shape[1]
    hdx = d // X_HEADS

    mem2d = mem_prompt.reshape(bp * n_mem, d)
    mks, mvs = [], []
    for l in range(depth):
        mk, mv = _mem_kv(mem2d, mem_norm[l], W["wkv"], l, 256)
        mks.append(mk.reshape(bp, n_mem, d))
        mvs.append(mv.reshape(bp, n_mem, d))
    mem_k_p = jnp.stack(mks)
    mem_v_p = jnp.stack(mvs)

    z = lambda ref: jnp.zeros((ref.shape[0], bp) + ref.shape[2:], F32)
    y_p, conv_p, hg_p, ret_p, rw_p, sh_p = _trunk(
        x_prompt, 0, z(state_conv), z(state_hgrn), z(state_ret), z(state_rwkv), z(state_shift),
        mem_k_p, mem_v_p, W, P, _configs(bp, tp))
    past_len = 16384
    y_s, conv_s, hg_s, ret_s, rw_s, sh_s = _trunk(
        x_sample, past_len, state_conv, state_hgrn, state_ret, state_rwkv, state_shift,
        cache_mem_k, cache_mem_v, W, P, _configs(bs, ts))
    heads = lambda m: m.reshape(m.shape[:3] + (X_HEADS, hdx))
    return (y_p, y_s, conv_p, hg_p, ret_p, rw_p, sh_p, heads(mem_k_p), heads(mem_v_p),
            conv_s, hg_s, ret_s, rw_s, sh_s)
```

```python
import functools
import math

import jax
import jax.numpy as jnp
from jax import lax
from jax.experimental import pallas as pl
from jax.experimental.pallas import tpu as pltpu

F32 = jnp.float32
_MXU_DTYPE = jnp.bfloat16

NORM_EPS = 1e-6
RWKV_GN_EPS = 64e-5
ROPE_BASE = 10000.0

_VMEM_LIMIT_BYTES = 56 * 1024 * 1024
_SUBLANES = 8
_PACKED_ROWS = 16
_MXU_WIDTH = 256
_SEQS_PER_PASS = 4

A_WIDTH = 512
B_HEADS, B_DIM = 4, 128
C_HEADS, C_DIM = 4, 128
D_HEADS, D_HDIM = 8, 64
D_WIDTH = D_HEADS * D_HDIM
EVEN_IN = 7 * 512
C_IN = 4 * 512
D_IN = 3 * 512 + 64 + 64 + 128
ODD_IN = C_IN + D_IN
X_HEADS = 4


def _mm(a, b):
    return jnp.dot(a.astype(_MXU_DTYPE), b.astype(_MXU_DTYPE), preferred_element_type=F32)


def _mm_nt(a, b):
    return lax.dot_general(a.astype(_MXU_DTYPE), b.astype(_MXU_DTYPE), (((1,), (1,)), ((), ())),
                           preferred_element_type=F32)


def _mm_tn(a, b):
    k = a.shape[0]
    if k % _PACKED_ROWS:
        pad = _PACKED_ROWS - k % _PACKED_ROWS
        a = jnp.concatenate([a, jnp.zeros((pad, a.shape[1]), a.dtype)], axis=0)
        b = jnp.concatenate([b, jnp.zeros((pad, b.shape[1]), b.dtype)], axis=0)
    return lax.dot_general(a.astype(_MXU_DTYPE), b.astype(_MXU_DTYPE), (((0,), (0,)), ((), ())),
                           preferred_element_type=F32)


def _split2(a):
    hi = a.astype(_MXU_DTYPE)
    lo = (a - hi.astype(F32)).astype(_MXU_DTYPE)
    return hi, lo


def _mm_hi(a, b):
    ah, al = _split2(a)
    bh, bl = _split2(b)
    d = functools.partial(jnp.dot, preferred_element_type=F32)
    return d(ah, bh) + d(ah, bl) + d(al, bh)


_BATCH_NN = (((2,), (1,)), ((0,), (0,)))
_BATCH_NT = (((2,), (2,)), ((0,), (0,)))


def _bmm(a, b):
    return lax.dot_general(a.astype(_MXU_DTYPE), b.astype(_MXU_DTYPE), _BATCH_NN, preferred_element_type=F32)


def _bmm_nt(a, b):
    return lax.dot_general(a.astype(_MXU_DTYPE), b.astype(_MXU_DTYPE), _BATCH_NT, preferred_element_type=F32)


def _bmm_hi(a, b):
    ah, al = _split2(a)
    bh, bl = _split2(b)
    d = functools.partial(lax.dot_general, dimension_numbers=_BATCH_NN, preferred_element_type=F32)
    return d(ah, bh) + d(ah, bl) + d(al, bh)


def _mm_exact_lhs(m01, x):
    m = m01.astype(_MXU_DTYPE)
    x0 = x.astype(_MXU_DTYPE)
    r1 = x - x0.astype(F32)
    x1 = r1.astype(_MXU_DTYPE)
    x2 = (r1 - x1.astype(F32)).astype(_MXU_DTYPE)
    d = functools.partial(jnp.dot, preferred_element_type=F32)
    return d(m, x0) + d(m, x1) + d(m, x2)


def _mm_exact_rhs(x, m01):
    m = m01.astype(_MXU_DTYPE)
    x0 = x.astype(_MXU_DTYPE)
    r1 = x - x0.astype(F32)
    x1 = r1.astype(_MXU_DTYPE)
    x2 = (r1 - x1.astype(F32)).astype(_MXU_DTYPE)
    d = functools.partial(jnp.dot, preferred_element_type=F32)
    return d(x0, m) + d(x1, m) + d(x2, m)


def _rms(x, g):
    return x * lax.rsqrt(jnp.mean(x * x, axis=-1, keepdims=True) + NORM_EPS) * g


def _sigmoid(x):
    return 1.0 / (1.0 + jnp.exp(-x))


def _silu(x):
    return x * _sigmoid(x)


def _softplus(x):
    return jnp.maximum(x, 0.0) + jnp.log1p(jnp.exp(-jnp.abs(x)))


def _block_tri(n, blk, strict=False):
    r = lax.broadcasted_iota(jnp.int32, (n, n), 0)
    c = lax.broadcasted_iota(jnp.int32, (n, n), 1)
    same = (r // blk) == (c // blk)
    low = (c < r) if strict else (c <= r)
    return same & low


def _head_sums(x, hd):
    width = min(_MXU_WIDTH, x.shape[1])
    r = lax.broadcasted_iota(jnp.int32, (width, width), 0)
    c = lax.broadcasted_iota(jnp.int32, (width, width), 1)
    ones = ((r // hd) == (c // hd)).astype(F32)
    parts = [_mm_exact_rhs(x[:, i:i + width], ones) for i in range(0, x.shape[1], width)]
    return parts[0] if len(parts) == 1 else jnp.concatenate(parts, axis=1)


def _ffn_kernel(x_ref, g_ref, wgu_ref, wd_ref, fg_ref, o_ref, h_scr, acc_scr, *, tf, final):
    dff = wd_ref.shape[0]
    x = x_ref[...]
    h_scr[...] = _rms(x, g_ref[...]).astype(_MXU_DTYPE)
    for c in range(dff // tf):
        h = h_scr[...]
        gate = jnp.dot(h, wgu_ref[:, c * tf:(c + 1) * tf], preferred_element_type=F32)
        up = jnp.dot(h, wgu_ref[:, dff + c * tf:dff + (c + 1) * tf], preferred_element_type=F32)
        act = (_silu(gate) * up).astype(_MXU_DTYPE)
        part = jnp.dot(act, wd_ref[c * tf:(c + 1) * tf, :], preferred_element_type=F32)
        if c == 0:
            acc_scr[...] = part
        else:
            acc_scr[...] += part
    y = x + 0.5 * acc_scr[...]
    if final:
        y = _rms(y, fg_ref[...])
    o_ref[...] = y


def _const_spec(shape):
    nd = len(shape)
    return pl.BlockSpec(shape, lambda *_: (0,) * nd, pipeline_mode=pl.Buffered(1))


def _layer_spec(stacked_shape, layer):
    nd = len(stacked_shape) - 1
    return pl.BlockSpec((None,) + tuple(stacked_shape[1:]), lambda *_: (layer,) + (0,) * nd,
                        pipeline_mode=pl.Buffered(1))


def _ffn(x2d, g, wgu, wd, layer, final_g, tm, tf):
    n, d = x2d.shape
    final = final_g is not None
    fg = final_g if final else g
    return pl.pallas_call(
        functools.partial(_ffn_kernel, tf=tf, final=final),
        out_shape=jax.ShapeDtypeStruct((n, d), F32),
        grid=(n // tm,),
        in_specs=[
            pl.BlockSpec((tm, d), lambda i: (i, 0)),
            _const_spec((1, d)),
            _layer_spec(wgu.shape, layer),
            _layer_spec(wd.shape, layer),
            _const_spec((1, d)),
        ],
        out_specs=pl.BlockSpec((tm, d), lambda i: (i, 0)),
        scratch_shapes=[pltpu.VMEM((tm, d), _MXU_DTYPE), pltpu.VMEM((tm, d), F32)],
        compiler_params=pltpu.CompilerParams(dimension_semantics=("parallel",),
                                             vmem_limit_bytes=_VMEM_LIMIT_BYTES),
        name="ffn",
    )(x2d, g.reshape(1, d), wgu, wd, fg.reshape(1, d))


def _mem_kv_kernel(x_ref, g_ref, w_ref, k_ref, v_ref):
    d = x_ref.shape[-1]
    kv = _mm(_rms(x_ref[...], g_ref[...]), w_ref[...])
    k_ref[...] = kv[:, :d]
    v_ref[...] = kv[:, d:]


def _mem_kv(x2d, g, w, tm):
    n, d = x2d.shape
    depth = w.shape[0]
    out_spec = pl.BlockSpec((None, tm, d), lambda l, i: (l, i, 0))
    out_shape = jax.ShapeDtypeStruct((depth, n, d), F32)
    return pl.pallas_call(
        _mem_kv_kernel,
        out_shape=(out_shape, out_shape),
        grid=(depth, n // tm),
        in_specs=[pl.BlockSpec((tm, d), lambda l, i: (i, 0)),
                  pl.BlockSpec((None, 1, d), lambda l, i: (l, 0, 0)),
                  pl.BlockSpec((None, d, 2 * d), lambda l, i: (l, 0, 0))],
        out_specs=(out_spec, out_spec),
        compiler_params=pltpu.CompilerParams(dimension_semantics=("parallel", "parallel"),
                                             vmem_limit_bytes=_VMEM_LIMIT_BYTES),
        name="mem_kv",
    )(x2d, g.reshape(depth, 1, d), w)


def _xattn_kernel(x_ref, g_ref, wq_ref, wo_ref, mk_ref, mv_ref, o_ref, q_scr, a_scr, *cache_scr,
                  nb, tt, n_heads, layer, head_split):
    d = x_ref.shape[-1]
    hd = d // n_heads
    rows = nb * tt
    scale = hd ** -0.5

    if head_split:
        kbuf, vbuf, sem = cache_scr
        i = pl.program_id(0)
        slot = i % 2

        def slab_copies(step, to_slot):
            seqs = pl.ds(step * nb, nb)
            return [pltpu.make_async_copy(src.at[layer, seqs, :, h, :], buf.at[to_slot, h], sem.at[to_slot, kv, h])
                    for kv, (src, buf) in enumerate(((mk_ref, kbuf), (mv_ref, vbuf))) for h in range(n_heads)]

        @pl.when(i == 0)
        def _():
            for cp in slab_copies(0, 0):
                cp.start()

        @pl.when(i + 1 < pl.num_programs(0))
        def _():
            for cp in slab_copies(i + 1, 1 - slot):
                cp.start()

        keys = lambda s, h: kbuf[slot, h, s]
        vals = lambda s, h: vbuf[slot, h, s]
    else:
        keys = lambda s, h: mk_ref[s, :, h * hd:(h + 1) * hd]
        vals = lambda s, h: mv_ref[s, :, h * hd:(h + 1) * hd]

    x = x_ref[...].reshape(rows, d)
    q_scr[...] = _mm(_rms(x, g_ref[...]), wq_ref[...])

    if head_split:
        for cp in slab_copies(i, slot):
            cp.wait()

    def seq_body(s, carry):
        r0 = pl.multiple_of(s * tt, _SUBLANES)
        hs = [slice(h * hd, (h + 1) * hd) for h in range(n_heads)]
        scs = [_mm_nt(q_scr[pl.ds(r0, tt), hs[h]], keys(s, h)) * scale for h in range(n_heads)]
        es = [jnp.exp(sc - jnp.max(sc, axis=-1, keepdims=True)) for sc in scs]
        prs = [e / jnp.sum(e, axis=-1, keepdims=True) for e in es]
        for h in range(n_heads):
            a_scr[pl.ds(r0, tt), hs[h]] = _mm(prs[h], vals(s, h))
        return carry

    lax.fori_loop(0, nb, seq_body, 0)
    o_ref[...] = (x + _mm(a_scr[...], wo_ref[...])).reshape(nb, tt, d)


def _xattn(x, g, wq, wo, mk, mv, layer, nb, tt):
    b, t, d = x.shape
    n_mem = mk.shape[2]
    rows = nb * tt
    head_split = mk.ndim == 5
    if head_split:
        assert t == tt and mk.shape[3] == X_HEADS
        mem_spec = pl.BlockSpec(memory_space=pl.ANY)
        slab = pltpu.VMEM((2, X_HEADS, nb, n_mem, d // X_HEADS), F32)
        cache_scr = [slab, slab, pltpu.SemaphoreType.DMA((2, 2, X_HEADS))]
        semantics = ("arbitrary", "arbitrary")
    else:
        mem_spec = pl.BlockSpec((None, nb, n_mem, d), lambda i, j: (layer, i, 0, 0))
        cache_scr = []
        semantics = ("parallel", "parallel")
    return pl.pallas_call(
        functools.partial(_xattn_kernel, nb=nb, tt=tt, n_heads=X_HEADS, layer=layer, head_split=head_split),
        out_shape=jax.ShapeDtypeStruct((b, t, d), F32),
        grid=(b // nb, t // tt),
        in_specs=[
            pl.BlockSpec((nb, tt, d), lambda i, j: (i, j, 0)),
            _const_spec((1, d)),
            _layer_spec(wq.shape, layer),
            _layer_spec(wo.shape, layer),
            mem_spec,
            mem_spec,
        ],
        out_specs=pl.BlockSpec((nb, tt, d), lambda i, j: (i, j, 0)),
        scratch_shapes=[pltpu.VMEM((rows, d), F32), pltpu.VMEM((rows, d), F32)] + cache_scr,
        compiler_params=pltpu.CompilerParams(dimension_semantics=semantics,
                                             vmem_limit_bytes=_VMEM_LIMIT_BYTES),
        name="xattn",
    )(x, g.reshape(1, d), wq, wo, mk, mv)


def _even_kernel(x_ref, g_ref, win_ref, wout_ref, cw_ref, lb_ref, gn_ref, conv_in_ref, hg_in_ref,
                 xo_ref, conv_out_ref, hg_out_ref,
                 p_scr, y_scr, ext_scr, b_scr, qg_scr, kk_scr, o_scr, st_scr, *, nb, tt, blk):
    j = pl.program_id(1)
    nj = pl.num_programs(1)
    d = x_ref.shape[-1]
    rows = nb * tt
    aw = A_WIDTH
    hd = B_DIM

    @pl.when(j == 0)
    def _():
        conv_out_ref[...] = conv_in_ref[...]

        def init(s, carry):
            for h in range(B_HEADS):
                st_scr[s, h] = hg_in_ref[s, h].T
            return carry

        lax.fori_loop(0, nb, init, 0)

    x = x_ref[...].reshape(rows, d)
    p_scr[...] = _mm(_rms(x, g_ref[...]), win_ref[...])

    cw = cw_ref[...]

    def conv_body(s, carry):
        r0 = pl.multiple_of(s * tt, _SUBLANES)
        u = p_scr[pl.ds(r0, tt), 2 * aw:3 * aw] * p_scr[pl.ds(r0, tt), 0:aw]
        ext_scr[_SUBLANES:_SUBLANES + tt, :] = u
        ext_scr[_SUBLANES - 2:_SUBLANES, :] = conv_out_ref[s]
        conv = (cw[0:1] * ext_scr[_SUBLANES - 2:_SUBLANES - 2 + tt, :]
                + cw[1:2] * ext_scr[_SUBLANES - 1:_SUBLANES - 1 + tt, :]
                + cw[2:3] * ext_scr[_SUBLANES:_SUBLANES + tt, :])
        y_scr[pl.ds(r0, tt), 0:aw] = p_scr[pl.ds(r0, tt), aw:2 * aw] * conv
        conv_out_ref[s] = ext_scr[_SUBLANES + tt - 2:_SUBLANES + tt, :]
        return carry

    lax.fori_loop(0, nb, conv_body, 0)

    lb = lb_ref[...]
    f = lb + (1.0 - lb) * _sigmoid(p_scr[:, 4 * aw:5 * aw])
    kk_scr[...] = 1.0 - f
    b_scr[...] = _mm_exact_lhs(_block_tri(rows, blk).astype(F32), jnp.log(f))
    qg_scr[...] = _silu(p_scr[:, 3 * aw:4 * aw])
    rowi = lax.broadcasted_iota(jnp.int32, (1, blk, hd), 1)
    sp = math.gcd(nb, _SEQS_PER_PASS)
    chains = sp * B_HEADS

    def seq_group(sg, carry):
        s0 = sg * sp

        def time_block(k, c2):
            def stack(ref, col0):
                return jnp.stack([ref[pl.ds(pl.multiple_of((s0 + si) * tt + k * blk, _SUBLANES), blk),
                                      col0 + h * hd:col0 + (h + 1) * hd]
                                  for si in range(sp) for h in range(B_HEADS)])

            bb, qg, kkb = stack(b_scr, 0), stack(qg_scr, 0), stack(kk_scr, 0)
            vb = stack(p_scr, 5 * aw)
            st = st_scr[pl.ds(s0, sp)].reshape(chains, hd, hd)
            blast = bb[:, blk - 1:blk, :]
            o = _bmm_nt(qg * jnp.exp(bb), st)
            for jj in range(blk):
                dec = jnp.exp(jnp.where(rowi >= jj, bb - bb[:, jj:jj + 1, :], -jnp.inf))
                att = jnp.sum(qg * kkb[:, jj:jj + 1, :] * dec, axis=-1, keepdims=True)
                o = o + att * vb[:, jj:jj + 1, :]
            kd = kkb * jnp.exp(blast - bb)
            upd = jnp.stack([_mm_tn(vb[c], kd[c]) for c in range(chains)])
            st_scr[pl.ds(s0, sp)] = (st * jnp.exp(blast) + upd).reshape(sp, B_HEADS, hd, hd)
            for si in range(sp):
                rs = pl.ds(pl.multiple_of((s0 + si) * tt + k * blk, _SUBLANES), blk)
                for h in range(B_HEADS):
                    o_scr[rs, h * hd:(h + 1) * hd] = o[si * B_HEADS + h]
            return c2

        lax.fori_loop(0, tt // blk, time_block, 0)
        return carry

    lax.fori_loop(0, nb // sp, seq_group, 0)

    for h in range(B_HEADS):
        hs = slice(h * hd, (h + 1) * hd)
        o = o_scr[:, hs]
        on = o * lax.rsqrt(jnp.mean(o * o, axis=-1, keepdims=True) + NORM_EPS) * gn_ref[:, hs]
        y_scr[:, aw + h * hd:aw + (h + 1) * hd] = on * _silu(p_scr[:, 6 * aw + h * hd:6 * aw + (h + 1) * hd])

    xo_ref[...] = (x + _mm(y_scr[...], wout_ref[...])).reshape(nb, tt, d)

    @pl.when(j == nj - 1)
    def _():
        def fin(s, carry):
            for h in range(B_HEADS):
                hg_out_ref[s, h] = st_scr[s, h].T
            return carry

        lax.fori_loop(0, nb, fin, 0)


def _even_layer(x, g, win, wout, cw, lb, gn, conv_in, hg_in, nb, tt, blk):
    b, t, d = x.shape
    rows = nb * tt
    return pl.pallas_call(
        functools.partial(_even_kernel, nb=nb, tt=tt, blk=blk),
        out_shape=(jax.ShapeDtypeStruct((b, t, d), F32),
                   jax.ShapeDtypeStruct(conv_in.shape, F32),
                   jax.ShapeDtypeStruct(hg_in.shape, F32)),
        grid=(b // nb, t // tt),
        in_specs=[
            pl.BlockSpec((nb, tt, d), lambda i, j: (i, j, 0)),
            _const_spec((1, d)),
            _const_spec(win.shape),
            _const_spec(wout.shape),
            _const_spec(cw.shape),
            _const_spec((1, A_WIDTH)),
            _const_spec((1, B_HEADS * B_DIM)),
            pl.BlockSpec((nb,) + conv_in.shape[1:], lambda i, j: (i, 0, 0)),
            pl.BlockSpec((nb,) + hg_in.shape[1:], lambda i, j: (i, 0, 0, 0), pipeline_mode=pl.Buffered(1)),
        ],
        out_specs=(
            pl.BlockSpec((nb, tt, d), lambda i, j: (i, j, 0)),
            pl.BlockSpec((nb,) + conv_in.shape[1:], lambda i, j: (i, 0, 0)),
            pl.BlockSpec((nb,) + hg_in.shape[1:], lambda i, j: (i, 0, 0, 0)),
        ),
        scratch_shapes=[
            pltpu.VMEM((rows, EVEN_IN), F32),
            pltpu.VMEM((rows, 2 * A_WIDTH), F32),
            pltpu.VMEM((tt + _SUBLANES, A_WIDTH), F32),
            pltpu.VMEM((rows, A_WIDTH), F32),
            pltpu.VMEM((rows, A_WIDTH), F32),
            pltpu.VMEM((rows, A_WIDTH), F32),
            pltpu.VMEM((rows, A_WIDTH), F32),
            pltpu.VMEM((nb, B_HEADS, B_DIM, B_DIM), F32),
        ],
        compiler_params=pltpu.CompilerParams(dimension_semantics=("parallel", "arbitrary"),
                                             vmem_limit_bytes=_VMEM_LIMIT_BYTES),
        name="even_layer",
    )(x, g.reshape(1, d), win, wout, cw, lb.reshape(1, -1), gn.reshape(1, -1), conv_in, hg_in)


def _odd_kernel(x_ref, g_ref, win_ref, wout_ref, cos_ref, sin_ref, qdec_ref, kdec_ref, cdec_ref, dmask_ref,
                rgn_ref, mu_ref, w0_ref, w2_ref, a0_ref, a2_ref, g2_ref, kk_ref, ka_ref, rk_ref, lng_ref,
                lnb_ref, ret_in_ref, rw_in_ref, sh_in_ref,
                xo_ref, ret_out_ref, rw_out_ref, sh_out_ref,
                p_scr, y_scr, ext_scr, prev_scr, qr_scr, kr_scr, ir_scr,
                rt_scr, at_scr, bt_scr, kt_scr, vd_scr, cum_scr, od_scr, ah_scr, rh_scr, u0_scr, o0_scr,
                *, nb, tt, ch, gl):
    j = pl.program_id(1)
    d = x_ref.shape[-1]
    rows = nb * tt
    cw = C_HEADS * C_DIM
    hd = C_DIM
    nd = D_HDIM
    n_groups = rows // gl
    cpg = gl // ch
    cps = tt // ch

    @pl.when(j == 0)
    def _():
        ret_out_ref[...] = ret_in_ref[...]
        rw_out_ref[...] = rw_in_ref[...]
        sh_out_ref[...] = sh_in_ref[...]

    x = x_ref[...].reshape(rows, d)
    p_scr[...] = _mm(_rms(x, g_ref[...]), win_ref[...])

    cos = cos_ref[0]
    sin = sin_ref[0]
    for h in range(C_HEADS):
        hs = slice(h * hd, (h + 1) * hd)
        qh = p_scr[:, h * hd:(h + 1) * hd]
        kh = p_scr[:, cw + h * hd:cw + (h + 1) * hd]
        qr_scr[:, hs] = qh * cos + pltpu.roll(qh, hd // 2, axis=1) * sin
        kr_scr[:, hs] = (kh * cos + pltpu.roll(kh, hd // 2, axis=1) * sin) * (C_DIM ** -0.5)

    gpi = math.gcd(n_groups, _SEQS_PER_PASS)
    sp = math.gcd(nb, _SEQS_PER_PASS)

    def chain_rows(s0, k):
        return [pl.ds(pl.multiple_of((s0 + si) * tt + k * ch, _SUBLANES), ch) for si in range(sp)]

    def ret_intra(it, carry):
        rows_g = [pl.ds(pl.multiple_of((it * gpi + g) * gl, _SUBLANES), gl) for g in range(gpi)]
        stack = lambda ref, col0: jnp.stack([ref[rg, col0 + h * hd:col0 + (h + 1) * hd]
                                             for rg in rows_g for h in range(C_HEADS)])
        dm = dmask_ref[...]
        dm = dm if gpi == 1 else jnp.concatenate([dm] * gpi, axis=0)
        o = _bmm(_bmm_nt(stack(qr_scr, 0), stack(kr_scr, 0)) * dm, stack(p_scr, 2 * cw))
        for g, rg in enumerate(rows_g):
            for h in range(C_HEADS):
                ir_scr[rg, h * hd:(h + 1) * hd] = o[g * C_HEADS + h]
        return carry

    lax.fori_loop(0, n_groups // gpi, ret_intra, 0)

    per_chain = lambda ref, r: jnp.stack([ref[r, h * hd:(h + 1) * hd] for _ in range(sp) for h in range(C_HEADS)])
    qdec_c = per_chain(qdec_ref, slice(0, ch))
    kdec_c = per_chain(kdec_ref, slice(0, ch))
    cdec_c = per_chain(cdec_ref, slice(0, 1))

    def ret_seq_group(sg, carry):
        s0 = sg * sp

        def ret_chunk(k, c2):
            rows_c = chain_rows(s0, k)
            stack = lambda ref, col0: jnp.stack([ref[rc, col0 + h * hd:col0 + (h + 1) * hd]
                                                 for rc in rows_c for h in range(C_HEADS)])
            qc, kc, vc = stack(qr_scr, 0), stack(kr_scr, 0), stack(p_scr, 2 * cw)
            st = ret_out_ref[pl.ds(s0, sp)].reshape(sp * C_HEADS, hd, hd)
            inter = _bmm(qc, st) * qdec_c
            kcd = kc * kdec_c
            upd = jnp.stack([_mm_tn(kcd[c], vc[c]) for c in range(sp * C_HEADS)])
            ret_out_ref[pl.ds(s0, sp)] = (cdec_c * st + upd).reshape(sp, C_HEADS, hd, hd)
            for si, rc in enumerate(rows_c):
                for h in range(C_HEADS):
                    ir_scr[rc, h * hd:(h + 1) * hd] += inter[si * C_HEADS + h]
            return c2

        lax.fori_loop(0, cps, ret_chunk, 0)
        return carry

    lax.fori_loop(0, nb // sp, ret_seq_group, 0)

    for h in range(C_HEADS):
        hs = slice(h * hd, (h + 1) * hd)
        o = ir_scr[:, hs]
        on = o * lax.rsqrt(jnp.mean(o * o, axis=-1, keepdims=True) + NORM_EPS) * rgn_ref[:, hs]
        y_scr[:, hs] = on * _silu(p_scr[:, 3 * cw + h * hd:3 * cw + (h + 1) * hd])

    def shift_body(s, carry):
        r0 = pl.multiple_of(s * tt, _SUBLANES)
        ext_scr[_SUBLANES:_SUBLANES + tt, :] = p_scr[pl.ds(r0, tt), C_IN:ODD_IN]
        ext_scr[_SUBLANES - 1:_SUBLANES, :] = sh_out_ref[s]
        prev_scr[pl.ds(r0, tt), :] = ext_scr[_SUBLANES - 1:_SUBLANES - 1 + tt, :]
        sh_out_ref[s] = ext_scr[_SUBLANES + tt - 1:_SUBLANES + tt, :]
        return carry

    lax.fori_loop(0, nb, shift_body, 0)

    pd = p_scr[:, C_IN:ODD_IN]
    pm = pd + mu_ref[...] * (prev_scr[...] - pd)
    dw = D_WIDTH
    r = pm[:, 0:dw]
    kd = pm[:, dw:2 * dw]
    vd = pm[:, 2 * dw:3 * dw]
    w_dn = pm[:, 3 * dw:3 * dw + 64]
    a_dn = pm[:, 3 * dw + 64:3 * dw + 128]
    g_dn = pm[:, 3 * dw + 128:3 * dw + 256]
    w_log = -_softplus(-(w0_ref[...] + _mm(jnp.tanh(w_dn), w2_ref[...]))) - 0.5
    logdec = -jnp.exp(w_log)
    a = _sigmoid(a0_ref[...] + _mm(a_dn, a2_ref[...]))
    gate = _mm(_sigmoid(g_dn), g2_ref[...])
    kk0 = kd * kk_ref[...]
    kk = kk0 / jnp.maximum(jnp.sqrt(_head_sums(kk0 * kk0, nd)), 1e-12)
    kd2 = kd * (1.0 + (a - 1.0) * ka_ref[...])
    cum = _mm_exact_lhs(_block_tri(rows, ch).astype(F32), logdec)
    cum_scr[...] = cum
    rt_scr[...] = r * jnp.exp(cum)
    at_scr[...] = -kk * jnp.exp(cum - logdec)
    e_neg = jnp.exp(-cum)
    bt_scr[...] = kk * a * e_neg
    kt_scr[...] = kd2 * e_neg
    vd_scr[...] = vd
    bonus = _head_sums(r * kd2 * rk_ref[...], nd) * vd

    strict = _block_tri(gl, ch, strict=True)
    incl = _block_tri(gl, ch)
    eye = (lax.broadcasted_iota(jnp.int32, (gl, gl), 0) == lax.broadcasted_iota(jnp.int32, (gl, gl), 1)).astype(F32)
    n_dbl = int(math.log2(ch))

    def rw_prepare(it, carry):
        def heads(ref):
            return jnp.stack([ref[pl.ds(pl.multiple_of((it * gpi + g) * gl, _SUBLANES), gl), h * nd:(h + 1) * nd]
                              for g in range(gpi) for h in range(D_HEADS)])

        at, rt, bt, kt, vv = (heads(ref) for ref in (at_scr, rt_scr, bt_scr, kt_scr, vd_scr))
        sc = _bmm_nt(jnp.concatenate([at, rt], axis=1), jnp.concatenate([bt, kt], axis=1))
        n_ab = jnp.where(strict, sc[:, 0:gl, 0:gl], 0.0)
        a_ak = jnp.where(strict, sc[:, 0:gl, gl:2 * gl], 0.0)
        r_rb = jnp.where(incl, sc[:, gl:2 * gl, 0:gl], 0.0)
        r_rk = jnp.where(incl, sc[:, gl:2 * gl, gl:2 * gl], 0.0)
        xinv = eye + n_ab
        pw = n_ab
        for _ in range(n_dbl - 1):
            pw = _bmm(pw, pw)
            xinv = xinv + _bmm(pw, xinv)
        resid = eye - (xinv - _bmm_hi(n_ab, xinv))
        xinv = xinv + _bmm(xinv, resid)
        xa = _bmm_hi(xinv, jnp.concatenate([at, _bmm(a_ak, vv)], axis=2))
        ah = xa[:, :, 0:nd]
        u0 = xa[:, :, nd:2 * nd]
        ra = _bmm(r_rb, xa)
        rh = rt + ra[:, :, 0:nd]
        o0 = ra[:, :, nd:2 * nd] + _bmm(r_rk, vv)
        for g in range(gpi):
            part = slice(g * D_HEADS, (g + 1) * D_HEADS)
            ah_scr[it * gpi + g] = ah[part]
            u0_scr[it * gpi + g] = u0[part]
            rh_scr[it * gpi + g] = rh[part]
            o0_scr[it * gpi + g] = o0[part]
        return carry

    lax.fori_loop(0, n_groups // gpi, rw_prepare, 0)

    n_chains = sp * D_HEADS

    def rw_seq_group(sg, carry):
        s0 = sg * sp

        def rw_chunk(k, c2):
            rows_c = chain_rows(s0, k)
            where = []
            for si in range(sp):
                cidx = (s0 + si) * cps + k
                where.append((cidx // cpg, pl.ds(pl.multiple_of((cidx % cpg) * ch, _SUBLANES), ch)))
            prep = lambda ref: jnp.stack([ref[gi, h, ls, :] for gi, ls in where for h in range(D_HEADS)])
            rowsl = lambda ref: jnp.stack([ref[rc, h * nd:(h + 1) * nd] for rc in rows_c for h in range(D_HEADS)])
            st = rw_out_ref[pl.ds(s0, sp)].reshape(n_chains, nd, nd)
            ars = _bmm_nt(jnp.concatenate([prep(ah_scr), prep(rh_scr)], axis=1), st)
            uv = jnp.concatenate([ars[:, 0:ch, :] + prep(u0_scr), rowsl(vd_scr)], axis=1)
            bk = jnp.concatenate([rowsl(bt_scr), rowsl(kt_scr)], axis=1)
            upd = jnp.stack([_mm_tn(uv[c], bk[c]) for c in range(n_chains)])
            glast = jnp.stack([cum_scr[pl.ds(pl.multiple_of((s0 + si) * tt + (k + 1) * ch - _SUBLANES, _SUBLANES),
                                             _SUBLANES), h * nd:(h + 1) * nd][_SUBLANES - 1:_SUBLANES]
                               for si in range(sp) for h in range(D_HEADS)])
            rw_out_ref[pl.ds(s0, sp)] = ((st + upd) * jnp.exp(glast)).reshape(sp, D_HEADS, nd, nd)
            o = ars[:, ch:2 * ch, :] + prep(o0_scr)
            for si, rc in enumerate(rows_c):
                for h in range(D_HEADS):
                    od_scr[rc, h * nd:(h + 1) * nd] = o[si * D_HEADS + h]
            return c2

        lax.fori_loop(0, cps, rw_chunk, 0)
        return carry

    lax.fori_loop(0, nb // sp, rw_seq_group, 0)

    o_d = od_scr[...]
    mean = _head_sums(o_d, nd) * (1.0 / nd)
    xc = o_d - mean
    var = _head_sums(xc * xc, nd) * (1.0 / nd)
    on = xc * lax.rsqrt(var + RWKV_GN_EPS) * lng_ref[...] + lnb_ref[...]
    y_scr[:, cw:cw + dw] = (on + bonus) * gate

    xo_ref[...] = (x + _mm(y_scr[...], wout_ref[...])).reshape(nb, tt, d)


def _odd_layer(x, g, win, wout, tabs, rgn, rw, ret_in, rw_in, sh_in, nb, tt, ch, gl):
    b, t, d = x.shape
    rows = nb * tt
    cos, sin, qdec, kdec, cdec, dmask = tabs
    sh3 = sh_in.reshape(b, 1, D_IN)
    vecs = [rw[k].reshape(1, -1) for k in ("mu", "w0")] + [rw["w2"]] + [rw["a0"].reshape(1, -1), rw["a2"], rw["g2"]] + \
           [rw[k].reshape(1, -1) for k in ("k_k", "k_a", "r_k", "lnx_g", "lnx_b")]
    w512 = pltpu.VMEM((rows, D_WIDTH), F32)
    outs = pl.pallas_call(
        functools.partial(_odd_kernel, nb=nb, tt=tt, ch=ch, gl=gl),
        out_shape=(jax.ShapeDtypeStruct((b, t, d), F32),
                   jax.ShapeDtypeStruct(ret_in.shape, F32),
                   jax.ShapeDtypeStruct(rw_in.shape, F32),
                   jax.ShapeDtypeStruct(sh3.shape, F32)),
        grid=(b // nb, t // tt),
        in_specs=[
            pl.BlockSpec((nb, tt, d), lambda i, j: (i, j, 0)),
            _const_spec((1, d)),
            _const_spec(win.shape),
            _const_spec(wout.shape),
            pl.BlockSpec((1, rows, C_DIM), lambda i, j: (j, 0, 0)),
            pl.BlockSpec((1, rows, C_DIM), lambda i, j: (j, 0, 0)),
            _const_spec(qdec.shape),
            _const_spec(kdec.shape),
            _const_spec(cdec.shape),
            _const_spec(dmask.shape),
            _const_spec((1, C_HEADS * C_DIM)),
        ] + [_const_spec(v.shape) for v in vecs] + [
            pl.BlockSpec((nb,) + ret_in.shape[1:], lambda i, j: (i, 0, 0, 0), pipeline_mode=pl.Buffered(1)),
            pl.BlockSpec((nb,) + rw_in.shape[1:], lambda i, j: (i, 0, 0, 0), pipeline_mode=pl.Buffered(1)),
            pl.BlockSpec((nb, 1, D_IN), lambda i, j: (i, 0, 0)),
        ],
        out_specs=(
            pl.BlockSpec((nb, tt, d), lambda i, j: (i, j, 0)),
            pl.BlockSpec((nb,) + ret_in.shape[1:], lambda i, j: (i, 0, 0, 0)),
            pl.BlockSpec((nb,) + rw_in.shape[1:], lambda i, j: (i, 0, 0, 0)),
            pl.BlockSpec((nb, 1, D_IN), lambda i, j: (i, 0, 0)),
        ),
        scratch_shapes=[
            pltpu.VMEM((rows, ODD_IN), F32),
            pltpu.VMEM((rows, C_HEADS * C_DIM + D_WIDTH), F32),
            pltpu.VMEM((tt + _SUBLANES, D_IN), F32),
            pltpu.VMEM((rows, D_IN), F32),
        ] + [w512] * 10 + [pltpu.VMEM((rows // gl, D_HEADS, gl, D_HDIM), F32)] * 4,
        compiler_params=pltpu.CompilerParams(dimension_semantics=("parallel", "arbitrary"),
                                             vmem_limit_bytes=_VMEM_LIMIT_BYTES),
        name="odd_layer",
    )(x, g.reshape(1, d), win, wout, cos, sin, qdec, kdec, cdec, dmask, rgn.reshape(1, -1), *vecs,
      ret_in, rw_in, sh3)
    xo, ret_o, rw_o, sh_o = outs
    return xo, ret_o, rw_o, sh_o.reshape(b, D_IN)


def _odd_tables(pos0, t, nb, tt, ch, gl):
    half = C_DIM // 2
    inv = ROPE_BASE ** (-jnp.arange(half, dtype=F32) / half)
    pos = pos0 + jnp.arange(t, dtype=jnp.int32)
    ang = pos.astype(F32)[:, None] * inv[None, :]
    cos = jnp.cos(ang)
    sin = jnp.sin(ang)
    cosf = jnp.concatenate([cos, cos], axis=-1).reshape(t // tt, tt, C_DIM)
    sinf = jnp.concatenate([-sin, sin], axis=-1).reshape(t // tt, tt, C_DIM)
    cosf = jnp.tile(cosf, (1, nb, 1))
    sinf = jnp.tile(sinf, (1, nb, 1))
    lg = jnp.log1p(-jnp.exp2(-5.0 - jnp.arange(C_HEADS, dtype=F32)))[:, None]
    idx = jnp.arange(ch, dtype=F32)
    q_dec = jnp.exp(lg * (idx + 1.0))
    k_dec = jnp.exp(lg * (ch - 1.0 - idx))
    rel = idx[:, None] - idx[None, :]
    dm = jnp.where(rel >= 0, jnp.exp(lg[:, :, None] * jnp.maximum(rel, 0.0)), 0.0)
    c_dec = jnp.exp(lg * ch)
    cpg = gl // ch
    expand = lambda v: jnp.repeat(jnp.tile(v.T, (cpg, 1)), C_DIM, axis=1)
    qdec = expand(q_dec)
    kdec = expand(k_dec)
    cdec = jnp.repeat(c_dec.T, C_DIM, axis=1)
    gi = jnp.arange(gl)
    same = (gi[:, None] // ch) == (gi[None, :] // ch)
    dmask = jnp.where(same[None], jnp.tile(dm, (1, cpg, cpg)), 0.0)
    return cosf, sinf, qdec, kdec, cdec, dmask


def _prep_weights(ffn1_w_gu, ffn1_w_down, ffn2_w_gu, ffn2_w_down, even_w_in, even_w_out, odd_w_in, odd_w_out,
                  xattn_wq, xattn_wkv, xattn_wo):
    bf = lambda w: w.astype(_MXU_DTYPE)
    return dict(ffn1=(bf(ffn1_w_gu), bf(ffn1_w_down)), ffn2=(bf(ffn2_w_gu), bf(ffn2_w_down)),
                even_in=bf(even_w_in), even_out=bf(even_w_out), odd_in=bf(odd_w_in), odd_out=bf(odd_w_out),
                wq=bf(xattn_wq), wkv=bf(xattn_wkv), wo=bf(xattn_wo))


def _trunk(x, pos0, conv_in, hg_in, ret_in, rw_in, sh_in, mem_k, mem_v, W, P, cfg):
    b, t, d = x.shape
    depth = P["ffn1_norm"].shape[0]
    lb_all = jnp.cumsum(jax.nn.softmax(P["hgrn_lb"].astype(F32), axis=0), axis=0)
    convs, hgs, rets, rws, shs = [], [], [], [], []
    for l in range(depth):
        jl = l // 2
        wgu, wd = W["ffn1"]
        x = _ffn(x.reshape(b * t, d), P["ffn1_norm"][l], wgu, wd, l, None, cfg["tm"], cfg["tf"]).reshape(b, t, d)
        if l % 2 == 0:
            x, cb, sh = _even_layer(x, P["mix_norm"][l], W["even_in"][jl], W["even_out"][jl], P["conv_w"][jl],
                                    lb_all[jl], P["hgrn_gnorm"][jl], conv_in[jl], hg_in[jl],
                                    cfg["nb"], cfg["tt"], cfg["blk"])
            convs.append(cb)
            hgs.append(sh)
        else:
            tabs = _odd_tables(pos0, t, cfg["nb"], cfg["tt"], cfg["ch"], cfg["gl"])
            rw = {k: P["rwkv_" + k][jl] for k in ("mu", "w0", "w2", "a0", "a2", "g2", "k_k", "k_a", "r_k",
                                                  "lnx_g", "lnx_b")}
            x, sr, sw, ss = _odd_layer(x, P["mix_norm"][l], W["odd_in"][jl], W["odd_out"][jl], tabs,
                                       P["ret_gnorm"][jl], rw, ret_in[jl], rw_in[jl], sh_in[jl],
                                       cfg["nb"], cfg["tt"], cfg["ch"], cfg["gl"])
            rets.append(sr)
            rws.append(sw)
            shs.append(ss)
        x = _xattn(x, P["xattn_norm"][l], W["wq"], W["wo"], mem_k, mem_v, l, cfg["xnb"], cfg["xtt"])
        wgu, wd = W["ffn2"]
        fin = P["final_norm"] if l == depth - 1 else None
        x = _ffn(x.reshape(b * t, d), P["ffn2_norm"][l], wgu, wd, l, fin, cfg["tm"], cfg["tf"]).reshape(b, t, d)
    return x, jnp.stack(convs), jnp.stack(hgs), jnp.stack(rets), jnp.stack(rws), jnp.stack(shs)


def _configs(b, t):
    if t >= 256:
        nb = math.gcd(b, _SEQS_PER_PASS)
        return dict(tm=512, tf=256, nb=nb, tt=256 // nb, blk=16, ch=64, gl=64, xnb=1, xtt=min(t, 512))
    nb = min(b, 128 // t)
    return dict(tm=min(b * t, 512), tf=256, nb=nb, tt=t, blk=t, ch=t, gl=nb * t, xnb=min(b, 8), xtt=t)


def kernel(x_prompt, x_sample, state_conv, state_hgrn, state_ret, state_rwkv, state_shift, cache_mem_k, cache_mem_v, mem_prompt, ffn1_norm, ffn1_w_gu, ffn1_w_down, mix_norm, even_w_in, even_w_out, conv_w, hgrn_lb, hgrn_gnorm, odd_w_in, odd_w_out, ret_gnorm, rwkv_mu, rwkv_w0, rwkv_w2, rwkv_a0, rwkv_a2, rwkv_g2, rwkv_k_k, rwkv_k_a, rwkv_r_k, rwkv_lnx_g, rwkv_lnx_b, xattn_norm, mem_norm, xattn_wq, xattn_wkv, xattn_wo, ffn2_norm, ffn2_w_gu, ffn2_w_down, final_norm):
    P = dict(ffn1_norm=ffn1_norm, mix_norm=mix_norm, conv_w=conv_w, hgrn_lb=hgrn_lb, hgrn_gnorm=hgrn_gnorm,
             ret_gnorm=ret_gnorm, rwkv_mu=rwkv_mu, rwkv_w0=rwkv_w0, rwkv_w2=rwkv_w2, rwkv_a0=rwkv_a0,
             rwkv_a2=rwkv_a2, rwkv_g2=rwkv_g2, rwkv_k_k=rwkv_k_k, rwkv_k_a=rwkv_k_a,
             rwkv_r_k=rwkv_r_k.reshape(rwkv_r_k.shape[0], -1), rwkv_lnx_g=rwkv_lnx_g, rwkv_lnx_b=rwkv_lnx_b,
             xattn_norm=xattn_norm, ffn2_norm=ffn2_norm, final_norm=final_norm)
    W = _prep_weights(ffn1_w_gu, ffn1_w_down, ffn2_w_gu, ffn2_w_down, even_w_in, even_w_out, odd_w_in,
                      odd_w_out, xattn_wq, xattn_wkv, xattn_wo)
    bp, tp, d = x_prompt.shape
    bs, ts, _ = x_sample.shape
    depth = ffn1_norm.shape[0]
    n_mem = mem_prompt.shape[1]
    hdx = d // X_HEADS

    mem_k_p, mem_v_p = (m.reshape(depth, bp, n_mem, d)
                        for m in _mem_kv(mem_prompt.reshape(bp * n_mem, d), mem_norm, W["wkv"], 256))

    z = lambda ref: jnp.zeros((ref.shape[0], bp) + ref.shape[2:], F32)
    y_p, conv_p, hg_p, ret_p, rw_p, sh_p = _trunk(
        x_prompt, 0, z(state_conv), z(state_hgrn), z(state_ret), z(state_rwkv), z(state_shift),
        mem_k_p, mem_v_p, W, P, _configs(bp, tp))
    past_len = 16384
    y_s, conv_s, hg_s, ret_s, rw_s, sh_s = _trunk(
        x_sample, past_len, state_conv, state_hgrn, state_ret, state_rwkv, state_shift,
        cache_mem_k, cache_mem_v, W, P, _configs(bs, ts))
    heads = lambda m: m.reshape(m.shape[:3] + (X_HEADS, hdx))
    return (y_p, y_s, conv_p, hg_p, ret_p, rw_p, sh_p, heads(mem_k_p), heads(mem_v_p),
            conv_s, hg_s, ret_s, rw_s, sh_s)
```

```python
import functools
import math

import jax
import jax.numpy as jnp
from jax import lax
from jax.experimental import pallas as pl
from jax.experimental.pallas import tpu as pltpu

F32 = jnp.float32
_MXU_DTYPE = jnp.bfloat16

NORM_EPS = 1e-6
RWKV_GN_EPS = 64e-5
ROPE_BASE = 10000.0

_VMEM_LIMIT_BYTES = 56 * 1024 * 1024
_SUBLANES = 8
_PACKED_ROWS = 16
_MXU_WIDTH = 256
_SEQS_PER_PASS = 4

A_WIDTH = 512
B_HEADS, B_DIM = 4, 128
C_HEADS, C_DIM = 4, 128
D_HEADS, D_HDIM = 8, 64
D_WIDTH = D_HEADS * D_HDIM
EVEN_IN = 7 * 512
C_IN = 4 * 512
D_IN = 3 * 512 + 64 + 64 + 128
ODD_IN = C_IN + D_IN
X_HEADS = 4


def _mm(a, b):
    return jnp.dot(a.astype(_MXU_DTYPE), b.astype(_MXU_DTYPE), preferred_element_type=F32)


def _mm_nt(a, b):
    return lax.dot_general(a.astype(_MXU_DTYPE), b.astype(_MXU_DTYPE), (((1,), (1,)), ((), ())),
                           preferred_element_type=F32)


def _mm_tn(a, b):
    k = a.shape[0]
    if k % _PACKED_ROWS:
        pad = _PACKED_ROWS - k % _PACKED_ROWS
        a = jnp.concatenate([a, jnp.zeros((pad, a.shape[1]), a.dtype)], axis=0)
        b = jnp.concatenate([b, jnp.zeros((pad, b.shape[1]), b.dtype)], axis=0)
    return lax.dot_general(a.astype(_MXU_DTYPE), b.astype(_MXU_DTYPE), (((0,), (0,)), ((), ())),
                           preferred_element_type=F32)


def _project(h, w_ref, out_ref, col0, col1, step=_MXU_WIDTH):
    for c in range(col0, col1, step):
        e = min(c + step, col1)
        out_ref[:, c:e] = jnp.dot(h, w_ref[:, c:e], preferred_element_type=F32)


def _split2(a):
    hi = a.astype(_MXU_DTYPE)
    lo = (a - hi.astype(F32)).astype(_MXU_DTYPE)
    return hi, lo


def _mm_hi(a, b):
    ah, al = _split2(a)
    bh, bl = _split2(b)
    d = functools.partial(jnp.dot, preferred_element_type=F32)
    return d(ah, bh) + d(ah, bl) + d(al, bh)


_BATCH_NN = (((2,), (1,)), ((0,), (0,)))
_BATCH_NT = (((2,), (2,)), ((0,), (0,)))


def _bmm(a, b):
    return lax.dot_general(a.astype(_MXU_DTYPE), b.astype(_MXU_DTYPE), _BATCH_NN, preferred_element_type=F32)


def _bmm_nt(a, b):
    return lax.dot_general(a.astype(_MXU_DTYPE), b.astype(_MXU_DTYPE), _BATCH_NT, preferred_element_type=F32)


def _bmm_hi(a, b):
    ah, al = _split2(a)
    bh, bl = _split2(b)
    d = functools.partial(lax.dot_general, dimension_numbers=_BATCH_NN, preferred_element_type=F32)
    return d(ah, bh) + d(ah, bl) + d(al, bh)


def _mm_exact_lhs(m01, x):
    m = m01.astype(_MXU_DTYPE)
    x0 = x.astype(_MXU_DTYPE)
    r1 = x - x0.astype(F32)
    x1 = r1.astype(_MXU_DTYPE)
    x2 = (r1 - x1.astype(F32)).astype(_MXU_DTYPE)
    d = functools.partial(jnp.dot, preferred_element_type=F32)
    return d(m, x0) + d(m, x1) + d(m, x2)


def _mm_exact_rhs(x, m01):
    m = m01.astype(_MXU_DTYPE)
    x0 = x.astype(_MXU_DTYPE)
    r1 = x - x0.astype(F32)
    x1 = r1.astype(_MXU_DTYPE)
    x2 = (r1 - x1.astype(F32)).astype(_MXU_DTYPE)
    d = functools.partial(jnp.dot, preferred_element_type=F32)
    return d(x0, m) + d(x1, m) + d(x2, m)


def _rms(x, g):
    return x * lax.rsqrt(jnp.mean(x * x, axis=-1, keepdims=True) + NORM_EPS) * g


def _sigmoid(x):
    return 1.0 / (1.0 + jnp.exp(-x))


def _silu(x):
    return x * _sigmoid(x)


def _softplus(x):
    return jnp.maximum(x, 0.0) + jnp.log1p(jnp.exp(-jnp.abs(x)))


def _block_tri(n, blk, strict=False):
    r = lax.broadcasted_iota(jnp.int32, (n, n), 0)
    c = lax.broadcasted_iota(jnp.int32, (n, n), 1)
    same = (r // blk) == (c // blk)
    low = (c < r) if strict else (c <= r)
    return same & low


def _head_sums(x, hd):
    width = min(_MXU_WIDTH, x.shape[1])
    r = lax.broadcasted_iota(jnp.int32, (width, width), 0)
    c = lax.broadcasted_iota(jnp.int32, (width, width), 1)
    ones = ((r // hd) == (c // hd)).astype(F32)
    parts = [_mm_exact_rhs(x[:, i:i + width], ones) for i in range(0, x.shape[1], width)]
    return parts[0] if len(parts) == 1 else jnp.concatenate(parts, axis=1)


def _ffn_kernel(x_ref, g_ref, wgu_ref, wd_ref, fg_ref, o_ref, h_scr, acc_scr, *, tf, final):
    dff = wd_ref.shape[0]
    x = x_ref[...]
    h_scr[...] = _rms(x, g_ref[...]).astype(_MXU_DTYPE)
    for c in range(dff // tf):
        h = h_scr[...]
        gate = jnp.dot(h, wgu_ref[:, c * tf:(c + 1) * tf], preferred_element_type=F32)
        up = jnp.dot(h, wgu_ref[:, dff + c * tf:dff + (c + 1) * tf], preferred_element_type=F32)
        act = (_silu(gate) * up).astype(_MXU_DTYPE)
        part = jnp.dot(act, wd_ref[c * tf:(c + 1) * tf, :], preferred_element_type=F32)
        if c == 0:
            acc_scr[...] = part
        else:
            acc_scr[...] += part
    y = x + 0.5 * acc_scr[...]
    if final:
        y = _rms(y, fg_ref[...])
    o_ref[...] = y


def _const_spec(shape):
    nd = len(shape)
    return pl.BlockSpec(shape, lambda *_: (0,) * nd, pipeline_mode=pl.Buffered(1))


def _layer_spec(stacked_shape, layer):
    nd = len(stacked_shape) - 1
    return pl.BlockSpec((None,) + tuple(stacked_shape[1:]), lambda *_: (layer,) + (0,) * nd,
                        pipeline_mode=pl.Buffered(1))


def _ffn(x2d, g, wgu, wd, layer, final_g, tm, tf):
    n, d = x2d.shape
    final = final_g is not None
    fg = final_g if final else g
    return pl.pallas_call(
        functools.partial(_ffn_kernel, tf=tf, final=final),
        out_shape=jax.ShapeDtypeStruct((n, d), F32),
        grid=(n // tm,),
        in_specs=[
            pl.BlockSpec((tm, d), lambda i: (i, 0)),
            _const_spec((1, d)),
            _layer_spec(wgu.shape, layer),
            _layer_spec(wd.shape, layer),
            _const_spec((1, d)),
        ],
        out_specs=pl.BlockSpec((tm, d), lambda i: (i, 0)),
        scratch_shapes=[pltpu.VMEM((tm, d), _MXU_DTYPE), pltpu.VMEM((tm, d), F32)],
        compiler_params=pltpu.CompilerParams(dimension_semantics=("parallel",),
                                             vmem_limit_bytes=_VMEM_LIMIT_BYTES),
        name="ffn",
    )(x2d, g.reshape(1, d), wgu, wd, fg.reshape(1, d))


def _mem_kv_kernel(x_ref, g_ref, w_ref, k_ref, v_ref):
    d = x_ref.shape[-1]
    kv = _mm(_rms(x_ref[...], g_ref[...]), w_ref[...])
    k_ref[...] = kv[:, :d]
    v_ref[...] = kv[:, d:]


def _mem_kv(x2d, g, w, tm):
    n, d = x2d.shape
    depth = w.shape[0]
    out_spec = pl.BlockSpec((None, tm, d), lambda l, i: (l, i, 0))
    out_shape = jax.ShapeDtypeStruct((depth, n, d), F32)
    return pl.pallas_call(
        _mem_kv_kernel,
        out_shape=(out_shape, out_shape),
        grid=(depth, n // tm),
        in_specs=[pl.BlockSpec((tm, d), lambda l, i: (i, 0)),
                  pl.BlockSpec((None, 1, d), lambda l, i: (l, 0, 0)),
                  pl.BlockSpec((None, d, 2 * d), lambda l, i: (l, 0, 0))],
        out_specs=(out_spec, out_spec),
        compiler_params=pltpu.CompilerParams(dimension_semantics=("parallel", "parallel"),
                                             vmem_limit_bytes=_VMEM_LIMIT_BYTES),
        name="mem_kv",
    )(x2d, g.reshape(depth, 1, d), w)


def _xattn_kernel(x_ref, g_ref, wq_ref, wo_ref, mk_ref, mv_ref, o_ref, q_scr, a_scr, *cache_scr,
                  nb, tt, n_heads, layer, head_split):
    d = x_ref.shape[-1]
    hd = d // n_heads
    rows = nb * tt
    scale = hd ** -0.5

    if head_split:
        kbuf, vbuf, sem = cache_scr
        i = pl.program_id(0)
        slot = i % 2

        def slab_copies(step, to_slot):
            seqs = pl.ds(step * nb, nb)
            return [pltpu.make_async_copy(src.at[layer, seqs, :, h, :], buf.at[to_slot, h], sem.at[to_slot, kv, h])
                    for kv, (src, buf) in enumerate(((mk_ref, kbuf), (mv_ref, vbuf))) for h in range(n_heads)]

        @pl.when(i == 0)
        def _():
            for cp in slab_copies(0, 0):
                cp.start()

        @pl.when(i + 1 < pl.num_programs(0))
        def _():
            for cp in slab_copies(i + 1, 1 - slot):
                cp.start()

        keys = lambda s, h: kbuf[slot, h, s]
        vals = lambda s, h: vbuf[slot, h, s]
    else:
        keys = lambda s, h: mk_ref[s, :, h * hd:(h + 1) * hd]
        vals = lambda s, h: mv_ref[s, :, h * hd:(h + 1) * hd]

    x = x_ref[...].reshape(rows, d)
    q_scr[...] = _mm(_rms(x, g_ref[...]), wq_ref[...])

    if head_split:
        for cp in slab_copies(i, slot):
            cp.wait()

    def seq_body(s, carry):
        r0 = pl.multiple_of(s * tt, _SUBLANES)
        hs = [slice(h * hd, (h + 1) * hd) for h in range(n_heads)]
        scs = [_mm_nt(q_scr[pl.ds(r0, tt), hs[h]], keys(s, h)) * scale for h in range(n_heads)]
        es = [jnp.exp(sc - jnp.max(sc, axis=-1, keepdims=True)) for sc in scs]
        prs = [e / jnp.sum(e, axis=-1, keepdims=True) for e in es]
        for h in range(n_heads):
            a_scr[pl.ds(r0, tt), hs[h]] = _mm(prs[h], vals(s, h))
        return carry

    lax.fori_loop(0, nb, seq_body, 0)
    o_ref[...] = (x + _mm(a_scr[...], wo_ref[...])).reshape(nb, tt, d)


def _xattn(x, g, wq, wo, mk, mv, layer, nb, tt):
    b, t, d = x.shape
    n_mem = mk.shape[2]
    rows = nb * tt
    head_split = mk.ndim == 5
    if head_split:
        assert t == tt and mk.shape[3] == X_HEADS
        mem_spec = pl.BlockSpec(memory_space=pl.ANY)
        slab = pltpu.VMEM((2, X_HEADS, nb, n_mem, d // X_HEADS), F32)
        cache_scr = [slab, slab, pltpu.SemaphoreType.DMA((2, 2, X_HEADS))]
        semantics = ("arbitrary", "arbitrary")
    else:
        mem_spec = pl.BlockSpec((None, nb, n_mem, d), lambda i, j: (layer, i, 0, 0))
        cache_scr = []
        semantics = ("parallel", "parallel")
    return pl.pallas_call(
        functools.partial(_xattn_kernel, nb=nb, tt=tt, n_heads=X_HEADS, layer=layer, head_split=head_split),
        out_shape=jax.ShapeDtypeStruct((b, t, d), F32),
        grid=(b // nb, t // tt),
        in_specs=[
            pl.BlockSpec((nb, tt, d), lambda i, j: (i, j, 0)),
            _const_spec((1, d)),
            _layer_spec(wq.shape, layer),
            _layer_spec(wo.shape, layer),
            mem_spec,
            mem_spec,
        ],
        out_specs=pl.BlockSpec((nb, tt, d), lambda i, j: (i, j, 0)),
        scratch_shapes=[pltpu.VMEM((rows, d), F32), pltpu.VMEM((rows, d), F32)] + cache_scr,
        compiler_params=pltpu.CompilerParams(dimension_semantics=semantics,
                                             vmem_limit_bytes=_VMEM_LIMIT_BYTES),
        name="xattn",
    )(x, g.reshape(1, d), wq, wo, mk, mv)


def _even_kernel(x_ref, g_ref, win_ref, wout_ref, cw_ref, lb_ref, gn_ref, conv_in_ref, hg_in_ref,
                 xo_ref, conv_out_ref, hg_out_ref,
                 p_scr, y_scr, ext_scr, b_scr, qg_scr, kk_scr, o_scr, st_scr, *, nb, tt, blk):
    j = pl.program_id(1)
    nj = pl.num_programs(1)
    d = x_ref.shape[-1]
    rows = nb * tt
    aw = A_WIDTH
    hd = B_DIM

    @pl.when(j == 0)
    def _():
        conv_out_ref[...] = conv_in_ref[...]

        def init(s, carry):
            for h in range(B_HEADS):
                st_scr[s, h] = hg_in_ref[s, h].T
            return carry

        lax.fori_loop(0, nb, init, 0)

    x = x_ref[...].reshape(rows, d)
    _project(_rms(x, g_ref[...]).astype(_MXU_DTYPE), win_ref, p_scr, 0, EVEN_IN)

    cw = cw_ref[...]
    u = p_scr[:, 2 * aw:3 * aw] * p_scr[:, 0:aw]
    ext_scr[:, _SUBLANES:_SUBLANES + tt, :] = u.reshape(nb, tt, aw)
    ext_scr[:, _SUBLANES - 2:_SUBLANES, :] = conv_out_ref[...]
    conv = (cw[0:1] * ext_scr[:, _SUBLANES - 2:_SUBLANES - 2 + tt, :]
            + cw[1:2] * ext_scr[:, _SUBLANES - 1:_SUBLANES - 1 + tt, :]
            + cw[2:3] * ext_scr[:, _SUBLANES:_SUBLANES + tt, :])
    y_scr[:, 0:aw] = p_scr[:, aw:2 * aw] * conv.reshape(rows, aw)
    conv_out_ref[...] = ext_scr[:, _SUBLANES + tt - 2:_SUBLANES + tt, :]

    lb = lb_ref[...]
    f = lb + (1.0 - lb) * _sigmoid(p_scr[:, 4 * aw:5 * aw])
    kk_scr[...] = 1.0 - f
    b_scr[...] = _mm_exact_lhs(_block_tri(rows, blk).astype(F32), jnp.log2(f))
    qg_scr[...] = _silu(p_scr[:, 3 * aw:4 * aw])
    rowi = lax.broadcasted_iota(jnp.int32, (1, _SUBLANES, hd), 1)
    sp = math.gcd(nb, 2)
    chains = sp * B_HEADS

    def seq_group(sg, carry):
        s0 = sg * sp

        def time_block(k, c2):
            def stack(ref, col0):
                return jnp.stack([ref[pl.ds(pl.multiple_of((s0 + si) * tt + k * blk, _SUBLANES), blk),
                                      col0 + h * hd:col0 + (h + 1) * hd]
                                  for si in range(sp) for h in range(B_HEADS)])

            bb, qg, kkb = stack(b_scr, 0), stack(qg_scr, 0), stack(kk_scr, 0)
            vb = stack(p_scr, 5 * aw)
            st = st_scr[pl.ds(s0, sp)].reshape(chains, hd, hd)
            blast = bb[:, blk - 1:blk, :]
            intra = []
            for p0 in range(0, blk, _SUBLANES):
                bi, qi = bb[:, p0:p0 + _SUBLANES, :], qg[:, p0:p0 + _SUBLANES, :]
                oi = jnp.zeros_like(bi)
                for jj in range(p0 + _SUBLANES):
                    diff = bi - bb[:, jj:jj + 1, :]
                    if jj >= p0:
                        diff = jnp.where(rowi >= jj - p0, diff, -jnp.inf)
                    att = jnp.sum(qi * kkb[:, jj:jj + 1, :] * jnp.exp2(diff), axis=-1, keepdims=True)
                    oi = oi + att * vb[:, jj:jj + 1, :]
                intra.append(oi)
            o = _bmm_nt(qg * jnp.exp2(bb), st) + (intra[0] if len(intra) == 1 else jnp.concatenate(intra, axis=1))
            kd = kkb * jnp.exp2(blast - bb)
            upd = jnp.stack([_mm_tn(vb[c], kd[c]) for c in range(chains)])
            st_scr[pl.ds(s0, sp)] = (st * jnp.exp2(blast) + upd).reshape(sp, B_HEADS, hd, hd)
            for si in range(sp):
                rs = pl.ds(pl.multiple_of((s0 + si) * tt + k * blk, _SUBLANES), blk)
                for h in range(B_HEADS):
                    o_scr[rs, h * hd:(h + 1) * hd] = o[si * B_HEADS + h]
            return c2

        lax.fori_loop(0, tt // blk, time_block, 0)
        return carry

    lax.fori_loop(0, nb // sp, seq_group, 0)

    for h in range(B_HEADS):
        hs = slice(h * hd, (h + 1) * hd)
        o = o_scr[:, hs]
        on = o * lax.rsqrt(jnp.mean(o * o, axis=-1, keepdims=True) + NORM_EPS) * gn_ref[:, hs]
        y_scr[:, aw + h * hd:aw + (h + 1) * hd] = on * _silu(p_scr[:, 6 * aw + h * hd:6 * aw + (h + 1) * hd])

    xo_ref[...] = (x + _mm(y_scr[...], wout_ref[...])).reshape(nb, tt, d)

    @pl.when(j == nj - 1)
    def _():
        def fin(s, carry):
            for h in range(B_HEADS):
                hg_out_ref[s, h] = st_scr[s, h].T
            return carry

        lax.fori_loop(0, nb, fin, 0)


def _even_layer(x, g, win, wout, cw, lb, gn, conv_in, hg_in, nb, tt, blk):
    b, t, d = x.shape
    rows = nb * tt
    return pl.pallas_call(
        functools.partial(_even_kernel, nb=nb, tt=tt, blk=blk),
        out_shape=(jax.ShapeDtypeStruct((b, t, d), F32),
                   jax.ShapeDtypeStruct(conv_in.shape, F32),
                   jax.ShapeDtypeStruct(hg_in.shape, F32)),
        grid=(b // nb, t // tt),
        in_specs=[
            pl.BlockSpec((nb, tt, d), lambda i, j: (i, j, 0)),
            _const_spec((1, d)),
            _const_spec(win.shape),
            _const_spec(wout.shape),
            _const_spec(cw.shape),
            _const_spec((1, A_WIDTH)),
            _const_spec((1, B_HEADS * B_DIM)),
            pl.BlockSpec((nb,) + conv_in.shape[1:], lambda i, j: (i, 0, 0)),
            pl.BlockSpec((nb,) + hg_in.shape[1:], lambda i, j: (i, 0, 0, 0), pipeline_mode=pl.Buffered(1)),
        ],
        out_specs=(
            pl.BlockSpec((nb, tt, d), lambda i, j: (i, j, 0)),
            pl.BlockSpec((nb,) + conv_in.shape[1:], lambda i, j: (i, 0, 0)),
            pl.BlockSpec((nb,) + hg_in.shape[1:], lambda i, j: (i, 0, 0, 0)),
        ),
        scratch_shapes=[
            pltpu.VMEM((rows, EVEN_IN), F32),
            pltpu.VMEM((rows, 2 * A_WIDTH), F32),
            pltpu.VMEM((nb, tt + _SUBLANES, A_WIDTH), F32),
            pltpu.VMEM((rows, A_WIDTH), F32),
            pltpu.VMEM((rows, A_WIDTH), F32),
            pltpu.VMEM((rows, A_WIDTH), F32),
            pltpu.VMEM((rows, A_WIDTH), F32),
            pltpu.VMEM((nb, B_HEADS, B_DIM, B_DIM), F32),
        ],
        compiler_params=pltpu.CompilerParams(dimension_semantics=("parallel", "arbitrary"),
                                             vmem_limit_bytes=_VMEM_LIMIT_BYTES),
        name="even_layer",
    )(x, g.reshape(1, d), win, wout, cw, lb.reshape(1, -1), gn.reshape(1, -1), conv_in, hg_in)


def _odd_kernel(x_ref, g_ref, win_ref, wout_ref, cos_ref, sin_ref, qdec_ref, kdec_ref, cdec_ref, dmask_ref,
                rgn_ref, mu_ref, w0_ref, w2_ref, a0_ref, a2_ref, g2_ref, kk_ref, ka_ref, rk_ref, lng_ref,
                lnb_ref, ret_in_ref, rw_in_ref, sh_in_ref,
                xo_ref, ret_out_ref, rw_out_ref, sh_out_ref,
                p_scr, y_scr, ext_scr, qr_scr, kr_scr, ir_scr,
                rt_scr, at_scr, bt_scr, kt_scr, vd_scr, cum_scr, od_scr, ah_scr, rh_scr, u0_scr, o0_scr,
                *, nb, tt, ch, gl):
    j = pl.program_id(1)
    d = x_ref.shape[-1]
    rows = nb * tt
    cw = C_HEADS * C_DIM
    hd = C_DIM
    nd = D_HDIM
    n_groups = rows // gl
    cpg = gl // ch
    cps = tt // ch

    @pl.when(j == 0)
    def _():
        ret_out_ref[...] = ret_in_ref[...]
        rw_out_ref[...] = rw_in_ref[...]
        sh_out_ref[...] = sh_in_ref[...]

    x = x_ref[...].reshape(rows, d)
    h_in = _rms(x, g_ref[...]).astype(_MXU_DTYPE)
    gpi = math.gcd(n_groups, _SEQS_PER_PASS)
    sp = math.gcd(nb, _SEQS_PER_PASS)

    def chain_rows(s0, k):
        return [pl.ds(pl.multiple_of((s0 + si) * tt + k * ch, _SUBLANES), ch) for si in range(sp)]

    _project(h_in, win_ref, p_scr, C_IN, ODD_IN)

    pd = p_scr[:, C_IN:ODD_IN]
    ext_scr[:, _SUBLANES:_SUBLANES + tt, :] = pd.reshape(nb, tt, D_IN)
    ext_scr[:, _SUBLANES - 1:_SUBLANES, :] = sh_out_ref[...]
    prev = ext_scr[:, _SUBLANES - 1:_SUBLANES - 1 + tt, :].reshape(rows, D_IN)
    sh_out_ref[...] = ext_scr[:, _SUBLANES + tt - 1:_SUBLANES + tt, :]
    pm = pd + mu_ref[...] * (prev - pd)
    dw = D_WIDTH
    r = pm[:, 0:dw]
    kd = pm[:, dw:2 * dw]
    vd = pm[:, 2 * dw:3 * dw]
    w_dn = pm[:, 3 * dw:3 * dw + 64]
    a_dn = pm[:, 3 * dw + 64:3 * dw + 128]
    g_dn = pm[:, 3 * dw + 128:3 * dw + 256]
    logdec = -math.exp(-0.5) * _sigmoid(w0_ref[...] + _mm(jnp.tanh(w_dn), w2_ref[...]))
    a = _sigmoid(a0_ref[...] + _mm(a_dn, a2_ref[...]))
    gate = _mm(_sigmoid(g_dn), g2_ref[...])
    kk0 = kd * kk_ref[...]
    kk = kk0 / jnp.maximum(jnp.sqrt(_head_sums(kk0 * kk0, nd)), 1e-12)
    kd2 = kd * (1.0 + (a - 1.0) * ka_ref[...])
    cum = _mm_exact_lhs(_block_tri(rows, ch).astype(F32), logdec)
    cum_scr[...] = cum
    rt_scr[...] = r * jnp.exp(cum)
    at_scr[...] = -kk * jnp.exp(cum - logdec)
    e_neg = jnp.exp(-cum)
    bt_scr[...] = kk * a * e_neg
    kt_scr[...] = kd2 * e_neg
    vd_scr[...] = vd
    bonus = _head_sums(r * kd2 * rk_ref[...], nd) * vd

    _project(h_in, win_ref, p_scr, 0, C_IN)
    cos = cos_ref[0]
    sin = sin_ref[0]
    for h in range(C_HEADS):
        hs = slice(h * hd, (h + 1) * hd)
        qh = p_scr[:, h * hd:(h + 1) * hd]
        kh = p_scr[:, cw + h * hd:cw + (h + 1) * hd]
        qr_scr[:, hs] = qh * cos + pltpu.roll(qh, hd // 2, axis=1) * sin
        kr_scr[:, hs] = (kh * cos + pltpu.roll(kh, hd // 2, axis=1) * sin) * (C_DIM ** -0.5)

    def ret_intra(it, carry):
        rows_g = [pl.ds(pl.multiple_of((it * gpi + g) * gl, _SUBLANES), gl) for g in range(gpi)]
        stack = lambda ref, col0: jnp.stack([ref[rg, col0 + h * hd:col0 + (h + 1) * hd]
                                             for rg in rows_g for h in range(C_HEADS)])
        dm = dmask_ref[...]
        dm = dm if gpi == 1 else jnp.concatenate([dm] * gpi, axis=0)
        o = _bmm(_bmm_nt(stack(qr_scr, 0), stack(kr_scr, 0)) * dm, stack(p_scr, 2 * cw))
        for g, rg in enumerate(rows_g):
            for h in range(C_HEADS):
                ir_scr[rg, h * hd:(h + 1) * hd] = o[g * C_HEADS + h]
        return carry

    lax.fori_loop(0, n_groups // gpi, ret_intra, 0)

    per_chain = lambda ref, r: jnp.stack([ref[r, h * hd:(h + 1) * hd] for _ in range(sp) for h in range(C_HEADS)])
    qdec_c = per_chain(qdec_ref, slice(0, ch))
    kdec_c = per_chain(kdec_ref, slice(0, ch))
    cdec_c = per_chain(cdec_ref, slice(0, 1))

    def ret_seq_group(sg, carry):
        s0 = sg * sp

        def ret_chunk(k, c2):
            rows_c = chain_rows(s0, k)
            stack = lambda ref, col0: jnp.stack([ref[rc, col0 + h * hd:col0 + (h + 1) * hd]
                                                 for rc in rows_c for h in range(C_HEADS)])
            qc, kc, vc = stack(qr_scr, 0), stack(kr_scr, 0), stack(p_scr, 2 * cw)
            st = ret_out_ref[pl.ds(s0, sp)].reshape(sp * C_HEADS, hd, hd)
            inter = _bmm(qc, st) * qdec_c
            kcd = kc * kdec_c
            upd = jnp.stack([_mm_tn(kcd[c], vc[c]) for c in range(sp * C_HEADS)])
            ret_out_ref[pl.ds(s0, sp)] = (cdec_c * st + upd).reshape(sp, C_HEADS, hd, hd)
            for si, rc in enumerate(rows_c):
                for h in range(C_HEADS):
                    ir_scr[rc, h * hd:(h + 1) * hd] += inter[si * C_HEADS + h]
            return c2

        lax.fori_loop(0, cps, ret_chunk, 0)
        return carry

    lax.fori_loop(0, nb // sp, ret_seq_group, 0)

    for h in range(C_HEADS):
        hs = slice(h * hd, (h + 1) * hd)
        o = ir_scr[:, hs]
        on = o * lax.rsqrt(jnp.mean(o * o, axis=-1, keepdims=True) + NORM_EPS) * rgn_ref[:, hs]
        y_scr[:, hs] = on * _silu(p_scr[:, 3 * cw + h * hd:3 * cw + (h + 1) * hd])

    strict = _block_tri(gl, ch, strict=True)
    incl = _block_tri(gl, ch)
    eye = (lax.broadcasted_iota(jnp.int32, (gl, gl), 0) == lax.broadcasted_iota(jnp.int32, (gl, gl), 1)).astype(F32)
    n_dbl = int(math.log2(ch))

    def rw_prepare(it, carry):
        def heads(ref):
            return jnp.stack([ref[pl.ds(pl.multiple_of((it * gpi + g) * gl, _SUBLANES), gl), h * nd:(h + 1) * nd]
                              for g in range(gpi) for h in range(D_HEADS)])

        at, rt, bt, kt, vv = (heads(ref) for ref in (at_scr, rt_scr, bt_scr, kt_scr, vd_scr))
        sc = _bmm_nt(jnp.concatenate([at, rt], axis=1), jnp.concatenate([bt, kt], axis=1))
        n_ab = jnp.where(strict, sc[:, 0:gl, 0:gl], 0.0)
        a_ak = jnp.where(strict, sc[:, 0:gl, gl:2 * gl], 0.0)
        r_rb = jnp.where(incl, sc[:, gl:2 * gl, 0:gl], 0.0)
        r_rk = jnp.where(incl, sc[:, gl:2 * gl, gl:2 * gl], 0.0)
        xinv = eye + n_ab
        pw = n_ab
        for _ in range(n_dbl - 1):
            pw = _bmm(pw, pw)
            xinv = xinv + _bmm(pw, xinv)
        resid = eye - (xinv - _bmm_hi(n_ab, xinv))
        xinv = xinv + _bmm(xinv, resid)
        xa = _bmm(xinv, jnp.concatenate([at, _bmm(a_ak, vv)], axis=2))
        ah = xa[:, :, 0:nd]
        u0 = xa[:, :, nd:2 * nd]
        ra = _bmm(r_rb, xa)
        rh = rt + ra[:, :, 0:nd]
        o0 = ra[:, :, nd:2 * nd] + _bmm(r_rk, vv)
        for g in range(gpi):
            part = slice(g * D_HEADS, (g + 1) * D_HEADS)
            ah_scr[it * gpi + g] = ah[part]
            u0_scr[it * gpi + g] = u0[part]
            rh_scr[it * gpi + g] = rh[part]
            o0_scr[it * gpi + g] = o0[part]
        return carry

    lax.fori_loop(0, n_groups // gpi, rw_prepare, 0)

    n_chains = sp * D_HEADS

    def rw_seq_group(sg, carry):
        s0 = sg * sp

        def rw_chunk(k, c2):
            rows_c = chain_rows(s0, k)
            where = []
            for si in range(sp):
                cidx = (s0 + si) * cps + k
                where.append((cidx // cpg, pl.ds(pl.multiple_of((cidx % cpg) * ch, _SUBLANES), ch)))
            prep = lambda ref: jnp.stack([ref[gi, h, ls, :] for gi, ls in where for h in range(D_HEADS)])
            rowsl = lambda ref: jnp.stack([ref[rc, h * nd:(h + 1) * nd] for rc in rows_c for h in range(D_HEADS)])
            st = rw_out_ref[pl.ds(s0, sp)].reshape(n_chains, nd, nd)
            ars = _bmm_nt(jnp.concatenate([prep(ah_scr), prep(rh_scr)], axis=1), st)
            uv = jnp.concatenate([ars[:, 0:ch, :] + prep(u0_scr), rowsl(vd_scr)], axis=1)
            bk = jnp.concatenate([rowsl(bt_scr), rowsl(kt_scr)], axis=1)
            upd = jnp.stack([_mm_tn(uv[c], bk[c]) for c in range(n_chains)])
            glast = jnp.stack([cum_scr[pl.ds(pl.multiple_of((s0 + si) * tt + (k + 1) * ch - _SUBLANES, _SUBLANES),
                                             _SUBLANES), h * nd:(h + 1) * nd][_SUBLANES - 1:_SUBLANES]
                               for si in range(sp) for h in range(D_HEADS)])
            rw_out_ref[pl.ds(s0, sp)] = ((st + upd) * jnp.exp(glast)).reshape(sp, D_HEADS, nd, nd)
            o = ars[:, ch:2 * ch, :] + prep(o0_scr)
            for si, rc in enumerate(rows_c):
                for h in range(D_HEADS):
                    od_scr[rc, h * nd:(h + 1) * nd] = o[si * D_HEADS + h]
            return c2

        lax.fori_loop(0, cps, rw_chunk, 0)
        return carry

    lax.fori_loop(0, nb // sp, rw_seq_group, 0)

    o_d = od_scr[...]
    mean = _head_sums(o_d, nd) * (1.0 / nd)
    xc = o_d - mean
    var = _head_sums(xc * xc, nd) * (1.0 / nd)
    on = xc * lax.rsqrt(var + RWKV_GN_EPS) * lng_ref[...] + lnb_ref[...]
    y_scr[:, cw:cw + dw] = (on + bonus) * gate

    xo_ref[...] = (x + _mm(y_scr[...], wout_ref[...])).reshape(nb, tt, d)


def _odd_layer(x, g, win, wout, tabs, rgn, rw, ret_in, rw_in, sh_in, nb, tt, ch, gl):
    b, t, d = x.shape
    rows = nb * tt
    cos, sin, qdec, kdec, cdec, dmask = tabs
    sh3 = sh_in.reshape(b, 1, D_IN)
    vecs = [rw[k].reshape(1, -1) for k in ("mu", "w0")] + [rw["w2"]] + [rw["a0"].reshape(1, -1), rw["a2"], rw["g2"]] + \
           [rw[k].reshape(1, -1) for k in ("k_k", "k_a", "r_k", "lnx_g", "lnx_b")]
    w512 = pltpu.VMEM((rows, D_WIDTH), F32)
    outs = pl.pallas_call(
        functools.partial(_odd_kernel, nb=nb, tt=tt, ch=ch, gl=gl),
        out_shape=(jax.ShapeDtypeStruct((b, t, d), F32),
                   jax.ShapeDtypeStruct(ret_in.shape, F32),
                   jax.ShapeDtypeStruct(rw_in.shape, F32),
                   jax.ShapeDtypeStruct(sh3.shape, F32)),
        grid=(b // nb, t // tt),
        in_specs=[
            pl.BlockSpec((nb, tt, d), lambda i, j: (i, j, 0)),
            _const_spec((1, d)),
            _const_spec(win.shape),
            _const_spec(wout.shape),
            pl.BlockSpec((1, rows, C_DIM), lambda i, j: (j, 0, 0)),
            pl.BlockSpec((1, rows, C_DIM), lambda i, j: (j, 0, 0)),
            _const_spec(qdec.shape),
            _const_spec(kdec.shape),
            _const_spec(cdec.shape),
            _const_spec(dmask.shape),
            _const_spec((1, C_HEADS * C_DIM)),
        ] + [_const_spec(v.shape) for v in vecs] + [
            pl.BlockSpec((nb,) + ret_in.shape[1:], lambda i, j: (i, 0, 0, 0), pipeline_mode=pl.Buffered(1)),
            pl.BlockSpec((nb,) + rw_in.shape[1:], lambda i, j: (i, 0, 0, 0), pipeline_mode=pl.Buffered(1)),
            pl.BlockSpec((nb, 1, D_IN), lambda i, j: (i, 0, 0)),
        ],
        out_specs=(
            pl.BlockSpec((nb, tt, d), lambda i, j: (i, j, 0)),
            pl.BlockSpec((nb,) + ret_in.shape[1:], lambda i, j: (i, 0, 0, 0)),
            pl.BlockSpec((nb,) + rw_in.shape[1:], lambda i, j: (i, 0, 0, 0)),
            pl.BlockSpec((nb, 1, D_IN), lambda i, j: (i, 0, 0)),
        ),
        scratch_shapes=[
            pltpu.VMEM((rows, ODD_IN), F32),
            pltpu.VMEM((rows, C_HEADS * C_DIM + D_WIDTH), F32),
            pltpu.VMEM((nb, tt + _SUBLANES, D_IN), F32),
        ] + [w512] * 10 + [pltpu.VMEM((rows // gl, D_HEADS, gl, D_HDIM), F32)] * 4,
        compiler_params=pltpu.CompilerParams(dimension_semantics=("parallel", "arbitrary"),
                                             vmem_limit_bytes=_VMEM_LIMIT_BYTES),
        name="odd_layer",
    )(x, g.reshape(1, d), win, wout, cos, sin, qdec, kdec, cdec, dmask, rgn.reshape(1, -1), *vecs,
      ret_in, rw_in, sh3)
    xo, ret_o, rw_o, sh_o = outs
    return xo, ret_o, rw_o, sh_o.reshape(b, D_IN)


def _odd_tables(pos0, t, nb, tt, ch, gl):
    half = C_DIM // 2
    inv = ROPE_BASE ** (-jnp.arange(half, dtype=F32) / half)
    pos = pos0 + jnp.arange(t, dtype=jnp.int32)
    ang = pos.astype(F32)[:, None] * inv[None, :]
    cos = jnp.cos(ang)
    sin = jnp.sin(ang)
    cosf = jnp.concatenate([cos, cos], axis=-1).reshape(t // tt, tt, C_DIM)
    sinf = jnp.concatenate([-sin, sin], axis=-1).reshape(t // tt, tt, C_DIM)
    cosf = jnp.tile(cosf, (1, nb, 1))
    sinf = jnp.tile(sinf, (1, nb, 1))
    lg = jnp.log1p(-jnp.exp2(-5.0 - jnp.arange(C_HEADS, dtype=F32)))[:, None]
    idx = jnp.arange(ch, dtype=F32)
    q_dec = jnp.exp(lg * (idx + 1.0))
    k_dec = jnp.exp(lg * (ch - 1.0 - idx))
    rel = idx[:, None] - idx[None, :]
    dm = jnp.where(rel >= 0, jnp.exp(lg[:, :, None] * jnp.maximum(rel, 0.0)), 0.0)
    c_dec = jnp.exp(lg * ch)
    cpg = gl // ch
    expand = lambda v: jnp.repeat(jnp.tile(v.T, (cpg, 1)), C_DIM, axis=1)
    qdec = expand(q_dec)
    kdec = expand(k_dec)
    cdec = jnp.repeat(c_dec.T, C_DIM, axis=1)
    gi = jnp.arange(gl)
    same = (gi[:, None] // ch) == (gi[None, :] // ch)
    dmask = jnp.where(same[None], jnp.tile(dm, (1, cpg, cpg)), 0.0)
    return cosf, sinf, qdec, kdec, cdec, dmask


def _prep_weights(ffn1_w_gu, ffn1_w_down, ffn2_w_gu, ffn2_w_down, even_w_in, even_w_out, odd_w_in, odd_w_out,
                  xattn_wq, xattn_wkv, xattn_wo):
    bf = lambda w: w.astype(_MXU_DTYPE)
    return dict(ffn1=(bf(ffn1_w_gu), bf(ffn1_w_down)), ffn2=(bf(ffn2_w_gu), bf(ffn2_w_down)),
                even_in=bf(even_w_in), even_out=bf(even_w_out), odd_in=bf(odd_w_in), odd_out=bf(odd_w_out),
                wq=bf(xattn_wq), wkv=bf(xattn_wkv), wo=bf(xattn_wo))


def _trunk(x, pos0, conv_in, hg_in, ret_in, rw_in, sh_in, mem_k, mem_v, W, P, cfg):
    b, t, d = x.shape
    depth = P["ffn1_norm"].shape[0]
    lb_all = jnp.cumsum(jax.nn.softmax(P["hgrn_lb"].astype(F32), axis=0), axis=0)
    convs, hgs, rets, rws, shs = [], [], [], [], []
    for l in range(depth):
        jl = l // 2
        wgu, wd = W["ffn1"]
        x = _ffn(x.reshape(b * t, d), P["ffn1_norm"][l], wgu, wd, l, None, cfg["tm"], cfg["tf"]).reshape(b, t, d)
        if l % 2 == 0:
            x, cb, sh = _even_layer(x, P["mix_norm"][l], W["even_in"][jl], W["even_out"][jl], P["conv_w"][jl],
                                    lb_all[jl], P["hgrn_gnorm"][jl], conv_in[jl], hg_in[jl],
                                    cfg["enb"], cfg["ett"], cfg["blk"])
            convs.append(cb)
            hgs.append(sh)
        else:
            tabs = _odd_tables(pos0, t, cfg["nb"], cfg["tt"], cfg["ch"], cfg["gl"])
            rw = {k: P["rwkv_" + k][jl] for k in ("mu", "w0", "w2", "a0", "a2", "g2", "k_k", "k_a", "r_k",
                                                  "lnx_g", "lnx_b")}
            x, sr, sw, ss = _odd_layer(x, P["mix_norm"][l], W["odd_in"][jl], W["odd_out"][jl], tabs,
                                       P["ret_gnorm"][jl], rw, ret_in[jl], rw_in[jl], sh_in[jl],
                                       cfg["nb"], cfg["tt"], cfg["ch"], cfg["gl"])
            rets.append(sr)
            rws.append(sw)
            shs.append(ss)
        x = _xattn(x, P["xattn_norm"][l], W["wq"], W["wo"], mem_k, mem_v, l, cfg["xnb"], cfg["xtt"])
        wgu, wd = W["ffn2"]
        fin = P["final_norm"] if l == depth - 1 else None
        x = _ffn(x.reshape(b * t, d), P["ffn2_norm"][l], wgu, wd, l, fin, cfg["tm"], cfg["tf"]).reshape(b, t, d)
    return x, jnp.stack(convs), jnp.stack(hgs), jnp.stack(rets), jnp.stack(rws), jnp.stack(shs)


def _configs(b, t):
    if t >= 256:
        nb = math.gcd(b, _SEQS_PER_PASS)
        enb = math.gcd(b, 2 * _SEQS_PER_PASS)
        return dict(tm=512, tf=256, nb=nb, tt=256 // nb, enb=enb, ett=512 // enb, blk=16, ch=64, gl=64,
                    xnb=1, xtt=min(t, 512))
    nb = min(b, 128 // t)
    return dict(tm=min(b * t, 512), tf=256, nb=nb, tt=t, enb=nb, ett=t, blk=t, ch=t, gl=nb * t,
                xnb=min(b, 8), xtt=t)


def kernel(x_prompt, x_sample, state_conv, state_hgrn, state_ret, state_rwkv, state_shift, cache_mem_k, cache_mem_v, mem_prompt, ffn1_norm, ffn1_w_gu, ffn1_w_down, mix_norm, even_w_in, even_w_out, conv_w, hgrn_lb, hgrn_gnorm, odd_w_in, odd_w_out, ret_gnorm, rwkv_mu, rwkv_w0, rwkv_w2, rwkv_a0, rwkv_a2, rwkv_g2, rwkv_k_k, rwkv_k_a, rwkv_r_k, rwkv_lnx_g, rwkv_lnx_b, xattn_norm, mem_norm, xattn_wq, xattn_wkv, xattn_wo, ffn2_norm, ffn2_w_gu, ffn2_w_down, final_norm):
    P = dict(ffn1_norm=ffn1_norm, mix_norm=mix_norm, conv_w=conv_w, hgrn_lb=hgrn_lb, hgrn_gnorm=hgrn_gnorm,
             ret_gnorm=ret_gnorm, rwkv_mu=rwkv_mu, rwkv_w0=rwkv_w0, rwkv_w2=rwkv_w2, rwkv_a0=rwkv_a0,
             rwkv_a2=rwkv_a2, rwkv_g2=rwkv_g2, rwkv_k_k=rwkv_k_k, rwkv_k_a=rwkv_k_a,
             rwkv_r_k=rwkv_r_k.reshape(rwkv_r_k.shape[0], -1), rwkv_lnx_g=rwkv_lnx_g, rwkv_lnx_b=rwkv_lnx_b,
             xattn_norm=xattn_norm, ffn2_norm=ffn2_norm, final_norm=final_norm)
    W = _prep_weights(ffn1_w_gu, ffn1_w_down, ffn2_w_gu, ffn2_w_down, even_w_in, even_w_out, odd_w_in,
                      odd_w_out, xattn_wq, xattn_wkv, xattn_wo)
    bp, tp, d = x_prompt.shape
    bs, ts, _ = x_sample.shape
    depth = ffn1_norm.shape[0]
    n_mem = mem_prompt.shape[1]
    hdx = d // X_HEADS

    mem_k_p, mem_v_p = (m.reshape(depth, bp, n_mem, d)
                        for m in _mem_kv(mem_prompt.reshape(bp * n_mem, d), mem_norm, W["wkv"], 256))

    z = lambda ref: jnp.zeros((ref.shape[0], bp) + ref.shape[2:], F32)
    y_p, conv_p, hg_p, ret_p, rw_p, sh_p = _trunk(
        x_prompt, 0, z(state_conv), z(state_hgrn), z(state_ret), z(state_rwkv), z(state_shift),
        mem_k_p, mem_v_p, W, P, _configs(bp, tp))
    past_len = 16384
    y_s, conv_s, hg_s, ret_s, rw_s, sh_s = _trunk(
        x_sample, past_len, state_conv, state_hgrn, state_ret, state_rwkv, state_shift,
        cache_mem_k, cache_mem_v, W, P, _configs(bs, ts))
    heads = lambda m: m.reshape(m.shape[:3] + (X_HEADS, hdx))
    return (y_p, y_s, conv_p, hg_p, ret_p, rw_p, sh_p, heads(mem_k_p), heads(mem_v_p),
            conv_s, hg_s, ret_s, rw_s, sh_s)
```

```python
import functools
import math

import jax
import jax.numpy as jnp
from jax import lax
from jax.experimental import pallas as pl
from jax.experimental.pallas import tpu as pltpu

F32 = jnp.float32
_MXU_DTYPE = jnp.bfloat16

NORM_EPS = 1e-6
RWKV_GN_EPS = 64e-5
ROPE_BASE = 10000.0

_VMEM_LIMIT_BYTES = 56 * 1024 * 1024
_SUBLANES = 8
_PACKED_ROWS = 16
_MXU_WIDTH = 256
_SEQS_PER_PASS = 4

A_WIDTH = 512
B_HEADS, B_DIM = 4, 128
C_HEADS, C_DIM = 4, 128
D_HEADS, D_HDIM = 8, 64
D_WIDTH = D_HEADS * D_HDIM
EVEN_IN = 7 * 512
C_IN = 4 * 512
D_IN = 3 * 512 + 64 + 64 + 128
ODD_IN = C_IN + D_IN
X_HEADS = 4


def _mm(a, b):
    return jnp.dot(a.astype(_MXU_DTYPE), b.astype(_MXU_DTYPE), preferred_element_type=F32)


def _mm_nt(a, b):
    return lax.dot_general(a.astype(_MXU_DTYPE), b.astype(_MXU_DTYPE), (((1,), (1,)), ((), ())),
                           preferred_element_type=F32)


def _mm_tn(a, b):
    k = a.shape[0]
    if k % _PACKED_ROWS:
        pad = _PACKED_ROWS - k % _PACKED_ROWS
        a = jnp.concatenate([a, jnp.zeros((pad, a.shape[1]), a.dtype)], axis=0)
        b = jnp.concatenate([b, jnp.zeros((pad, b.shape[1]), b.dtype)], axis=0)
    return lax.dot_general(a.astype(_MXU_DTYPE), b.astype(_MXU_DTYPE), (((0,), (0,)), ((), ())),
                           preferred_element_type=F32)


def _project(h, w_ref, out_ref, col0, col1, step=_MXU_WIDTH):
    for c in range(col0, col1, step):
        e = min(c + step, col1)
        out_ref[:, c:e] = jnp.dot(h, w_ref[:, c:e], preferred_element_type=F32)


def _split2(a):
    hi = a.astype(_MXU_DTYPE)
    lo = (a - hi.astype(F32)).astype(_MXU_DTYPE)
    return hi, lo


def _mm_hi(a, b):
    ah, al = _split2(a)
    bh, bl = _split2(b)
    d = functools.partial(jnp.dot, preferred_element_type=F32)
    return d(ah, bh) + d(ah, bl) + d(al, bh)


_BATCH_NN = (((2,), (1,)), ((0,), (0,)))
_BATCH_NT = (((2,), (2,)), ((0,), (0,)))


def _bmm(a, b):
    return lax.dot_general(a.astype(_MXU_DTYPE), b.astype(_MXU_DTYPE), _BATCH_NN, preferred_element_type=F32)


def _bmm_nt(a, b):
    return lax.dot_general(a.astype(_MXU_DTYPE), b.astype(_MXU_DTYPE), _BATCH_NT, preferred_element_type=F32)


def _bmm_hi(a, b):
    ah, al = _split2(a)
    bh, bl = _split2(b)
    d = functools.partial(lax.dot_general, dimension_numbers=_BATCH_NN, preferred_element_type=F32)
    return d(ah, bh) + d(ah, bl) + d(al, bh)


def _mm_exact_lhs(m01, x):
    m = m01.astype(_MXU_DTYPE)
    x0 = x.astype(_MXU_DTYPE)
    r1 = x - x0.astype(F32)
    x1 = r1.astype(_MXU_DTYPE)
    x2 = (r1 - x1.astype(F32)).astype(_MXU_DTYPE)
    d = functools.partial(jnp.dot, preferred_element_type=F32)
    return d(m, x0) + d(m, x1) + d(m, x2)


def _mm_exact_rhs(x, m01):
    m = m01.astype(_MXU_DTYPE)
    x0 = x.astype(_MXU_DTYPE)
    r1 = x - x0.astype(F32)
    x1 = r1.astype(_MXU_DTYPE)
    x2 = (r1 - x1.astype(F32)).astype(_MXU_DTYPE)
    d = functools.partial(jnp.dot, preferred_element_type=F32)
    return d(x0, m) + d(x1, m) + d(x2, m)


def _rms(x, g):
    return x * lax.rsqrt(jnp.mean(x * x, axis=-1, keepdims=True) + NORM_EPS) * g


def _sigmoid(x):
    return 1.0 / (1.0 + jnp.exp(-x))


def _silu(x):
    return x * _sigmoid(x)


def _softplus(x):
    return jnp.maximum(x, 0.0) + jnp.log1p(jnp.exp(-jnp.abs(x)))


def _block_tri(n, blk, strict=False):
    r = lax.broadcasted_iota(jnp.int32, (n, n), 0)
    c = lax.broadcasted_iota(jnp.int32, (n, n), 1)
    same = (r // blk) == (c // blk)
    low = (c < r) if strict else (c <= r)
    return same & low


def _head_sums(x, hd):
    width = min(_MXU_WIDTH, x.shape[1])
    r = lax.broadcasted_iota(jnp.int32, (width, width), 0)
    c = lax.broadcasted_iota(jnp.int32, (width, width), 1)
    ones = ((r // hd) == (c // hd)).astype(F32)
    parts = [_mm_exact_rhs(x[:, i:i + width], ones) for i in range(0, x.shape[1], width)]
    return parts[0] if len(parts) == 1 else jnp.concatenate(parts, axis=1)


def _ffn_kernel(x_ref, g_ref, wgu_ref, wd_ref, fg_ref, o_ref, h_scr, acc_scr, *, tf, final):
    dff = wd_ref.shape[0]
    x = x_ref[...]
    h_scr[...] = _rms(x, g_ref[...]).astype(_MXU_DTYPE)
    for c in range(dff // tf):
        h = h_scr[...]
        gate = jnp.dot(h, wgu_ref[:, c * tf:(c + 1) * tf], preferred_element_type=F32)
        up = jnp.dot(h, wgu_ref[:, dff + c * tf:dff + (c + 1) * tf], preferred_element_type=F32)
        act = (_silu(gate) * up).astype(_MXU_DTYPE)
        part = jnp.dot(act, wd_ref[c * tf:(c + 1) * tf, :], preferred_element_type=F32)
        if c == 0:
            acc_scr[...] = part
        else:
            acc_scr[...] += part
    y = x + 0.5 * acc_scr[...]
    if final:
        y = _rms(y, fg_ref[...])
    o_ref[...] = y


def _const_spec(shape):
    nd = len(shape)
    return pl.BlockSpec(shape, lambda *_: (0,) * nd, pipeline_mode=pl.Buffered(1))


def _layer_spec(stacked_shape, layer):
    nd = len(stacked_shape) - 1
    return pl.BlockSpec((None,) + tuple(stacked_shape[1:]), lambda *_: (layer,) + (0,) * nd,
                        pipeline_mode=pl.Buffered(1))


def _ffn(x2d, g, wgu, wd, layer, final_g, tm, tf):
    n, d = x2d.shape
    final = final_g is not None
    fg = final_g if final else g
    return pl.pallas_call(
        functools.partial(_ffn_kernel, tf=tf, final=final),
        out_shape=jax.ShapeDtypeStruct((n, d), F32),
        grid=(n // tm,),
        in_specs=[
            pl.BlockSpec((tm, d), lambda i: (i, 0)),
            _const_spec((1, d)),
            _layer_spec(wgu.shape, layer),
            _layer_spec(wd.shape, layer),
            _const_spec((1, d)),
        ],
        out_specs=pl.BlockSpec((tm, d), lambda i: (i, 0)),
        scratch_shapes=[pltpu.VMEM((tm, d), _MXU_DTYPE), pltpu.VMEM((tm, d), F32)],
        compiler_params=pltpu.CompilerParams(dimension_semantics=("parallel",),
                                             vmem_limit_bytes=_VMEM_LIMIT_BYTES),
        name="ffn",
    )(x2d, g.reshape(1, d), wgu, wd, fg.reshape(1, d))


def _mem_kv_kernel(x_ref, g_ref, w_ref, k_ref, v_ref, ks_ref, vs_ref):
    d = x_ref.shape[-1]
    n_heads, hd = ks_ref.shape[-2:]
    kv = _mm(_rms(x_ref[...], g_ref[...]), w_ref[...])
    k_ref[...] = kv[:, :d]
    v_ref[...] = kv[:, d:]
    for h in range(n_heads):
        ks_ref[:, h, :] = kv[:, h * hd:(h + 1) * hd]
        vs_ref[:, h, :] = kv[:, d + h * hd:d + (h + 1) * hd]


def _mem_kv(mem, g, w, n_heads):
    b, n_mem, d = mem.shape
    depth = w.shape[0]
    flat_spec = pl.BlockSpec((None, None, n_mem, d), lambda l, i: (l, i, 0, 0))
    split_spec = pl.BlockSpec((None, None, n_mem, n_heads, d // n_heads), lambda l, i: (l, i, 0, 0, 0))
    flat_shape = jax.ShapeDtypeStruct((depth, b, n_mem, d), F32)
    split_shape = jax.ShapeDtypeStruct((depth, b, n_mem, n_heads, d // n_heads), F32)
    return pl.pallas_call(
        _mem_kv_kernel,
        out_shape=(flat_shape, flat_shape, split_shape, split_shape),
        grid=(depth, b),
        in_specs=[pl.BlockSpec((None, n_mem, d), lambda l, i: (i, 0, 0)),
                  pl.BlockSpec((None, 1, d), lambda l, i: (l, 0, 0)),
                  pl.BlockSpec((None, d, 2 * d), lambda l, i: (l, 0, 0))],
        out_specs=(flat_spec, flat_spec, split_spec, split_spec),
        compiler_params=pltpu.CompilerParams(dimension_semantics=("parallel", "parallel"),
                                             vmem_limit_bytes=_VMEM_LIMIT_BYTES),
        name="mem_kv",
    )(mem, g.reshape(depth, 1, d), w)


def _xattn_kernel(x_ref, g_ref, wq_ref, wo_ref, mk_ref, mv_ref, o_ref, q_scr, a_scr, *cache_scr,
                  nb, tt, n_heads, layer, head_split):
    d = x_ref.shape[-1]
    hd = d // n_heads
    rows = nb * tt
    scale = hd ** -0.5

    if head_split:
        kbuf, vbuf, sem = cache_scr
        i = pl.program_id(0)
        slot = i % 2

        def slab_copies(step, to_slot):
            seqs = pl.ds(step * nb, nb)
            return [pltpu.make_async_copy(src.at[layer, seqs, :, h, :], buf.at[to_slot, h], sem.at[to_slot, kv, h])
                    for kv, (src, buf) in enumerate(((mk_ref, kbuf), (mv_ref, vbuf))) for h in range(n_heads)]

        @pl.when(i == 0)
        def _():
            for cp in slab_copies(0, 0):
                cp.start()

        @pl.when(i + 1 < pl.num_programs(0))
        def _():
            for cp in slab_copies(i + 1, 1 - slot):
                cp.start()

        keys = lambda s, h: kbuf[slot, h, s]
        vals = lambda s, h: vbuf[slot, h, s]
    else:
        keys = lambda s, h: mk_ref[s, :, h * hd:(h + 1) * hd]
        vals = lambda s, h: mv_ref[s, :, h * hd:(h + 1) * hd]

    x = x_ref[...].reshape(rows, d)
    q_scr[...] = _mm(_rms(x, g_ref[...]), wq_ref[...])

    if head_split:
        for cp in slab_copies(i, slot):
            cp.wait()

    def seq_body(s, carry):
        r0 = pl.multiple_of(s * tt, _SUBLANES)
        hs = [slice(h * hd, (h + 1) * hd) for h in range(n_heads)]
        scs = [_mm_nt(q_scr[pl.ds(r0, tt), hs[h]], keys(s, h)) * scale for h in range(n_heads)]
        es = [jnp.exp(sc - jnp.max(sc, axis=-1, keepdims=True)) for sc in scs]
        prs = [e / jnp.sum(e, axis=-1, keepdims=True) for e in es]
        for h in range(n_heads):
            a_scr[pl.ds(r0, tt), hs[h]] = _mm(prs[h], vals(s, h))
        return carry

    lax.fori_loop(0, nb, seq_body, 0)
    o_ref[...] = (x + _mm(a_scr[...], wo_ref[...])).reshape(nb, tt, d)


def _xattn(x, g, wq, wo, mk, mv, layer, nb, tt):
    b, t, d = x.shape
    n_mem = mk.shape[2]
    rows = nb * tt
    head_split = mk.ndim == 5
    if head_split:
        assert t == tt and mk.shape[3] == X_HEADS
        mem_spec = pl.BlockSpec(memory_space=pl.ANY)
        slab = pltpu.VMEM((2, X_HEADS, nb, n_mem, d // X_HEADS), F32)
        cache_scr = [slab, slab, pltpu.SemaphoreType.DMA((2, 2, X_HEADS))]
        semantics = ("arbitrary", "arbitrary")
    else:
        mem_spec = pl.BlockSpec((None, nb, n_mem, d), lambda i, j: (layer, i, 0, 0))
        cache_scr = []
        semantics = ("parallel", "parallel")
    return pl.pallas_call(
        functools.partial(_xattn_kernel, nb=nb, tt=tt, n_heads=X_HEADS, layer=layer, head_split=head_split),
        out_shape=jax.ShapeDtypeStruct((b, t, d), F32),
        grid=(b // nb, t // tt),
        in_specs=[
            pl.BlockSpec((nb, tt, d), lambda i, j: (i, j, 0)),
            _const_spec((1, d)),
            _layer_spec(wq.shape, layer),
            _layer_spec(wo.shape, layer),
            mem_spec,
            mem_spec,
        ],
        out_specs=pl.BlockSpec((nb, tt, d), lambda i, j: (i, j, 0)),
        scratch_shapes=[pltpu.VMEM((rows, d), F32), pltpu.VMEM((rows, d), F32)] + cache_scr,
        compiler_params=pltpu.CompilerParams(dimension_semantics=semantics,
                                             vmem_limit_bytes=_VMEM_LIMIT_BYTES),
        name="xattn",
    )(x, g.reshape(1, d), wq, wo, mk, mv)


def _even_kernel(x_ref, g_ref, win_ref, wout_ref, cw_ref, lb_ref, gn_ref, conv_in_ref, hg_in_ref,
                 xo_ref, conv_out_ref, hg_out_ref,
                 p_scr, y_scr, ext_scr, b_scr, qg_scr, kk_scr, o_scr, st_scr, *, nb, tt, blk, one_tile):
    j = pl.program_id(1)
    nj = pl.num_programs(1)
    d = x_ref.shape[-1]
    rows = nb * tt
    aw = A_WIDTH
    hd = B_DIM

    def transpose_states(src, dst):
        if one_tile:
            for s in range(nb):
                for h in range(B_HEADS):
                    dst[s, h] = src[s, h].T
        else:
            def body(s, carry):
                for h in range(B_HEADS):
                    dst[s, h] = src[s, h].T
                return carry

            lax.fori_loop(0, nb, body, 0)

    def load_state():
        conv_out_ref[...] = conv_in_ref[...]
        transpose_states(hg_in_ref, st_scr)

    if one_tile:
        load_state()
    else:
        pl.when(j == 0)(load_state)

    x = x_ref[...].reshape(rows, d)
    _project(_rms(x, g_ref[...]).astype(_MXU_DTYPE), win_ref, p_scr, 0, EVEN_IN)

    cw = cw_ref[...]
    u = p_scr[:, 2 * aw:3 * aw] * p_scr[:, 0:aw]
    ext_scr[:, _SUBLANES:_SUBLANES + tt, :] = u.reshape(nb, tt, aw)
    ext_scr[:, _SUBLANES - 2:_SUBLANES, :] = conv_out_ref[...]
    conv = (cw[0:1] * ext_scr[:, _SUBLANES - 2:_SUBLANES - 2 + tt, :]
            + cw[1:2] * ext_scr[:, _SUBLANES - 1:_SUBLANES - 1 + tt, :]
            + cw[2:3] * ext_scr[:, _SUBLANES:_SUBLANES + tt, :])
    y_scr[:, 0:aw] = p_scr[:, aw:2 * aw] * conv.reshape(rows, aw)
    conv_out_ref[...] = ext_scr[:, _SUBLANES + tt - 2:_SUBLANES + tt, :]

    lb = lb_ref[...]
    f = lb + (1.0 - lb) * _sigmoid(p_scr[:, 4 * aw:5 * aw])
    kk_scr[...] = 1.0 - f
    b_scr[...] = _mm_exact_lhs(_block_tri(rows, blk).astype(F32), jnp.log2(f))
    qg_scr[...] = _silu(p_scr[:, 3 * aw:4 * aw])
    rowi = lax.broadcasted_iota(jnp.int32, (1, _SUBLANES, hd), 1)
    sp = math.gcd(nb, 2)
    chains = sp * B_HEADS

    def seq_group(sg, carry):
        s0 = sg * sp

        def time_block(k, c2):
            def stack(ref, col0):
                return jnp.stack([ref[pl.ds(pl.multiple_of((s0 + si) * tt + k * blk, _SUBLANES), blk),
                                      col0 + h * hd:col0 + (h + 1) * hd]
                                  for si in range(sp) for h in range(B_HEADS)])

            bb, qg, kkb = stack(b_scr, 0), stack(qg_scr, 0), stack(kk_scr, 0)
            vb = stack(p_scr, 5 * aw)
            st = st_scr[pl.ds(s0, sp)].reshape(chains, hd, hd)
            blast = bb[:, blk - 1:blk, :]
            intra = []
            for p0 in range(0, blk, _SUBLANES):
                bi, qi = bb[:, p0:p0 + _SUBLANES, :], qg[:, p0:p0 + _SUBLANES, :]
                oi = jnp.zeros_like(bi)
                for jj in range(p0 + _SUBLANES):
                    diff = bi - bb[:, jj:jj + 1, :]
                    if jj >= p0:
                        diff = jnp.where(rowi >= jj - p0, diff, -jnp.inf)
                    att = jnp.sum(qi * kkb[:, jj:jj + 1, :] * jnp.exp2(diff), axis=-1, keepdims=True)
                    oi = oi + att * vb[:, jj:jj + 1, :]
                intra.append(oi)
            o = _bmm_nt(qg * jnp.exp2(bb), st) + (intra[0] if len(intra) == 1 else jnp.concatenate(intra, axis=1))
            kd = kkb * jnp.exp2(blast - bb)
            upd = jnp.stack([_mm_tn(vb[c], kd[c]) for c in range(chains)])
            st_scr[pl.ds(s0, sp)] = (st * jnp.exp2(blast) + upd).reshape(sp, B_HEADS, hd, hd)
            for si in range(sp):
                rs = pl.ds(pl.multiple_of((s0 + si) * tt + k * blk, _SUBLANES), blk)
                for h in range(B_HEADS):
                    o_scr[rs, h * hd:(h + 1) * hd] = o[si * B_HEADS + h]
            return c2

        lax.fori_loop(0, tt // blk, time_block, 0)
        return carry

    lax.fori_loop(0, nb // sp, seq_group, 0)

    for h in range(B_HEADS):
        hs = slice(h * hd, (h + 1) * hd)
        o = o_scr[:, hs]
        on = o * lax.rsqrt(jnp.mean(o * o, axis=-1, keepdims=True) + NORM_EPS) * gn_ref[:, hs]
        y_scr[:, aw + h * hd:aw + (h + 1) * hd] = on * _silu(p_scr[:, 6 * aw + h * hd:6 * aw + (h + 1) * hd])

    xo_ref[...] = (x + _mm(y_scr[...], wout_ref[...])).reshape(nb, tt, d)

    if one_tile:
        transpose_states(st_scr, hg_out_ref)
    else:
        pl.when(j == nj - 1)(lambda: transpose_states(st_scr, hg_out_ref))


def _even_layer(x, g, win, wout, cw, lb, gn, conv_in, hg_in, nb, tt, blk):
    b, t, d = x.shape
    rows = nb * tt
    return pl.pallas_call(
        functools.partial(_even_kernel, nb=nb, tt=tt, blk=blk, one_tile=(t == tt)),
        out_shape=(jax.ShapeDtypeStruct((b, t, d), F32),
                   jax.ShapeDtypeStruct(conv_in.shape, F32),
                   jax.ShapeDtypeStruct(hg_in.shape, F32)),
        grid=(b // nb, t // tt),
        in_specs=[
            pl.BlockSpec((nb, tt, d), lambda i, j: (i, j, 0)),
            _const_spec((1, d)),
            _const_spec(win.shape),
            _const_spec(wout.shape),
            _const_spec(cw.shape),
            _const_spec((1, A_WIDTH)),
            _const_spec((1, B_HEADS * B_DIM)),
            pl.BlockSpec((nb,) + conv_in.shape[1:], lambda i, j: (i, 0, 0)),
            pl.BlockSpec((nb,) + hg_in.shape[1:], lambda i, j: (i, 0, 0, 0), pipeline_mode=pl.Buffered(1)),
        ],
        out_specs=(
            pl.BlockSpec((nb, tt, d), lambda i, j: (i, j, 0)),
            pl.BlockSpec((nb,) + conv_in.shape[1:], lambda i, j: (i, 0, 0)),
            pl.BlockSpec((nb,) + hg_in.shape[1:], lambda i, j: (i, 0, 0, 0)),
        ),
        scratch_shapes=[
            pltpu.VMEM((rows, EVEN_IN), F32),
            pltpu.VMEM((rows, 2 * A_WIDTH), F32),
            pltpu.VMEM((nb, tt + _SUBLANES, A_WIDTH), F32),
            pltpu.VMEM((rows, A_WIDTH), F32),
            pltpu.VMEM((rows, A_WIDTH), F32),
            pltpu.VMEM((rows, A_WIDTH), F32),
            pltpu.VMEM((rows, A_WIDTH), F32),
            pltpu.VMEM((nb, B_HEADS, B_DIM, B_DIM), F32),
        ],
        compiler_params=pltpu.CompilerParams(dimension_semantics=("parallel", "arbitrary"),
                                             vmem_limit_bytes=_VMEM_LIMIT_BYTES),
        name="even_layer",
    )(x, g.reshape(1, d), win, wout, cw, lb.reshape(1, -1), gn.reshape(1, -1), conv_in, hg_in)


def _odd_kernel(x_ref, g_ref, win_ref, wout_ref, cos_ref, sin_ref, qdec_ref, kdec_ref, cdec_ref, dmask_ref,
                rgn_ref, mu_ref, w0_ref, w2_ref, a0_ref, a2_ref, g2_ref, kk_ref, ka_ref, rk_ref, lng_ref,
                lnb_ref, ret_in_ref, rw_in_ref, sh_in_ref,
                xo_ref, ret_out_ref, rw_out_ref, sh_out_ref,
                p_scr, y_scr, ext_scr, qr_scr, kr_scr, ir_scr,
                rt_scr, at_scr, bt_scr, kt_scr, vd_scr, cum_scr, od_scr, ah_scr, rh_scr, u0_scr, o0_scr,
                *, nb, tt, ch, gl, one_tile):
    j = pl.program_id(1)
    d = x_ref.shape[-1]
    rows = nb * tt
    cw = C_HEADS * C_DIM
    hd = C_DIM
    nd = D_HDIM
    n_groups = rows // gl
    cpg = gl // ch
    cps = tt // ch

    def load_state():
        ret_out_ref[...] = ret_in_ref[...]
        rw_out_ref[...] = rw_in_ref[...]
        sh_out_ref[...] = sh_in_ref[...]

    if one_tile:
        load_state()
    else:
        pl.when(j == 0)(load_state)

    x = x_ref[...].reshape(rows, d)
    h_in = _rms(x, g_ref[...]).astype(_MXU_DTYPE)
    gpi = math.gcd(n_groups, _SEQS_PER_PASS)
    sp = math.gcd(nb, _SEQS_PER_PASS)

    def chain_rows(s0, k):
        return [pl.ds(pl.multiple_of((s0 + si) * tt + k * ch, _SUBLANES), ch) for si in range(sp)]

    _project(h_in, win_ref, p_scr, C_IN, ODD_IN)

    pd = p_scr[:, C_IN:ODD_IN]
    ext_scr[:, _SUBLANES:_SUBLANES + tt, :] = pd.reshape(nb, tt, D_IN)
    ext_scr[:, _SUBLANES - 1:_SUBLANES, :] = sh_out_ref[...]
    prev = ext_scr[:, _SUBLANES - 1:_SUBLANES - 1 + tt, :].reshape(rows, D_IN)
    sh_out_ref[...] = ext_scr[:, _SUBLANES + tt - 1:_SUBLANES + tt, :]
    pm = pd + mu_ref[...] * (prev - pd)
    dw = D_WIDTH
    r = pm[:, 0:dw]
    kd = pm[:, dw:2 * dw]
    vd = pm[:, 2 * dw:3 * dw]
    w_dn = pm[:, 3 * dw:3 * dw + 64]
    a_dn = pm[:, 3 * dw + 64:3 * dw + 128]
    g_dn = pm[:, 3 * dw + 128:3 * dw + 256]
    logdec = -math.exp(-0.5) * _sigmoid(w0_ref[...] + _mm(jnp.tanh(w_dn), w2_ref[...]))
    a = _sigmoid(a0_ref[...] + _mm(a_dn, a2_ref[...]))
    gate = _mm(_sigmoid(g_dn), g2_ref[...])
    kk0 = kd * kk_ref[...]
    kk = kk0 / jnp.maximum(jnp.sqrt(_head_sums(kk0 * kk0, nd)), 1e-12)
    kd2 = kd * (1.0 + (a - 1.0) * ka_ref[...])
    cum = _mm_exact_lhs(_block_tri(rows, ch).astype(F32), logdec)
    cum_scr[...] = cum
    rt_scr[...] = r * jnp.exp(cum)
    at_scr[...] = -kk * jnp.exp(cum - logdec)
    e_neg = jnp.exp(-cum)
    bt_scr[...] = kk * a * e_neg
    kt_scr[...] = kd2 * e_neg
    vd_scr[...] = vd
    bonus = _head_sums(r * kd2 * rk_ref[...], nd) * vd

    _project(h_in, win_ref, p_scr, 0, C_IN)
    cos = cos_ref[0]
    sin = sin_ref[0]
    for h in range(C_HEADS):
        hs = slice(h * hd, (h + 1) * hd)
        qh = p_scr[:, h * hd:(h + 1) * hd]
        kh = p_scr[:, cw + h * hd:cw + (h + 1) * hd]
        qr_scr[:, hs] = qh * cos + pltpu.roll(qh, hd // 2, axis=1) * sin
        kr_scr[:, hs] = (kh * cos + pltpu.roll(kh, hd // 2, axis=1) * sin) * (C_DIM ** -0.5)

    def ret_intra(it, carry):
        rows_g = [pl.ds(pl.multiple_of((it * gpi + g) * gl, _SUBLANES), gl) for g in range(gpi)]
        stack = lambda ref, col0: jnp.stack([ref[rg, col0 + h * hd:col0 + (h + 1) * hd]
                                             for rg in rows_g for h in range(C_HEADS)])
        dm = dmask_ref[...]
        dm = dm if gpi == 1 else jnp.concatenate([dm] * gpi, axis=0)
        o = _bmm(_bmm_nt(stack(qr_scr, 0), stack(kr_scr, 0)) * dm, stack(p_scr, 2 * cw))
        for g, rg in enumerate(rows_g):
            for h in range(C_HEADS):
                ir_scr[rg, h * hd:(h + 1) * hd] = o[g * C_HEADS + h]
        return carry

    lax.fori_loop(0, n_groups // gpi, ret_intra, 0)

    per_chain = lambda ref, r: jnp.stack([ref[r, h * hd:(h + 1) * hd] for _ in range(sp) for h in range(C_HEADS)])
    qdec_c = per_chain(qdec_ref, slice(0, ch))
    kdec_c = per_chain(kdec_ref, slice(0, ch))
    cdec_c = per_chain(cdec_ref, slice(0, 1))

    def ret_seq_group(sg, carry):
        s0 = sg * sp

        def ret_chunk(k, c2):
            rows_c = chain_rows(s0, k)
            stack = lambda ref, col0: jnp.stack([ref[rc, col0 + h * hd:col0 + (h + 1) * hd]
                                                 for rc in rows_c for h in range(C_HEADS)])
            qc, kc, vc = stack(qr_scr, 0), stack(kr_scr, 0), stack(p_scr, 2 * cw)
            st = ret_out_ref[pl.ds(s0, sp)].reshape(sp * C_HEADS, hd, hd)
            inter = _bmm(qc, st) * qdec_c
            kcd = kc * kdec_c
            upd = jnp.stack([_mm_tn(kcd[c], vc[c]) for c in range(sp * C_HEADS)])
            ret_out_ref[pl.ds(s0, sp)] = (cdec_c * st + upd).reshape(sp, C_HEADS, hd, hd)
            for si, rc in enumerate(rows_c):
                for h in range(C_HEADS):
                    ir_scr[rc, h * hd:(h + 1) * hd] += inter[si * C_HEADS + h]
            return c2

        lax.fori_loop(0, cps, ret_chunk, 0)
        return carry

    lax.fori_loop(0, nb // sp, ret_seq_group, 0)

    for h in range(C_HEADS):
        hs = slice(h * hd, (h + 1) * hd)
        o = ir_scr[:, hs]
        on = o * lax.rsqrt(jnp.mean(o * o, axis=-1, keepdims=True) + NORM_EPS) * rgn_ref[:, hs]
        y_scr[:, hs] = on * _silu(p_scr[:, 3 * cw + h * hd:3 * cw + (h + 1) * hd])

    strict = _block_tri(gl, ch, strict=True)
    incl = _block_tri(gl, ch)
    eye = (lax.broadcasted_iota(jnp.int32, (gl, gl), 0) == lax.broadcasted_iota(jnp.int32, (gl, gl), 1)).astype(F32)
    n_dbl = int(math.log2(ch))

    def rw_prepare(it, carry):
        def heads(ref):
            return jnp.stack([ref[pl.ds(pl.multiple_of((it * gpi + g) * gl, _SUBLANES), gl), h * nd:(h + 1) * nd]
                              for g in range(gpi) for h in range(D_HEADS)])

        at, rt, bt, kt, vv = (heads(ref) for ref in (at_scr, rt_scr, bt_scr, kt_scr, vd_scr))
        sc = _bmm_nt(jnp.concatenate([at, rt], axis=1), jnp.concatenate([bt, kt], axis=1))
        n_ab = jnp.where(strict, sc[:, 0:gl, 0:gl], 0.0)
        a_ak = jnp.where(strict, sc[:, 0:gl, gl:2 * gl], 0.0)
        r_rb = jnp.where(incl, sc[:, gl:2 * gl, 0:gl], 0.0)
        r_rk = jnp.where(incl, sc[:, gl:2 * gl, gl:2 * gl], 0.0)
        xinv = eye + n_ab
        pw = n_ab
        for _ in range(n_dbl - 1):
            pw = _bmm(pw, pw)
            xinv = xinv + _bmm(pw, xinv)
        resid = eye - (xinv - _bmm_hi(n_ab, xinv))
        xinv = xinv + _bmm(xinv, resid)
        xa = _bmm(xinv, jnp.concatenate([at, _bmm(a_ak, vv)], axis=2))
        ah = xa[:, :, 0:nd]
        u0 = xa[:, :, nd:2 * nd]
        ra = _bmm(r_rb, xa)
        rh = rt + ra[:, :, 0:nd]
        o0 = ra[:, :, nd:2 * nd] + _bmm(r_rk, vv)
        for g in range(gpi):
            part = slice(g * D_HEADS, (g + 1) * D_HEADS)
            ah_scr[it * gpi + g] = ah[part]
            u0_scr[it * gpi + g] = u0[part]
            rh_scr[it * gpi + g] = rh[part]
            o0_scr[it * gpi + g] = o0[part]
        return carry

    lax.fori_loop(0, n_groups // gpi, rw_prepare, 0)

    n_chains = sp * D_HEADS

    def rw_seq_group(sg, carry):
        s0 = sg * sp

        def rw_chunk(k, c2):
            rows_c = chain_rows(s0, k)
            where = []
            for si in range(sp):
                cidx = (s0 + si) * cps + k
                where.append((cidx // cpg, pl.ds(pl.multiple_of((cidx % cpg) * ch, _SUBLANES), ch)))
            prep = lambda ref: jnp.stack([ref[gi, h, ls, :] for gi, ls in where for h in range(D_HEADS)])
            rowsl = lambda ref: jnp.stack([ref[rc, h * nd:(h + 1) * nd] for rc in rows_c for h in range(D_HEADS)])
            st = rw_out_ref[pl.ds(s0, sp)].reshape(n_chains, nd, nd)
            ars = _bmm_nt(jnp.concatenate([prep(ah_scr), prep(rh_scr)], axis=1), st)
            uv = jnp.concatenate([ars[:, 0:ch, :] + prep(u0_scr), rowsl(vd_scr)], axis=1)
            bk = jnp.concatenate([rowsl(bt_scr), rowsl(kt_scr)], axis=1)
            upd = jnp.stack([_mm_tn(uv[c], bk[c]) for c in range(n_chains)])
            glast = jnp.stack([cum_scr[pl.ds(pl.multiple_of((s0 + si) * tt + (k + 1) * ch - _SUBLANES, _SUBLANES),
                                             _SUBLANES), h * nd:(h + 1) * nd][_SUBLANES - 1:_SUBLANES]
                               for si in range(sp) for h in range(D_HEADS)])
            rw_out_ref[pl.ds(s0, sp)] = ((st + upd) * jnp.exp(glast)).reshape(sp, D_HEADS, nd, nd)
            o = ars[:, ch:2 * ch, :] + prep(o0_scr)
            for si, rc in enumerate(rows_c):
                for h in range(D_HEADS):
                    od_scr[rc, h * nd:(h + 1) * nd] = o[si * D_HEADS + h]
            return c2

        lax.fori_loop(0, cps, rw_chunk, 0)
        return carry

    lax.fori_loop(0, nb // sp, rw_seq_group, 0)

    o_d = od_scr[...]
    mean = _head_sums(o_d, nd) * (1.0 / nd)
    xc = o_d - mean
    var = _head_sums(xc * xc, nd) * (1.0 / nd)
    on = xc * lax.rsqrt(var + RWKV_GN_EPS) * lng_ref[...] + lnb_ref[...]
    y_scr[:, cw:cw + dw] = (on + bonus) * gate

    xo_ref[...] = (x + _mm(y_scr[...], wout_ref[...])).reshape(nb, tt, d)


def _odd_layer(x, g, win, wout, tabs, rgn, rw, ret_in, rw_in, sh_in, nb, tt, ch, gl):
    b, t, d = x.shape
    rows = nb * tt
    cos, sin, qdec, kdec, cdec, dmask = tabs
    sh3 = sh_in.reshape(b, 1, D_IN)
    vecs = [rw[k].reshape(1, -1) for k in ("mu", "w0")] + [rw["w2"]] + [rw["a0"].reshape(1, -1), rw["a2"], rw["g2"]] + \
           [rw[k].reshape(1, -1) for k in ("k_k", "k_a", "r_k", "lnx_g", "lnx_b")]
    w512 = pltpu.VMEM((rows, D_WIDTH), F32)
    outs = pl.pallas_call(
        functools.partial(_odd_kernel, nb=nb, tt=tt, ch=ch, gl=gl, one_tile=(t == tt)),
        out_shape=(jax.ShapeDtypeStruct((b, t, d), F32),
                   jax.ShapeDtypeStruct(ret_in.shape, F32),
                   jax.ShapeDtypeStruct(rw_in.shape, F32),
                   jax.ShapeDtypeStruct(sh3.shape, F32)),
        grid=(b // nb, t // tt),
        in_specs=[
            pl.BlockSpec((nb, tt, d), lambda i, j: (i, j, 0)),
            _const_spec((1, d)),
            _const_spec(win.shape),
            _const_spec(wout.shape),
            pl.BlockSpec((1, rows, C_DIM), lambda i, j: (j, 0, 0)),
            pl.BlockSpec((1, rows, C_DIM), lambda i, j: (j, 0, 0)),
            _const_spec(qdec.shape),
            _const_spec(kdec.shape),
            _const_spec(cdec.shape),
            _const_spec(dmask.shape),
            _const_spec((1, C_HEADS * C_DIM)),
        ] + [_const_spec(v.shape) for v in vecs] + [
            pl.BlockSpec((nb,) + ret_in.shape[1:], lambda i, j: (i, 0, 0, 0), pipeline_mode=pl.Buffered(1)),
            pl.BlockSpec((nb,) + rw_in.shape[1:], lambda i, j: (i, 0, 0, 0), pipeline_mode=pl.Buffered(1)),
            pl.BlockSpec((nb, 1, D_IN), lambda i, j: (i, 0, 0)),
        ],
        out_specs=(
            pl.BlockSpec((nb, tt, d), lambda i, j: (i, j, 0)),
            pl.BlockSpec((nb,) + ret_in.shape[1:], lambda i, j: (i, 0, 0, 0)),
            pl.BlockSpec((nb,) + rw_in.shape[1:], lambda i, j: (i, 0, 0, 0)),
            pl.BlockSpec((nb, 1, D_IN), lambda i, j: (i, 0, 0)),
        ),
        scratch_shapes=[
            pltpu.VMEM((rows, ODD_IN), F32),
            pltpu.VMEM((rows, C_HEADS * C_DIM + D_WIDTH), F32),
            pltpu.VMEM((nb, tt + _SUBLANES, D_IN), F32),
        ] + [w512] * 10 + [pltpu.VMEM((rows // gl, D_HEADS, gl, D_HDIM), F32)] * 4,
        compiler_params=pltpu.CompilerParams(dimension_semantics=("parallel", "arbitrary"),
                                             vmem_limit_bytes=_VMEM_LIMIT_BYTES),
        name="odd_layer",
    )(x, g.reshape(1, d), win, wout, cos, sin, qdec, kdec, cdec, dmask, rgn.reshape(1, -1), *vecs,
      ret_in, rw_in, sh3)
    xo, ret_o, rw_o, sh_o = outs
    return xo, ret_o, rw_o, sh_o.reshape(b, D_IN)


def _odd_tables(pos0, t, nb, tt, ch, gl):
    half = C_DIM // 2
    inv = ROPE_BASE ** (-jnp.arange(half, dtype=F32) / half)
    pos = pos0 + jnp.arange(t, dtype=jnp.int32)
    ang = pos.astype(F32)[:, None] * inv[None, :]
    cos = jnp.cos(ang)
    sin = jnp.sin(ang)
    cosf = jnp.concatenate([cos, cos], axis=-1).reshape(t // tt, tt, C_DIM)
    sinf = jnp.concatenate([-sin, sin], axis=-1).reshape(t // tt, tt, C_DIM)
    cosf = jnp.tile(cosf, (1, nb, 1))
    sinf = jnp.tile(sinf, (1, nb, 1))
    lg = jnp.log1p(-jnp.exp2(-5.0 - jnp.arange(C_HEADS, dtype=F32)))[:, None]
    idx = jnp.arange(ch, dtype=F32)
    q_dec = jnp.exp(lg * (idx + 1.0))
    k_dec = jnp.exp(lg * (ch - 1.0 - idx))
    rel = idx[:, None] - idx[None, :]
    dm = jnp.where(rel >= 0, jnp.exp(lg[:, :, None] * jnp.maximum(rel, 0.0)), 0.0)
    c_dec = jnp.exp(lg * ch)
    cpg = gl // ch
    expand = lambda v: jnp.repeat(jnp.tile(v.T, (cpg, 1)), C_DIM, axis=1)
    qdec = expand(q_dec)
    kdec = expand(k_dec)
    cdec = jnp.repeat(c_dec.T, C_DIM, axis=1)
    gi = jnp.arange(gl)
    same = (gi[:, None] // ch) == (gi[None, :] // ch)
    dmask = jnp.where(same[None], jnp.tile(dm, (1, cpg, cpg)), 0.0)
    return cosf, sinf, qdec, kdec, cdec, dmask


def _prep_weights(ffn1_w_gu, ffn1_w_down, ffn2_w_gu, ffn2_w_down, even_w_in, even_w_out, odd_w_in, odd_w_out,
                  xattn_wq, xattn_wkv, xattn_wo):
    bf = lambda w: w.astype(_MXU_DTYPE)
    return dict(ffn1=(bf(ffn1_w_gu), bf(ffn1_w_down)), ffn2=(bf(ffn2_w_gu), bf(ffn2_w_down)),
                even_in=bf(even_w_in), even_out=bf(even_w_out), odd_in=bf(odd_w_in), odd_out=bf(odd_w_out),
                wq=bf(xattn_wq), wkv=bf(xattn_wkv), wo=bf(xattn_wo))


def _trunk(x, pos0, conv_in, hg_in, ret_in, rw_in, sh_in, mem_k, mem_v, W, P, cfg):
    b, t, d = x.shape
    depth = P["ffn1_norm"].shape[0]
    lb_all = jnp.cumsum(jax.nn.softmax(P["hgrn_lb"].astype(F32), axis=0), axis=0)
    convs, hgs, rets, rws, shs = [], [], [], [], []
    for l in range(depth):
        jl = l // 2
        wgu, wd = W["ffn1"]
        x = _ffn(x.reshape(b * t, d), P["ffn1_norm"][l], wgu, wd, l, None, cfg["tm"], cfg["tf"]).reshape(b, t, d)
        if l % 2 == 0:
            x, cb, sh = _even_layer(x, P["mix_norm"][l], W["even_in"][jl], W["even_out"][jl], P["conv_w"][jl],
                                    lb_all[jl], P["hgrn_gnorm"][jl], conv_in[jl], hg_in[jl],
                                    cfg["enb"], cfg["ett"], cfg["blk"])
            convs.append(cb)
            hgs.append(sh)
        else:
            tabs = _odd_tables(pos0, t, cfg["nb"], cfg["tt"], cfg["ch"], cfg["gl"])
            rw = {k: P["rwkv_" + k][jl] for k in ("mu", "w0", "w2", "a0", "a2", "g2", "k_k", "k_a", "r_k",
                                                  "lnx_g", "lnx_b")}
            x, sr, sw, ss = _odd_layer(x, P["mix_norm"][l], W["odd_in"][jl], W["odd_out"][jl], tabs,
                                       P["ret_gnorm"][jl], rw, ret_in[jl], rw_in[jl], sh_in[jl],
                                       cfg["nb"], cfg["tt"], cfg["ch"], cfg["gl"])
            rets.append(sr)
            rws.append(sw)
            shs.append(ss)
        x = _xattn(x, P["xattn_norm"][l], W["wq"], W["wo"], mem_k, mem_v, l, cfg["xnb"], cfg["xtt"])
        wgu, wd = W["ffn2"]
        fin = P["final_norm"] if l == depth - 1 else None
        x = _ffn(x.reshape(b * t, d), P["ffn2_norm"][l], wgu, wd, l, fin, cfg["tm"], cfg["tf"]).reshape(b, t, d)
    return x, jnp.stack(convs), jnp.stack(hgs), jnp.stack(rets), jnp.stack(rws), jnp.stack(shs)


def _configs(b, t):
    if t >= 256:
        nb = math.gcd(b, _SEQS_PER_PASS)
        enb = math.gcd(b, 2 * _SEQS_PER_PASS)
        return dict(tm=512, tf=256, nb=nb, tt=256 // nb, enb=enb, ett=512 // enb, blk=16, ch=64, gl=64,
                    xnb=1, xtt=min(t, 512))
    nb = min(b, 128 // t)
    return dict(tm=min(b * t, 512), tf=256, nb=nb, tt=t, enb=nb, ett=t, blk=t, ch=t, gl=nb * t,
                xnb=min(b, 8), xtt=t)


def kernel(x_prompt, x_sample, state_conv, state_hgrn, state_ret, state_rwkv, state_shift, cache_mem_k, cache_mem_v, mem_prompt, ffn1_norm, ffn1_w_gu, ffn1_w_down, mix_norm, even_w_in, even_w_out, conv_w, hgrn_lb, hgrn_gnorm, odd_w_in, odd_w_out, ret_gnorm, rwkv_mu, rwkv_w0, rwkv_w2, rwkv_a0, rwkv_a2, rwkv_g2, rwkv_k_k, rwkv_k_a, rwkv_r_k, rwkv_lnx_g, rwkv_lnx_b, xattn_norm, mem_norm, xattn_wq, xattn_wkv, xattn_wo, ffn2_norm, ffn2_w_gu, ffn2_w_down, final_norm):
    P = dict(ffn1_norm=ffn1_norm, mix_norm=mix_norm, conv_w=conv_w, hgrn_lb=hgrn_lb, hgrn_gnorm=hgrn_gnorm,
             ret_gnorm=ret_gnorm, rwkv_mu=rwkv_mu, rwkv_w0=rwkv_w0, rwkv_w2=rwkv_w2, rwkv_a0=rwkv_a0,
             rwkv_a2=rwkv_a2, rwkv_g2=rwkv_g2, rwkv_k_k=rwkv_k_k, rwkv_k_a=rwkv_k_a,
             rwkv_r_k=rwkv_r_k.reshape(rwkv_r_k.shape[0], -1), rwkv_lnx_g=rwkv_lnx_g, rwkv_lnx_b=rwkv_lnx_b,
             xattn_norm=xattn_norm, ffn2_norm=ffn2_norm, final_norm=final_norm)
    W = _prep_weights(ffn1_w_gu, ffn1_w_down, ffn2_w_gu, ffn2_w_down, even_w_in, even_w_out, odd_w_in,
                      odd_w_out, xattn_wq, xattn_wkv, xattn_wo)
    bp, tp, d = x_prompt.shape
    bs, ts, _ = x_sample.shape
    depth = ffn1_norm.shape[0]

    mem_k_p, mem_v_p, mem_k_out, mem_v_out = _mem_kv(mem_prompt, mem_norm, W["wkv"], X_HEADS)

    z = lambda ref: jnp.zeros((ref.shape[0], bp) + ref.shape[2:], F32)
    y_p, conv_p, hg_p, ret_p, rw_p, sh_p = _trunk(
        x_prompt, 0, z(state_conv), z(state_hgrn), z(state_ret), z(state_rwkv), z(state_shift),
        mem_k_p, mem_v_p, W, P, _configs(bp, tp))
    past_len = 16384
    y_s, conv_s, hg_s, ret_s, rw_s, sh_s = _trunk(
        x_sample, past_len, state_conv, state_hgrn, state_ret, state_rwkv, state_shift,
        cache_mem_k, cache_mem_v, W, P, _configs(bs, ts))
    return (y_p, y_s, conv_p, hg_p, ret_p, rw_p, sh_p, mem_k_out, mem_v_out,
            conv_s, hg_s, ret_s, rw_s, sh_s)
```

```python
import functools
import math

import jax
import jax.numpy as jnp
from jax import lax
from jax.experimental import pallas as pl
from jax.experimental.pallas import tpu as pltpu

F32 = jnp.float32
_MXU_DTYPE = jnp.bfloat16

PAST_LEN = 16384
NORM_EPS = 1e-6
RWKV_GN_EPS = 64e-5
ROPE_BASE = 10000.0

_VMEM_LIMIT_BYTES = 60 * 1024 * 1024
_SUBLANES = 8
_PACKED_ROWS = 16
_MXU_WIDTH = 256
_SEQS_PER_PASS = 4

A_WIDTH = 512
B_HEADS, B_DIM = 4, 128
C_HEADS, C_DIM = 4, 128
D_HEADS, D_HDIM = 8, 64
D_WIDTH = D_HEADS * D_HDIM
EVEN_IN = 7 * 512
C_IN = 4 * 512
D_IN = 3 * 512 + 64 + 64 + 128
ODD_IN = C_IN + D_IN
X_HEADS = 4


def _mm(a, b):
    return jnp.dot(a.astype(_MXU_DTYPE), b.astype(_MXU_DTYPE), preferred_element_type=F32)


def _mm_nt(a, b):
    return lax.dot_general(a.astype(_MXU_DTYPE), b.astype(_MXU_DTYPE), (((1,), (1,)), ((), ())),
                           preferred_element_type=F32)


def _mm_tn(a, b):
    k = a.shape[0]
    if k % _PACKED_ROWS:
        pad = _PACKED_ROWS - k % _PACKED_ROWS
        a = jnp.concatenate([a, jnp.zeros((pad, a.shape[1]), a.dtype)], axis=0)
        b = jnp.concatenate([b, jnp.zeros((pad, b.shape[1]), b.dtype)], axis=0)
    return lax.dot_general(a.astype(_MXU_DTYPE), b.astype(_MXU_DTYPE), (((0,), (0,)), ((), ())),
                           preferred_element_type=F32)


def _project(h, w_ref, out_ref, col0, col1, step=_MXU_WIDTH):
    for c in range(col0, col1, step):
        e = min(c + step, col1)
        out_ref[:, c:e] = jnp.dot(h, w_ref[:, c:e], preferred_element_type=F32)


def _split2(a):
    hi = a.astype(_MXU_DTYPE)
    lo = (a - hi.astype(F32)).astype(_MXU_DTYPE)
    return hi, lo


def _mm_hi(a, b):
    ah, al = _split2(a)
    bh, bl = _split2(b)
    d = functools.partial(jnp.dot, preferred_element_type=F32)
    return d(ah, bh) + d(ah, bl) + d(al, bh)


_BATCH_NN = (((2,), (1,)), ((0,), (0,)))
_BATCH_NT = (((2,), (2,)), ((0,), (0,)))


def _bmm(a, b):
    return lax.dot_general(a.astype(_MXU_DTYPE), b.astype(_MXU_DTYPE), _BATCH_NN, preferred_element_type=F32)


def _bmm_nt(a, b):
    return lax.dot_general(a.astype(_MXU_DTYPE), b.astype(_MXU_DTYPE), _BATCH_NT, preferred_element_type=F32)


def _bmm_hi(a, b):
    ah, al = _split2(a)
    bh, bl = _split2(b)
    d = functools.partial(lax.dot_general, dimension_numbers=_BATCH_NN, preferred_element_type=F32)
    return d(ah, bh) + d(ah, bl) + d(al, bh)


def _mm_exact_lhs(m01, x):
    m = m01.astype(_MXU_DTYPE)
    x0 = x.astype(_MXU_DTYPE)
    r1 = x - x0.astype(F32)
    x1 = r1.astype(_MXU_DTYPE)
    x2 = (r1 - x1.astype(F32)).astype(_MXU_DTYPE)
    d = functools.partial(jnp.dot, preferred_element_type=F32)
    return d(m, x0) + d(m, x1) + d(m, x2)


def _mm_exact_rhs(x, m01):
    m = m01.astype(_MXU_DTYPE)
    x0 = x.astype(_MXU_DTYPE)
    r1 = x - x0.astype(F32)
    x1 = r1.astype(_MXU_DTYPE)
    x2 = (r1 - x1.astype(F32)).astype(_MXU_DTYPE)
    d = functools.partial(jnp.dot, preferred_element_type=F32)
    return d(x0, m) + d(x1, m) + d(x2, m)


def _rms(x, g):
    return x * lax.rsqrt(jnp.mean(x * x, axis=-1, keepdims=True) + NORM_EPS) * g


def _sigmoid(x):
    return 1.0 / (1.0 + jnp.exp(-x))


def _silu(x):
    return x * _sigmoid(x)


def _softplus(x):
    return jnp.maximum(x, 0.0) + jnp.log1p(jnp.exp(-jnp.abs(x)))


def _block_tri(n, blk, strict=False):
    r = lax.broadcasted_iota(jnp.int32, (n, n), 0)
    c = lax.broadcasted_iota(jnp.int32, (n, n), 1)
    same = (r // blk) == (c // blk)
    low = (c < r) if strict else (c <= r)
    return same & low


def _head_sums(x, hd):
    width = min(_MXU_WIDTH, x.shape[1])
    r = lax.broadcasted_iota(jnp.int32, (width, width), 0)
    c = lax.broadcasted_iota(jnp.int32, (width, width), 1)
    ones = ((r // hd) == (c // hd)).astype(F32)
    parts = [_mm_exact_rhs(x[:, i:i + width], ones) for i in range(0, x.shape[1], width)]
    return parts[0] if len(parts) == 1 else jnp.concatenate(parts, axis=1)


def _ffn_kernel(x_ref, g_ref, wgu_ref, wd_ref, fg_ref, o_ref, h_scr, acc_scr, *, tf, final):
    dff = wd_ref.shape[0]
    x = x_ref[...]
    h_scr[...] = _rms(x, g_ref[...]).astype(_MXU_DTYPE)
    for c in range(dff // tf):
        h = h_scr[...]
        gate = jnp.dot(h, wgu_ref[:, c * tf:(c + 1) * tf], preferred_element_type=F32)
        up = jnp.dot(h, wgu_ref[:, dff + c * tf:dff + (c + 1) * tf], preferred_element_type=F32)
        act = (_silu(gate) * up).astype(_MXU_DTYPE)
        part = jnp.dot(act, wd_ref[c * tf:(c + 1) * tf, :], preferred_element_type=F32)
        if c == 0:
            acc_scr[...] = part
        else:
            acc_scr[...] += part
    y = x + 0.5 * acc_scr[...]
    if final:
        y = _rms(y, fg_ref[...])
    o_ref[...] = y


def _const_spec(shape):
    nd = len(shape)
    return pl.BlockSpec(shape, lambda *_: (0,) * nd, pipeline_mode=pl.Buffered(1))


def _layer_spec(stacked_shape, layer):
    nd = len(stacked_shape) - 1
    return pl.BlockSpec((None,) + tuple(stacked_shape[1:]), lambda *_: (layer,) + (0,) * nd,
                        pipeline_mode=pl.Buffered(1))


def _ffn(x2d, g, wgu, wd, layer, final_g, tm, tf):
    n, d = x2d.shape
    final = final_g is not None
    fg = final_g if final else g
    return pl.pallas_call(
        functools.partial(_ffn_kernel, tf=tf, final=final),
        out_shape=jax.ShapeDtypeStruct((n, d), F32),
        grid=(n // tm,),
        in_specs=[
            pl.BlockSpec((tm, d), lambda i: (i, 0)),
            _const_spec((1, d)),
            _layer_spec(wgu.shape, layer),
            _layer_spec(wd.shape, layer),
            _const_spec((1, d)),
        ],
        out_specs=pl.BlockSpec((tm, d), lambda i: (i, 0)),
        scratch_shapes=[pltpu.VMEM((tm, d), _MXU_DTYPE), pltpu.VMEM((tm, d), F32)],
        compiler_params=pltpu.CompilerParams(dimension_semantics=("parallel",),
                                             vmem_limit_bytes=_VMEM_LIMIT_BYTES),
        name="ffn",
    )(x2d, g.reshape(1, d), wgu, wd, fg.reshape(1, d))


def _mem_kv_kernel(x_ref, g_ref, w_ref, k_ref, v_ref, ks_ref, vs_ref):
    d = x_ref.shape[-1]
    n_heads, hd = ks_ref.shape[-2:]
    kv = _mm(_rms(x_ref[...], g_ref[...]), w_ref[...])
    k_ref[...] = kv[:, :d]
    v_ref[...] = kv[:, d:]
    for h in range(n_heads):
        ks_ref[:, h, :] = kv[:, h * hd:(h + 1) * hd]
        vs_ref[:, h, :] = kv[:, d + h * hd:d + (h + 1) * hd]


def _mem_kv(mem, g, w, n_heads):
    b, n_mem, d = mem.shape
    depth = w.shape[0]
    flat_spec = pl.BlockSpec((None, None, n_mem, d), lambda l, i: (l, i, 0, 0))
    split_spec = pl.BlockSpec((None, None, n_mem, n_heads, d // n_heads), lambda l, i: (l, i, 0, 0, 0))
    flat_shape = jax.ShapeDtypeStruct((depth, b, n_mem, d), F32)
    split_shape = jax.ShapeDtypeStruct((depth, b, n_mem, n_heads, d // n_heads), F32)
    return pl.pallas_call(
        _mem_kv_kernel,
        out_shape=(flat_shape, flat_shape, split_shape, split_shape),
        grid=(depth, b),
        in_specs=[pl.BlockSpec((None, n_mem, d), lambda l, i: (i, 0, 0)),
                  pl.BlockSpec((None, 1, d), lambda l, i: (l, 0, 0)),
                  pl.BlockSpec((None, d, 2 * d), lambda l, i: (l, 0, 0))],
        out_specs=(flat_spec, flat_spec, split_spec, split_spec),
        compiler_params=pltpu.CompilerParams(dimension_semantics=("parallel", "parallel"),
                                             vmem_limit_bytes=_VMEM_LIMIT_BYTES),
        name="mem_kv",
    )(mem, g.reshape(depth, 1, d), w)


def _xattn_kernel(x_ref, g_ref, wq_ref, wo_ref, mk_ref, mv_ref, o_ref, q_scr, a_scr, *cache_scr,
                  nb, tt, n_heads, layer, head_split):
    d = x_ref.shape[-1]
    hd = d // n_heads
    rows = nb * tt
    scale = hd ** -0.5

    if head_split:
        kbuf, vbuf, sem = cache_scr
        i = pl.program_id(0)
        slot = i % 2

        def slab_copies(step, to_slot):
            seqs = pl.ds(step * nb, nb)
            return [pltpu.make_async_copy(src.at[layer, seqs, :, h, :], buf.at[to_slot, h], sem.at[to_slot, kv, h])
                    for kv, (src, buf) in enumerate(((mk_ref, kbuf), (mv_ref, vbuf))) for h in range(n_heads)]

        @pl.when(i == 0)
        def _():
            for cp in slab_copies(0, 0):
                cp.start()

        @pl.when(i + 1 < pl.num_programs(0))
        def _():
            for cp in slab_copies(i + 1, 1 - slot):
                cp.start()

        keys = lambda s, h: kbuf[slot, h, s]
        vals = lambda s, h: vbuf[slot, h, s]
    else:
        keys = lambda s, h: mk_ref[s, :, h * hd:(h + 1) * hd]
        vals = lambda s, h: mv_ref[s, :, h * hd:(h + 1) * hd]

    x = x_ref[...].reshape(rows, d)
    q_scr[...] = _mm(_rms(x, g_ref[...]), wq_ref[...])

    if head_split:
        for cp in slab_copies(i, slot):
            cp.wait()

    spp = math.gcd(nb, 2)

    def seq_body(sg, carry):
        chains = [(sg * spp + si, pl.ds(pl.multiple_of((sg * spp + si) * tt, _SUBLANES), tt),
                   h, slice(h * hd, (h + 1) * hd)) for si in range(spp) for h in range(n_heads)]
        scs = [_mm_nt(q_scr[rs, hs], keys(s, h)) * scale for s, rs, h, hs in chains]
        es = [jnp.exp(sc - jnp.max(sc, axis=-1, keepdims=True)) for sc in scs]
        prs = [e / jnp.sum(e, axis=-1, keepdims=True) for e in es]
        for pr, (s, rs, h, hs) in zip(prs, chains):
            a_scr[rs, hs] = _mm(pr, vals(s, h))
        return carry

    lax.fori_loop(0, nb // spp, seq_body, 0)
    o_ref[...] = (x + _mm(a_scr[...], wo_ref[...])).reshape(nb, tt, d)


def _xattn(x, g, wq, wo, mk, mv, layer, nb, tt):
    b, t, d = x.shape
    n_mem = mk.shape[2]
    rows = nb * tt
    head_split = mk.ndim == 5
    if head_split:
        assert t == tt and mk.shape[3] == X_HEADS
        mem_spec = pl.BlockSpec(memory_space=pl.ANY)
        slab = pltpu.VMEM((2, X_HEADS, nb, n_mem, d // X_HEADS), F32)
        cache_scr = [slab, slab, pltpu.SemaphoreType.DMA((2, 2, X_HEADS))]
        semantics = ("arbitrary", "arbitrary")
    else:
        mem_spec = pl.BlockSpec((None, nb, n_mem, d), lambda i, j: (layer, i, 0, 0))
        cache_scr = []
        semantics = ("parallel", "parallel")
    return pl.pallas_call(
        functools.partial(_xattn_kernel, nb=nb, tt=tt, n_heads=X_HEADS, layer=layer, head_split=head_split),
        out_shape=jax.ShapeDtypeStruct((b, t, d), F32),
        grid=(b // nb, t // tt),
        in_specs=[
            pl.BlockSpec((nb, tt, d), lambda i, j: (i, j, 0)),
            _const_spec((1, d)),
            _layer_spec(wq.shape, layer),
            _layer_spec(wo.shape, layer),
            mem_spec,
            mem_spec,
        ],
        out_specs=pl.BlockSpec((nb, tt, d), lambda i, j: (i, j, 0)),
        scratch_shapes=[pltpu.VMEM((rows, d), F32), pltpu.VMEM((rows, d), F32)] + cache_scr,
        compiler_params=pltpu.CompilerParams(dimension_semantics=semantics,
                                             vmem_limit_bytes=_VMEM_LIMIT_BYTES),
        name="xattn",
    )(x, g.reshape(1, d), wq, wo, mk, mv)


def _even_kernel(x_ref, g_ref, win_ref, wout_ref, cw_ref, lb_ref, gn_ref, conv_in_ref, hg_in_ref,
                 xo_ref, conv_out_ref, hg_out_ref,
                 p_scr, y_scr, ext_scr, b_scr, qg_scr, kk_scr, o_scr, st_scr, *, nb, tt, blk, one_tile):
    j = pl.program_id(1)
    nj = pl.num_programs(1)
    d = x_ref.shape[-1]
    rows = nb * tt
    aw = A_WIDTH
    hd = B_DIM

    def transpose_states(src, dst):
        if one_tile:
            for s in range(nb):
                for h in range(B_HEADS):
                    dst[s, h] = src[s, h].T
        else:
            def body(s, carry):
                for h in range(B_HEADS):
                    dst[s, h] = src[s, h].T
                return carry

            lax.fori_loop(0, nb, body, 0)

    def load_state():
        conv_out_ref[...] = conv_in_ref[...]
        transpose_states(hg_in_ref, st_scr)

    if one_tile:
        load_state()
    else:
        pl.when(j == 0)(load_state)

    x = x_ref[...].reshape(rows, d)
    _project(_rms(x, g_ref[...]).astype(_MXU_DTYPE), win_ref, p_scr, 0, EVEN_IN)

    cw = cw_ref[...]
    u = p_scr[:, 2 * aw:3 * aw] * p_scr[:, 0:aw]
    ext_scr[:, _SUBLANES:_SUBLANES + tt, :] = u.reshape(nb, tt, aw)
    ext_scr[:, _SUBLANES - 2:_SUBLANES, :] = conv_out_ref[...]
    conv = (cw[0:1] * ext_scr[:, _SUBLANES - 2:_SUBLANES - 2 + tt, :]
            + cw[1:2] * ext_scr[:, _SUBLANES - 1:_SUBLANES - 1 + tt, :]
            + cw[2:3] * ext_scr[:, _SUBLANES:_SUBLANES + tt, :])
    y_scr[:, 0:aw] = p_scr[:, aw:2 * aw] * conv.reshape(rows, aw)
    conv_out_ref[...] = ext_scr[:, _SUBLANES + tt - 2:_SUBLANES + tt, :]

    lb = lb_ref[...]
    f = lb + (1.0 - lb) * _sigmoid(p_scr[:, 4 * aw:5 * aw])
    kk_scr[...] = 1.0 - f
    b_scr[...] = _mm_exact_lhs(_block_tri(rows, blk).astype(F32), jnp.log2(f))
    qg_scr[...] = _silu(p_scr[:, 3 * aw:4 * aw])
    rowi = lax.broadcasted_iota(jnp.int32, (1, _SUBLANES, hd), 1)
    sp = math.gcd(nb, 2)
    chains = sp * B_HEADS

    def seq_group(sg, carry):
        s0 = sg * sp

        def time_block(k, c2):
            def stack(ref, col0):
                return jnp.stack([ref[pl.ds(pl.multiple_of((s0 + si) * tt + k * blk, _SUBLANES), blk),
                                      col0 + h * hd:col0 + (h + 1) * hd]
                                  for si in range(sp) for h in range(B_HEADS)])

            bb, qg, kkb = stack(b_scr, 0), stack(qg_scr, 0), stack(kk_scr, 0)
            vb = stack(p_scr, 5 * aw)
            st = st_scr[pl.ds(s0, sp)].reshape(chains, hd, hd)
            blast = bb[:, blk - 1:blk, :]
            intra = []
            for p0 in range(0, blk, _SUBLANES):
                bi, qi = bb[:, p0:p0 + _SUBLANES, :], qg[:, p0:p0 + _SUBLANES, :]
                oi = jnp.zeros_like(bi)
                for jj in range(p0 + _SUBLANES):
                    diff = bi - bb[:, jj:jj + 1, :]
                    if jj >= p0:
                        diff = jnp.where(rowi >= jj - p0, diff, -jnp.inf)
                    att = jnp.sum(qi * kkb[:, jj:jj + 1, :] * jnp.exp2(diff), axis=-1, keepdims=True)
                    oi = oi + att * vb[:, jj:jj + 1, :]
                intra.append(oi)
            o = _bmm_nt(qg * jnp.exp2(bb), st) + (intra[0] if len(intra) == 1 else jnp.concatenate(intra, axis=1))
            kd = kkb * jnp.exp2(blast - bb)
            upd = jnp.stack([_mm_tn(vb[c], kd[c]) for c in range(chains)])
            st_scr[pl.ds(s0, sp)] = (st * jnp.exp2(blast) + upd).reshape(sp, B_HEADS, hd, hd)
            for si in range(sp):
                rs = pl.ds(pl.multiple_of((s0 + si) * tt + k * blk, _SUBLANES), blk)
                for h in range(B_HEADS):
                    o_scr[rs, h * hd:(h + 1) * hd] = o[si * B_HEADS + h]
            return c2

        lax.fori_loop(0, tt // blk, time_block, 0)
        return carry

    lax.fori_loop(0, nb // sp, seq_group, 0)

    for h in range(B_HEADS):
        hs = slice(h * hd, (h + 1) * hd)
        o = o_scr[:, hs]
        on = o * lax.rsqrt(jnp.mean(o * o, axis=-1, keepdims=True) + NORM_EPS) * gn_ref[:, hs]
        y_scr[:, aw + h * hd:aw + (h + 1) * hd] = on * _silu(p_scr[:, 6 * aw + h * hd:6 * aw + (h + 1) * hd])

    xo_ref[...] = (x + _mm(y_scr[...], wout_ref[...])).reshape(nb, tt, d)

    if one_tile:
        transpose_states(st_scr, hg_out_ref)
    else:
        pl.when(j == nj - 1)(lambda: transpose_states(st_scr, hg_out_ref))


def _even_layer(x, g, win, wout, cw, lb, gn, conv_in, hg_in, nb, tt, blk):
    b, t, d = x.shape
    rows = nb * tt
    return pl.pallas_call(
        functools.partial(_even_kernel, nb=nb, tt=tt, blk=blk, one_tile=(t == tt)),
        out_shape=(jax.ShapeDtypeStruct((b, t, d), F32),
                   jax.ShapeDtypeStruct(conv_in.shape, F32),
                   jax.ShapeDtypeStruct(hg_in.shape, F32)),
        grid=(b // nb, t // tt),
        in_specs=[
            pl.BlockSpec((nb, tt, d), lambda i, j: (i, j, 0)),
            _const_spec((1, d)),
            _const_spec(win.shape),
            _const_spec(wout.shape),
            _const_spec(cw.shape),
            _const_spec((1, A_WIDTH)),
            _const_spec((1, B_HEADS * B_DIM)),
            pl.BlockSpec((nb,) + conv_in.shape[1:], lambda i, j: (i, 0, 0)),
            pl.BlockSpec((nb,) + hg_in.shape[1:], lambda i, j: (i, 0, 0, 0)),
        ],
        out_specs=(
            pl.BlockSpec((nb, tt, d), lambda i, j: (i, j, 0)),
            pl.BlockSpec((nb,) + conv_in.shape[1:], lambda i, j: (i, 0, 0)),
            pl.BlockSpec((nb,) + hg_in.shape[1:], lambda i, j: (i, 0, 0, 0)),
        ),
        scratch_shapes=[
            pltpu.VMEM((rows, EVEN_IN), F32),
            pltpu.VMEM((rows, 2 * A_WIDTH), F32),
            pltpu.VMEM((nb, tt + _SUBLANES, A_WIDTH), F32),
            pltpu.VMEM((rows, A_WIDTH), F32),
            pltpu.VMEM((rows, A_WIDTH), F32),
            pltpu.VMEM((rows, A_WIDTH), F32),
            pltpu.VMEM((rows, A_WIDTH), F32),
            pltpu.VMEM((nb, B_HEADS, B_DIM, B_DIM), F32),
        ],
        compiler_params=pltpu.CompilerParams(dimension_semantics=("parallel", "arbitrary"),
                                             vmem_limit_bytes=_VMEM_LIMIT_BYTES),
        name="even_layer",
    )(x, g.reshape(1, d), win, wout, cw, lb.reshape(1, -1), gn.reshape(1, -1), conv_in, hg_in)


def _odd_kernel(x_ref, g_ref, win_ref, wout_ref, cos_ref, sin_ref, qdec_ref, kdec_ref, cdec_ref, dmask_ref,
                rgn_ref, mu_ref, w0_ref, w2_ref, a0_ref, a2_ref, g2_ref, kk_ref, ka_ref, rk_ref, lng_ref,
                lnb_ref, ret_in_ref, rw_in_ref, sh_in_ref,
                xo_ref, ret_out_ref, rw_out_ref, sh_out_ref,
                p_scr, y_scr, ext_scr, qr_scr, kr_scr, ir_scr,
                rt_scr, at_scr, bt_scr, kt_scr, vd_scr, cum_scr, od_scr, ah_scr, rh_scr, u0_scr, o0_scr,
                *, nb, tt, ch, gl, one_tile):
    j = pl.program_id(1)
    d = x_ref.shape[-1]
    rows = nb * tt
    cw = C_HEADS * C_DIM
    hd = C_DIM
    nd = D_HDIM
    n_groups = rows // gl
    cpg = gl // ch
    cps = tt // ch

    def load_state():
        ret_out_ref[...] = ret_in_ref[...]
        rw_out_ref[...] = rw_in_ref[...]
        sh_out_ref[...] = sh_in_ref[...]

    if one_tile and cps == 1:
        sh_out_ref[...] = sh_in_ref[...]
        ret_src, rw_src = ret_in_ref, rw_in_ref
    else:
        if one_tile:
            load_state()
        else:
            pl.when(j == 0)(load_state)
        ret_src, rw_src = ret_out_ref, rw_out_ref

    x = x_ref[...].reshape(rows, d)
    h_in = _rms(x, g_ref[...]).astype(_MXU_DTYPE)
    gpi = math.gcd(n_groups, _SEQS_PER_PASS)
    sp = math.gcd(nb, _SEQS_PER_PASS)

    def chain_rows(s0, k):
        return [pl.ds(pl.multiple_of((s0 + si) * tt + k * ch, _SUBLANES), ch) for si in range(sp)]

    _project(h_in, win_ref, p_scr, C_IN, ODD_IN)

    pd = p_scr[:, C_IN:ODD_IN]
    ext_scr[:, _SUBLANES:_SUBLANES + tt, :] = pd.reshape(nb, tt, D_IN)
    ext_scr[:, _SUBLANES - 1:_SUBLANES, :] = sh_out_ref[...]
    prev = ext_scr[:, _SUBLANES - 1:_SUBLANES - 1 + tt, :].reshape(rows, D_IN)
    sh_out_ref[...] = ext_scr[:, _SUBLANES + tt - 1:_SUBLANES + tt, :]
    pm = pd + mu_ref[...] * (prev - pd)
    dw = D_WIDTH
    r = pm[:, 0:dw]
    kd = pm[:, dw:2 * dw]
    vd = pm[:, 2 * dw:3 * dw]
    w_dn = pm[:, 3 * dw:3 * dw + 64]
    a_dn = pm[:, 3 * dw + 64:3 * dw + 128]
    g_dn = pm[:, 3 * dw + 128:3 * dw + 256]
    logdec = -math.exp(-0.5) * _sigmoid(w0_ref[...] + _mm(jnp.tanh(w_dn), w2_ref[...]))
    a = _sigmoid(a0_ref[...] + _mm(a_dn, a2_ref[...]))
    gate = _mm(_sigmoid(g_dn), g2_ref[...])
    kk0 = kd * kk_ref[...]
    kk = kk0 / jnp.maximum(jnp.sqrt(_head_sums(kk0 * kk0, nd)), 1e-12)
    kd2 = kd * (1.0 + (a - 1.0) * ka_ref[...])
    cum = _mm_exact_lhs(_block_tri(rows, ch).astype(F32), logdec)
    cum_scr[...] = cum
    rt_scr[...] = r * jnp.exp(cum)
    at_scr[...] = -kk * jnp.exp(cum - logdec)
    e_neg = jnp.exp(-cum)
    bt_scr[...] = kk * a * e_neg
    kt_scr[...] = kd2 * e_neg
    vd_scr[...] = vd
    bonus = _head_sums(r * kd2 * rk_ref[...], nd) * vd

    _project(h_in, win_ref, p_scr, 0, C_IN)
    cos = cos_ref[0]
    sin = sin_ref[0]
    for h in range(C_HEADS):
        hs = slice(h * hd, (h + 1) * hd)
        qh = p_scr[:, h * hd:(h + 1) * hd]
        kh = p_scr[:, cw + h * hd:cw + (h + 1) * hd]
        qr_scr[:, hs] = qh * cos + pltpu.roll(qh, hd // 2, axis=1) * sin
        kr_scr[:, hs] = (kh * cos + pltpu.roll(kh, hd // 2, axis=1) * sin) * (C_DIM ** -0.5)

    def ret_intra(it, carry):
        rows_g = [pl.ds(pl.multiple_of((it * gpi + g) * gl, _SUBLANES), gl) for g in range(gpi)]
        stack = lambda ref, col0: jnp.stack([ref[rg, col0 + h * hd:col0 + (h + 1) * hd]
                                             for rg in rows_g for h in range(C_HEADS)])
        dm = dmask_ref[...]
        dm = dm if gpi == 1 else jnp.concatenate([dm] * gpi, axis=0)
        o = _bmm(_bmm_nt(stack(qr_scr, 0), stack(kr_scr, 0)) * dm, stack(p_scr, 2 * cw))
        for g, rg in enumerate(rows_g):
            for h in range(C_HEADS):
                ir_scr[rg, h * hd:(h + 1) * hd] = o[g * C_HEADS + h]
        return carry

    lax.fori_loop(0, n_groups // gpi, ret_intra, 0)

    per_chain = lambda ref, r: jnp.stack([ref[r, h * hd:(h + 1) * hd] for _ in range(sp) for h in range(C_HEADS)])
    qdec_c = per_chain(qdec_ref, slice(0, ch))
    kdec_c = per_chain(kdec_ref, slice(0, ch))
    cdec_c = per_chain(cdec_ref, slice(0, 1))

    def ret_seq_group(sg, carry):
        s0 = sg * sp

        def ret_chunk(k, c2):
            rows_c = chain_rows(s0, k)
            stack = lambda ref, col0: jnp.stack([ref[rc, col0 + h * hd:col0 + (h + 1) * hd]
                                                 for rc in rows_c for h in range(C_HEADS)])
            qc, kc, vc = stack(qr_scr, 0), stack(kr_scr, 0), stack(p_scr, 2 * cw)
            st = ret_src[pl.ds(s0, sp)].reshape(sp * C_HEADS, hd, hd)
            inter = _bmm(qc, st) * qdec_c
            kcd = kc * kdec_c
            upd = jnp.stack([_mm_tn(kcd[c], vc[c]) for c in range(sp * C_HEADS)])
            ret_out_ref[pl.ds(s0, sp)] = (cdec_c * st + upd).reshape(sp, C_HEADS, hd, hd)
            for si, rc in enumerate(rows_c):
                for h in range(C_HEADS):
                    ir_scr[rc, h * hd:(h + 1) * hd] += inter[si * C_HEADS + h]
            return c2

        lax.fori_loop(0, cps, ret_chunk, 0)
        return carry

    lax.fori_loop(0, nb // sp, ret_seq_group, 0)

    for h in range(C_HEADS):
        hs = slice(h * hd, (h + 1) * hd)
        o = ir_scr[:, hs]
        on = o * lax.rsqrt(jnp.mean(o * o, axis=-1, keepdims=True) + NORM_EPS) * rgn_ref[:, hs]
        y_scr[:, hs] = on * _silu(p_scr[:, 3 * cw + h * hd:3 * cw + (h + 1) * hd])

    strict = _block_tri(gl, ch, strict=True)
    incl = _block_tri(gl, ch)
    eye = (lax.broadcasted_iota(jnp.int32, (gl, gl), 0) == lax.broadcasted_iota(jnp.int32, (gl, gl), 1)).astype(F32)
    n_dbl = int(math.log2(ch))

    def rw_prepare(it, carry):
        def heads(ref):
            return jnp.stack([ref[pl.ds(pl.multiple_of((it * gpi + g) * gl, _SUBLANES), gl), h * nd:(h + 1) * nd]
                              for g in range(gpi) for h in range(D_HEADS)])

        at, rt, bt, kt, vv = (heads(ref) for ref in (at_scr, rt_scr, bt_scr, kt_scr, vd_scr))
        sc = _bmm_nt(jnp.concatenate([at, rt], axis=1), jnp.concatenate([bt, kt], axis=1))
        n_ab = jnp.where(strict, sc[:, 0:gl, 0:gl], 0.0)
        a_ak = jnp.where(strict, sc[:, 0:gl, gl:2 * gl], 0.0)
        r_rb = jnp.where(incl, sc[:, gl:2 * gl, 0:gl], 0.0)
        r_rk = jnp.where(incl, sc[:, gl:2 * gl, gl:2 * gl], 0.0)
        xinv = eye + n_ab
        pw = n_ab
        for _ in range(n_dbl - 1):
            pw = _bmm(pw, pw)
            xinv = xinv + _bmm(pw, xinv)
        resid = eye - (xinv - _bmm_hi(n_ab, xinv))
        xinv = xinv + _bmm(xinv, resid)
        xa = _bmm(xinv, jnp.concatenate([at, _bmm(a_ak, vv)], axis=2))
        ah = xa[:, :, 0:nd]
        u0 = xa[:, :, nd:2 * nd]
        ra = _bmm(r_rb, xa)
        rh = rt + ra[:, :, 0:nd]
        o0 = ra[:, :, nd:2 * nd] + _bmm(r_rk, vv)
        for g in range(gpi):
            part = slice(g * D_HEADS, (g + 1) * D_HEADS)
            ah_scr[it * gpi + g] = ah[part]
            u0_scr[it * gpi + g] = u0[part]
            rh_scr[it * gpi + g] = rh[part]
            o0_scr[it * gpi + g] = o0[part]
        return carry

    lax.fori_loop(0, n_groups // gpi, rw_prepare, 0)

    n_chains = sp * D_HEADS

    def rw_seq_group(sg, carry):
        s0 = sg * sp

        def rw_chunk(k, c2):
            rows_c = chain_rows(s0, k)
            where = []
            for si in range(sp):
                cidx = (s0 + si) * cps + k
                where.append((cidx // cpg, pl.ds(pl.multiple_of((cidx % cpg) * ch, _SUBLANES), ch)))
            prep = lambda ref: jnp.stack([ref[gi, h, ls, :] for gi, ls in where for h in range(D_HEADS)])
            rowsl = lambda ref: jnp.stack([ref[rc, h * nd:(h + 1) * nd] for rc in rows_c for h in range(D_HEADS)])
            st = rw_src[pl.ds(s0, sp)].reshape(n_chains, nd, nd)
            ars = _bmm_nt(jnp.concatenate([prep(ah_scr), prep(rh_scr)], axis=1), st)
            uv = jnp.concatenate([ars[:, 0:ch, :] + prep(u0_scr), rowsl(vd_scr)], axis=1)
            bk = jnp.concatenate([rowsl(bt_scr), rowsl(kt_scr)], axis=1)
            upd = jnp.stack([_mm_tn(uv[c], bk[c]) for c in range(n_chains)])
            glast = jnp.stack([cum_scr[pl.ds(pl.multiple_of((s0 + si) * tt + (k + 1) * ch - _SUBLANES, _SUBLANES),
                                             _SUBLANES), h * nd:(h + 1) * nd][_SUBLANES - 1:_SUBLANES]
                               for si in range(sp) for h in range(D_HEADS)])
            rw_out_ref[pl.ds(s0, sp)] = ((st + upd) * jnp.exp(glast)).reshape(sp, D_HEADS, nd, nd)
            o = ars[:, ch:2 * ch, :] + prep(o0_scr)
            for si, rc in enumerate(rows_c):
                for h in range(D_HEADS):
                    od_scr[rc, h * nd:(h + 1) * nd] = o[si * D_HEADS + h]
            return c2

        lax.fori_loop(0, cps, rw_chunk, 0)
        return carry

    lax.fori_loop(0, nb // sp, rw_seq_group, 0)

    o_d = od_scr[...]
    mean = _head_sums(o_d, nd) * (1.0 / nd)
    xc = o_d - mean
    var = _head_sums(xc * xc, nd) * (1.0 / nd)
    on = xc * lax.rsqrt(var + RWKV_GN_EPS) * lng_ref[...] + lnb_ref[...]
    y_scr[:, cw:cw + dw] = (on + bonus) * gate

    xo_ref[...] = (x + _mm(y_scr[...], wout_ref[...])).reshape(nb, tt, d)


def _odd_layer(x, g, win, wout, tabs, rgn, rw, ret_in, rw_in, sh_in, nb, tt, ch, gl):
    b, t, d = x.shape
    rows = nb * tt
    cos, sin, qdec, kdec, cdec, dmask = tabs
    sh3 = sh_in.reshape(b, 1, D_IN)
    vecs = [rw[k].reshape(1, -1) for k in ("mu", "w0")] + [rw["w2"]] + [rw["a0"].reshape(1, -1), rw["a2"], rw["g2"]] + \
           [rw[k].reshape(1, -1) for k in ("k_k", "k_a", "r_k", "lnx_g", "lnx_b")]
    w512 = pltpu.VMEM((rows, D_WIDTH), F32)
    outs = pl.pallas_call(
        functools.partial(_odd_kernel, nb=nb, tt=tt, ch=ch, gl=gl, one_tile=(t == tt)),
        out_shape=(jax.ShapeDtypeStruct((b, t, d), F32),
                   jax.ShapeDtypeStruct(ret_in.shape, F32),
                   jax.ShapeDtypeStruct(rw_in.shape, F32),
                   jax.ShapeDtypeStruct(sh3.shape, F32)),
        grid=(b // nb, t // tt),
        in_specs=[
            pl.BlockSpec((nb, tt, d), lambda i, j: (i, j, 0)),
            _const_spec((1, d)),
            _const_spec(win.shape),
            _const_spec(wout.shape),
            pl.BlockSpec((1, rows, C_DIM), lambda i, j: (j, 0, 0)),
            pl.BlockSpec((1, rows, C_DIM), lambda i, j: (j, 0, 0)),
            _const_spec(qdec.shape),
            _const_spec(kdec.shape),
            _const_spec(cdec.shape),
            _const_spec(dmask.shape),
            _const_spec((1, C_HEADS * C_DIM)),
        ] + [_const_spec(v.shape) for v in vecs] + [
            pl.BlockSpec((nb,) + ret_in.shape[1:], lambda i, j: (i, 0, 0, 0)),
            pl.BlockSpec((nb,) + rw_in.shape[1:], lambda i, j: (i, 0, 0, 0)),
            pl.BlockSpec((nb, 1, D_IN), lambda i, j: (i, 0, 0)),
        ],
        out_specs=(
            pl.BlockSpec((nb, tt, d), lambda i, j: (i, j, 0)),
            pl.BlockSpec((nb,) + ret_in.shape[1:], lambda i, j: (i, 0, 0, 0)),
            pl.BlockSpec((nb,) + rw_in.shape[1:], lambda i, j: (i, 0, 0, 0)),
            pl.BlockSpec((nb, 1, D_IN), lambda i, j: (i, 0, 0)),
        ),
        scratch_shapes=[
            pltpu.VMEM((rows, ODD_IN), F32),
            pltpu.VMEM((rows, C_HEADS * C_DIM + D_WIDTH), F32),
            pltpu.VMEM((nb, tt + _SUBLANES, D_IN), F32),
        ] + [w512] * 10 + [pltpu.VMEM((rows // gl, D_HEADS, gl, D_HDIM), F32)] * 4,
        compiler_params=pltpu.CompilerParams(dimension_semantics=("parallel", "arbitrary"),
                                             vmem_limit_bytes=_VMEM_LIMIT_BYTES),
        name="odd_layer",
    )(x, g.reshape(1, d), win, wout, cos, sin, qdec, kdec, cdec, dmask, rgn.reshape(1, -1), *vecs,
      ret_in, rw_in, sh3)
    xo, ret_o, rw_o, sh_o = outs
    return xo, ret_o, rw_o, sh_o.reshape(b, D_IN)


def _odd_tables(pos0, t, nb, tt, ch, gl):
    half = C_DIM // 2
    inv = ROPE_BASE ** (-jnp.arange(half, dtype=F32) / half)
    pos = pos0 + jnp.arange(t, dtype=jnp.int32)
    ang = pos.astype(F32)[:, None] * inv[None, :]
    cos = jnp.cos(ang)
    sin = jnp.sin(ang)
    cosf = jnp.concatenate([cos, cos], axis=-1).reshape(t // tt, tt, C_DIM)
    sinf = jnp.concatenate([-sin, sin], axis=-1).reshape(t // tt, tt, C_DIM)
    cosf = jnp.tile(cosf, (1, nb, 1))
    sinf = jnp.tile(sinf, (1, nb, 1))
    lg = jnp.log1p(-jnp.exp2(-5.0 - jnp.arange(C_HEADS, dtype=F32)))[:, None]
    idx = jnp.arange(ch, dtype=F32)
    q_dec = jnp.exp(lg * (idx + 1.0))
    k_dec = jnp.exp(lg * (ch - 1.0 - idx))
    rel = idx[:, None] - idx[None, :]
    dm = jnp.where(rel >= 0, jnp.exp(lg[:, :, None] * jnp.maximum(rel, 0.0)), 0.0)
    c_dec = jnp.exp(lg * ch)
    cpg = gl // ch
    expand = lambda v: jnp.repeat(jnp.tile(v.T, (cpg, 1)), C_DIM, axis=1)
    qdec = expand(q_dec)
    kdec = expand(k_dec)
    cdec = jnp.repeat(c_dec.T, C_DIM, axis=1)
    gi = jnp.arange(gl)
    same = (gi[:, None] // ch) == (gi[None, :] // ch)
    dmask = jnp.where(same[None], jnp.tile(dm, (1, cpg, cpg)), 0.0)
    return cosf, sinf, qdec, kdec, cdec, dmask


def _prep_weights(ffn1_w_gu, ffn1_w_down, ffn2_w_gu, ffn2_w_down, even_w_in, even_w_out, odd_w_in, odd_w_out,
                  xattn_wq, xattn_wkv, xattn_wo):
    bf = lambda w: w.astype(_MXU_DTYPE)
    return dict(ffn1=(bf(ffn1_w_gu), bf(ffn1_w_down)), ffn2=(bf(ffn2_w_gu), bf(ffn2_w_down)),
                even_in=bf(even_w_in), even_out=bf(even_w_out), odd_in=bf(odd_w_in), odd_out=bf(odd_w_out),
                wq=bf(xattn_wq), wkv=bf(xattn_wkv), wo=bf(xattn_wo))


def _trunk(x, pos0, conv_in, hg_in, ret_in, rw_in, sh_in, mem_k, mem_v, W, P, cfg):
    b, t, d = x.shape
    depth = P["ffn1_norm"].shape[0]
    lb_all = jnp.cumsum(jax.nn.softmax(P["hgrn_lb"].astype(F32), axis=0), axis=0)
    convs, hgs, rets, rws, shs = [], [], [], [], []
    for l in range(depth):
        jl = l // 2
        wgu, wd = W["ffn1"]
        x = _ffn(x.reshape(b * t, d), P["ffn1_norm"][l], wgu, wd, l, None, cfg["tm"], cfg["tf"]).reshape(b, t, d)
        if l % 2 == 0:
            x, cb, sh = _even_layer(x, P["mix_norm"][l], W["even_in"][jl], W["even_out"][jl], P["conv_w"][jl],
                                    lb_all[jl], P["hgrn_gnorm"][jl], conv_in[jl], hg_in[jl],
                                    cfg["enb"], cfg["ett"], cfg["blk"])
            convs.append(cb)
            hgs.append(sh)
        else:
            tabs = _odd_tables(pos0, t, cfg["nb"], cfg["tt"], cfg["ch"], cfg["gl"])
            rw = {k: P["rwkv_" + k][jl] for k in ("mu", "w0", "w2", "a0", "a2", "g2", "k_k", "k_a", "r_k",
                                                  "lnx_g", "lnx_b")}
            x, sr, sw, ss = _odd_layer(x, P["mix_norm"][l], W["odd_in"][jl], W["odd_out"][jl], tabs,
                                       P["ret_gnorm"][jl], rw, ret_in[jl], rw_in[jl], sh_in[jl],
                                       cfg["nb"], cfg["tt"], cfg["ch"], cfg["gl"])
            rets.append(sr)
            rws.append(sw)
            shs.append(ss)
        x = _xattn(x, P["xattn_norm"][l], W["wq"], W["wo"], mem_k, mem_v, l, cfg["xnb"], cfg["xtt"])
        wgu, wd = W["ffn2"]
        fin = P["final_norm"] if l == depth - 1 else None
        x = _ffn(x.reshape(b * t, d), P["ffn2_norm"][l], wgu, wd, l, fin, cfg["tm"], cfg["tf"]).reshape(b, t, d)
    return x, jnp.stack(convs), jnp.stack(hgs), jnp.stack(rets), jnp.stack(rws), jnp.stack(shs)


_ROW_TILE = 2 * _MXU_WIDTH
_ODD_ROW_TILE = _MXU_WIDTH
_CHUNK = 64
_HGRN_BLOCK = 2 * _SUBLANES
_SHORT_ROWS = 128
_XATTN_SHORT_SEQS = 8


def _configs(b, t):
    if t >= _ROW_TILE // 2:
        nb = math.gcd(b, _SEQS_PER_PASS)
        enb = math.gcd(b, 2 * _SEQS_PER_PASS)
        return dict(tm=_ROW_TILE, tf=_MXU_WIDTH, nb=nb, tt=_ODD_ROW_TILE // nb, enb=enb, ett=_ROW_TILE // enb,
                    blk=_HGRN_BLOCK, ch=_CHUNK, gl=_CHUNK, xnb=1, xtt=min(t, _ROW_TILE))
    nb = min(b, _SHORT_ROWS // t)
    return dict(tm=min(b * t, _ROW_TILE), tf=_MXU_WIDTH, nb=nb, tt=t, enb=nb, ett=t, blk=t, ch=t, gl=nb * t,
                xnb=min(b, _XATTN_SHORT_SEQS), xtt=t)


def kernel(x_prompt, x_sample, state_conv, state_hgrn, state_ret, state_rwkv, state_shift, cache_mem_k, cache_mem_v, mem_prompt, ffn1_norm, ffn1_w_gu, ffn1_w_down, mix_norm, even_w_in, even_w_out, conv_w, hgrn_lb, hgrn_gnorm, odd_w_in, odd_w_out, ret_gnorm, rwkv_mu, rwkv_w0, rwkv_w2, rwkv_a0, rwkv_a2, rwkv_g2, rwkv_k_k, rwkv_k_a, rwkv_r_k, rwkv_lnx_g, rwkv_lnx_b, xattn_norm, mem_norm, xattn_wq, xattn_wkv, xattn_wo, ffn2_norm, ffn2_w_gu, ffn2_w_down, final_norm):
    P = dict(ffn1_norm=ffn1_norm, mix_norm=mix_norm, conv_w=conv_w, hgrn_lb=hgrn_lb, hgrn_gnorm=hgrn_gnorm,
             ret_gnorm=ret_gnorm, rwkv_mu=rwkv_mu, rwkv_w0=rwkv_w0, rwkv_w2=rwkv_w2, rwkv_a0=rwkv_a0,
             rwkv_a2=rwkv_a2, rwkv_g2=rwkv_g2, rwkv_k_k=rwkv_k_k, rwkv_k_a=rwkv_k_a,
             rwkv_r_k=rwkv_r_k.reshape(rwkv_r_k.shape[0], -1), rwkv_lnx_g=rwkv_lnx_g, rwkv_lnx_b=rwkv_lnx_b,
             xattn_norm=xattn_norm, ffn2_norm=ffn2_norm, final_norm=final_norm)
    W = _prep_weights(ffn1_w_gu, ffn1_w_down, ffn2_w_gu, ffn2_w_down, even_w_in, even_w_out, odd_w_in,
                      odd_w_out, xattn_wq, xattn_wkv, xattn_wo)
    bp, tp, d = x_prompt.shape
    bs, ts, _ = x_sample.shape
    depth = ffn1_norm.shape[0]

    mem_k_p, mem_v_p, mem_k_out, mem_v_out = _mem_kv(mem_prompt, mem_norm, W["wkv"], X_HEADS)

    z = lambda ref: jnp.zeros((ref.shape[0], bp) + ref.shape[2:], F32)
    y_p, conv_p, hg_p, ret_p, rw_p, sh_p = _trunk(
        x_prompt, 0, z(state_conv), z(state_hgrn), z(state_ret), z(state_rwkv), z(state_shift),
        mem_k_p, mem_v_p, W, P, _configs(bp, tp))
    y_s, conv_s, hg_s, ret_s, rw_s, sh_s = _trunk(
        x_sample, PAST_LEN, state_conv, state_hgrn, state_ret, state_rwkv, state_shift,
        cache_mem_k, cache_mem_v, W, P, _configs(bs, ts))
    return (y_p, y_s, conv_p, hg_p, ret_p, rw_p, sh_p, mem_k_out, mem_v_out,
            conv_s, hg_s, ret_s, rw_s, sh_s)
```

```python
import functools
import math

import jax
import jax.numpy as jnp
from jax import lax
from jax.experimental import pallas as pl
from jax.experimental.pallas import tpu as pltpu

F32 = jnp.float32
_MXU_DTYPE = jnp.bfloat16

PAST_LEN = 16384
NORM_EPS = 1e-6
RWKV_GN_EPS = 64e-5
ROPE_BASE = 10000.0

_VMEM_LIMIT_BYTES = 60 * 1024 * 1024
_SUBLANES = 8
_PACKED_ROWS = 16
_MXU_WIDTH = 256
_SEQS_PER_PASS = 4

A_WIDTH = 512
B_HEADS, B_DIM = 4, 128
C_HEADS, C_DIM = 4, 128
D_HEADS, D_HDIM = 8, 64
D_WIDTH = D_HEADS * D_HDIM
EVEN_IN = 7 * 512
C_IN = 4 * 512
D_IN = 3 * 512 + 64 + 64 + 128
ODD_IN = C_IN + D_IN
X_HEADS = 4


def _mm(a, b):
    return jnp.dot(a.astype(_MXU_DTYPE), b.astype(_MXU_DTYPE), preferred_element_type=F32)


def _mm_nt(a, b):
    return lax.dot_general(a.astype(_MXU_DTYPE), b.astype(_MXU_DTYPE), (((1,), (1,)), ((), ())),
                           preferred_element_type=F32)


def _mm_tn(a, b):
    k = a.shape[0]
    if k % _PACKED_ROWS:
        pad = _PACKED_ROWS - k % _PACKED_ROWS
        a = jnp.concatenate([a, jnp.zeros((pad, a.shape[1]), a.dtype)], axis=0)
        b = jnp.concatenate([b, jnp.zeros((pad, b.shape[1]), b.dtype)], axis=0)
    return lax.dot_general(a.astype(_MXU_DTYPE), b.astype(_MXU_DTYPE), (((0,), (0,)), ((), ())),
                           preferred_element_type=F32)


def _project(h, w_ref, out_ref, col0, col1, step=_MXU_WIDTH):
    for c in range(col0, col1, step):
        e = min(c + step, col1)
        out_ref[:, c:e] = jnp.dot(h, w_ref[:, c:e], preferred_element_type=F32)


def _split2(a):
    hi = a.astype(_MXU_DTYPE)
    lo = (a - hi.astype(F32)).astype(_MXU_DTYPE)
    return hi, lo


def _mm_hi(a, b):
    ah, al = _split2(a)
    bh, bl = _split2(b)
    d = functools.partial(jnp.dot, preferred_element_type=F32)
    return d(ah, bh) + d(ah, bl) + d(al, bh)


_BATCH_NN = (((2,), (1,)), ((0,), (0,)))
_BATCH_NT = (((2,), (2,)), ((0,), (0,)))


def _bmm(a, b):
    return lax.dot_general(a.astype(_MXU_DTYPE), b.astype(_MXU_DTYPE), _BATCH_NN, preferred_element_type=F32)


def _bmm_nt(a, b):
    return lax.dot_general(a.astype(_MXU_DTYPE), b.astype(_MXU_DTYPE), _BATCH_NT, preferred_element_type=F32)


def _bmm_hi(a, b):
    ah, al = _split2(a)
    bh, bl = _split2(b)
    d = functools.partial(lax.dot_general, dimension_numbers=_BATCH_NN, preferred_element_type=F32)
    return d(ah, bh) + d(ah, bl) + d(al, bh)


def _mm_exact_lhs(m01, x):
    m = m01.astype(_MXU_DTYPE)
    x0 = x.astype(_MXU_DTYPE)
    r1 = x - x0.astype(F32)
    x1 = r1.astype(_MXU_DTYPE)
    x2 = (r1 - x1.astype(F32)).astype(_MXU_DTYPE)
    d = functools.partial(jnp.dot, preferred_element_type=F32)
    return d(m, x0) + d(m, x1) + d(m, x2)


def _mm_exact_rhs(x, m01):
    m = m01.astype(_MXU_DTYPE)
    x0 = x.astype(_MXU_DTYPE)
    r1 = x - x0.astype(F32)
    x1 = r1.astype(_MXU_DTYPE)
    x2 = (r1 - x1.astype(F32)).astype(_MXU_DTYPE)
    d = functools.partial(jnp.dot, preferred_element_type=F32)
    return d(x0, m) + d(x1, m) + d(x2, m)


def _rms(x, g):
    return x * lax.rsqrt(jnp.mean(x * x, axis=-1, keepdims=True) + NORM_EPS) * g


def _sigmoid(x):
    return 1.0 / (1.0 + jnp.exp(-x))


def _silu(x):
    return x * _sigmoid(x)


def _softplus(x):
    return jnp.maximum(x, 0.0) + jnp.log1p(jnp.exp(-jnp.abs(x)))


def _block_tri(n, blk, strict=False):
    r = lax.broadcasted_iota(jnp.int32, (n, n), 0)
    c = lax.broadcasted_iota(jnp.int32, (n, n), 1)
    same = (r // blk) == (c // blk)
    low = (c < r) if strict else (c <= r)
    return same & low


def _head_sums(x, hd):
    width = min(_MXU_WIDTH, x.shape[1])
    r = lax.broadcasted_iota(jnp.int32, (width, width), 0)
    c = lax.broadcasted_iota(jnp.int32, (width, width), 1)
    ones = ((r // hd) == (c // hd)).astype(F32)
    parts = [_mm_exact_rhs(x[:, i:i + width], ones) for i in range(0, x.shape[1], width)]
    return parts[0] if len(parts) == 1 else jnp.concatenate(parts, axis=1)


def _ffn_kernel(x_ref, g_ref, wgu_ref, wd_ref, fg_ref, o_ref, h_scr, acc_scr, *, tf, final):
    dff = wd_ref.shape[0]
    x = x_ref[...]
    h_scr[...] = _rms(x, g_ref[...]).astype(_MXU_DTYPE)
    for c in range(dff // tf):
        h = h_scr[...]
        gate = jnp.dot(h, wgu_ref[:, c * tf:(c + 1) * tf], preferred_element_type=F32)
        up = jnp.dot(h, wgu_ref[:, dff + c * tf:dff + (c + 1) * tf], preferred_element_type=F32)
        act = (_silu(gate) * up).astype(_MXU_DTYPE)
        part = jnp.dot(act, wd_ref[c * tf:(c + 1) * tf, :], preferred_element_type=F32)
        if c == 0:
            acc_scr[...] = part
        else:
            acc_scr[...] += part
    y = x + 0.5 * acc_scr[...]
    if final:
        y = _rms(y, fg_ref[...])
    o_ref[...] = y


def _const_spec(shape):
    nd = len(shape)
    return pl.BlockSpec(shape, lambda *_: (0,) * nd, pipeline_mode=pl.Buffered(1))


def _layer_spec(stacked_shape, layer):
    nd = len(stacked_shape) - 1
    return pl.BlockSpec((None,) + tuple(stacked_shape[1:]), lambda *_: (layer,) + (0,) * nd,
                        pipeline_mode=pl.Buffered(1))


def _ffn(x2d, g, wgu, wd, layer, final_g, tm, tf):
    n, d = x2d.shape
    final = final_g is not None
    fg = final_g if final else g
    return pl.pallas_call(
        functools.partial(_ffn_kernel, tf=tf, final=final),
        out_shape=jax.ShapeDtypeStruct((n, d), F32),
        grid=(n // tm,),
        in_specs=[
            pl.BlockSpec((tm, d), lambda i: (i, 0)),
            _const_spec((1, d)),
            _layer_spec(wgu.shape, layer),
            _layer_spec(wd.shape, layer),
            _const_spec((1, d)),
        ],
        out_specs=pl.BlockSpec((tm, d), lambda i: (i, 0)),
        scratch_shapes=[pltpu.VMEM((tm, d), _MXU_DTYPE), pltpu.VMEM((tm, d), F32)],
        compiler_params=pltpu.CompilerParams(dimension_semantics=("parallel",),
                                             vmem_limit_bytes=_VMEM_LIMIT_BYTES),
        name="ffn",
    )(x2d, g.reshape(1, d), wgu, wd, fg.reshape(1, d))


def _mem_kv_kernel(x_ref, g_ref, w_ref, k_ref, v_ref, ks_ref, vs_ref):
    d = x_ref.shape[-1]
    n_heads, hd = ks_ref.shape[-2:]
    kv = _mm(_rms(x_ref[...], g_ref[...]), w_ref[...])
    k_ref[...] = kv[:, :d]
    v_ref[...] = kv[:, d:]
    for h in range(n_heads):
        ks_ref[:, h, :] = kv[:, h * hd:(h + 1) * hd]
        vs_ref[:, h, :] = kv[:, d + h * hd:d + (h + 1) * hd]


def _mem_kv(mem, g, w, n_heads):
    b, n_mem, d = mem.shape
    depth = w.shape[0]
    flat_spec = pl.BlockSpec((None, None, n_mem, d), lambda l, i: (l, i, 0, 0))
    split_spec = pl.BlockSpec((None, None, n_mem, n_heads, d // n_heads), lambda l, i: (l, i, 0, 0, 0))
    flat_shape = jax.ShapeDtypeStruct((depth, b, n_mem, d), F32)
    split_shape = jax.ShapeDtypeStruct((depth, b, n_mem, n_heads, d // n_heads), F32)
    return pl.pallas_call(
        _mem_kv_kernel,
        out_shape=(flat_shape, flat_shape, split_shape, split_shape),
        grid=(depth, b),
        in_specs=[pl.BlockSpec((None, n_mem, d), lambda l, i: (i, 0, 0)),
                  pl.BlockSpec((None, 1, d), lambda l, i: (l, 0, 0)),
                  pl.BlockSpec((None, d, 2 * d), lambda l, i: (l, 0, 0))],
        out_specs=(flat_spec, flat_spec, split_spec, split_spec),
        compiler_params=pltpu.CompilerParams(dimension_semantics=("parallel", "parallel"),
                                             vmem_limit_bytes=_VMEM_LIMIT_BYTES),
        name="mem_kv",
    )(mem, g.reshape(depth, 1, d), w)


def _xattn_kernel(x_ref, g_ref, wq_ref, wo_ref, mk_ref, mv_ref, o_ref, q_scr, a_scr, *cache_scr,
                  nb, tt, n_heads, layer, head_split):
    d = x_ref.shape[-1]
    hd = d // n_heads
    rows = nb * tt
    scale = hd ** -0.5

    if head_split:
        kbuf, vbuf, sem = cache_scr
        i = pl.program_id(0)
        slot = i % 2

        def slab_copies(step, to_slot):
            seqs = pl.ds(step * nb, nb)
            return [pltpu.make_async_copy(src.at[layer, seqs, :, h, :], buf.at[to_slot, h], sem.at[to_slot, kv, h])
                    for kv, (src, buf) in enumerate(((mk_ref, kbuf), (mv_ref, vbuf))) for h in range(n_heads)]

        @pl.when(i == 0)
        def _():
            for cp in slab_copies(0, 0):
                cp.start()

        @pl.when(i + 1 < pl.num_programs(0))
        def _():
            for cp in slab_copies(i + 1, 1 - slot):
                cp.start()

        keys = lambda s, h: kbuf[slot, h, s]
        vals = lambda s, h: vbuf[slot, h, s]
    else:
        keys = lambda s, h: mk_ref[s, :, h * hd:(h + 1) * hd]
        vals = lambda s, h: mv_ref[s, :, h * hd:(h + 1) * hd]

    x = x_ref[...].reshape(rows, d)
    q_scr[...] = _mm(_rms(x, g_ref[...]), wq_ref[...])

    if head_split:
        for cp in slab_copies(i, slot):
            cp.wait()

    spp = math.gcd(nb, 2)

    def seq_body(sg, carry):
        chains = [(sg * spp + si, pl.ds(pl.multiple_of((sg * spp + si) * tt, _SUBLANES), tt),
                   h, slice(h * hd, (h + 1) * hd)) for si in range(spp) for h in range(n_heads)]
        scs = [_mm_nt(q_scr[rs, hs], keys(s, h)) * scale for s, rs, h, hs in chains]
        es = [jnp.exp(sc - jnp.max(sc, axis=-1, keepdims=True)) for sc in scs]
        prs = [e / jnp.sum(e, axis=-1, keepdims=True) for e in es]
        for pr, (s, rs, h, hs) in zip(prs, chains):
            a_scr[rs, hs] = _mm(pr, vals(s, h))
        return carry

    lax.fori_loop(0, nb // spp, seq_body, 0)
    o_ref[...] = (x + _mm(a_scr[...], wo_ref[...])).reshape(nb, tt, d)


def _xattn(x, g, wq, wo, mk, mv, layer, nb, tt):
    b, t, d = x.shape
    n_mem = mk.shape[2]
    rows = nb * tt
    head_split = mk.ndim == 5
    if head_split:
        assert t == tt and mk.shape[3] == X_HEADS
        mem_spec = pl.BlockSpec(memory_space=pl.ANY)
        slab = pltpu.VMEM((2, X_HEADS, nb, n_mem, d // X_HEADS), F32)
        cache_scr = [slab, slab, pltpu.SemaphoreType.DMA((2, 2, X_HEADS))]
        semantics = ("arbitrary", "arbitrary")
    else:
        mem_spec = pl.BlockSpec((None, nb, n_mem, d), lambda i, j: (layer, i, 0, 0))
        cache_scr = []
        semantics = ("parallel", "parallel")
    return pl.pallas_call(
        functools.partial(_xattn_kernel, nb=nb, tt=tt, n_heads=X_HEADS, layer=layer, head_split=head_split),
        out_shape=jax.ShapeDtypeStruct((b, t, d), F32),
        grid=(b // nb, t // tt),
        in_specs=[
            pl.BlockSpec((nb, tt, d), lambda i, j: (i, j, 0)),
            _const_spec((1, d)),
            _layer_spec(wq.shape, layer),
            _layer_spec(wo.shape, layer),
            mem_spec,
            mem_spec,
        ],
        out_specs=pl.BlockSpec((nb, tt, d), lambda i, j: (i, j, 0)),
        scratch_shapes=[pltpu.VMEM((rows, d), F32), pltpu.VMEM((rows, d), F32)] + cache_scr,
        compiler_params=pltpu.CompilerParams(dimension_semantics=semantics,
                                             vmem_limit_bytes=_VMEM_LIMIT_BYTES),
        name="xattn",
    )(x, g.reshape(1, d), wq, wo, mk, mv)


def _even_kernel(x_ref, g_ref, win_ref, wout_ref, cw_ref, lb_ref, gn_ref, conv_in_ref, hg_in_ref,
                 xo_ref, conv_out_ref, hg_out_ref,
                 p_scr, y_scr, ext_scr, b_scr, qg_scr, kk_scr, o_scr, st_scr, *, nb, tt, blk, one_tile):
    j = pl.program_id(1)
    nj = pl.num_programs(1)
    d = x_ref.shape[-1]
    rows = nb * tt
    aw = A_WIDTH
    hd = B_DIM

    def transpose_states(src, dst):
        if one_tile:
            for s in range(nb):
                for h in range(B_HEADS):
                    dst[s, h] = src[s, h].T
        else:
            def body(s, carry):
                for h in range(B_HEADS):
                    dst[s, h] = src[s, h].T
                return carry

            lax.fori_loop(0, nb, body, 0)

    def load_state():
        conv_out_ref[...] = conv_in_ref[...]
        transpose_states(hg_in_ref, st_scr)

    if one_tile:
        load_state()
    else:
        pl.when(j == 0)(load_state)

    x = x_ref[...].reshape(rows, d)
    h_in = _rms(x, g_ref[...]).astype(_MXU_DTYPE)
    n_parts = 2 if (nb % 2 == 0 and rows >= 2 * _MXU_WIDTH) else 1
    pnb = nb // n_parts
    prow = pnb * tt
    lb = lb_ref[...]
    cw = cw_ref[...]
    rowi = lax.broadcasted_iota(jnp.int32, (1, _SUBLANES, hd), 1)
    sp = math.gcd(pnb, _SEQS_PER_PASS * _SUBLANES // blk)
    chains = sp * B_HEADS
    col_blocks = lambda width: [(c, min(c + _MXU_WIDTH, width)) for c in range(0, width, _MXU_WIDTH)]

    def project_steps(part):
        r = slice(part * prow, (part + 1) * prow)

        def step(c, e):
            p_scr[r, c:e] = jnp.dot(h_in[r], win_ref[:, c:e], preferred_element_type=F32)

        return [functools.partial(step, c, e) for c, e in col_blocks(EVEN_IN)]

    def prepare(part):
        r = slice(part * prow, (part + 1) * prow)
        sq = slice(part * pnb, (part + 1) * pnb)
        u = p_scr[r, 2 * aw:3 * aw] * p_scr[r, 0:aw]
        ext_scr[sq, _SUBLANES:_SUBLANES + tt, :] = u.reshape(pnb, tt, aw)
        ext_scr[sq, _SUBLANES - 2:_SUBLANES, :] = conv_out_ref[sq]
        conv = (cw[0:1] * ext_scr[sq, _SUBLANES - 2:_SUBLANES - 2 + tt, :]
                + cw[1:2] * ext_scr[sq, _SUBLANES - 1:_SUBLANES - 1 + tt, :]
                + cw[2:3] * ext_scr[sq, _SUBLANES:_SUBLANES + tt, :])
        y_scr[r, 0:aw] = p_scr[r, aw:2 * aw] * conv.reshape(prow, aw)
        conv_out_ref[sq] = ext_scr[sq, _SUBLANES + tt - 2:_SUBLANES + tt, :]
        f = lb + (1.0 - lb) * _sigmoid(p_scr[r, 4 * aw:5 * aw])
        kk_scr[r, :] = 1.0 - f
        b_scr[r, :] = _mm_exact_lhs(_block_tri(prow, blk).astype(F32), jnp.log2(f))
        qg_scr[r, :] = _silu(p_scr[r, 3 * aw:4 * aw])

    def time_block(s0, k):
        rows_of = [slice((s0 + si) * tt + k * blk, (s0 + si) * tt + (k + 1) * blk) for si in range(sp)]
        stack = lambda ref, col0: jnp.stack([ref[rs, col0 + h * hd:col0 + (h + 1) * hd]
                                             for rs in rows_of for h in range(B_HEADS)])
        bb, qg, kkb = stack(b_scr, 0), stack(qg_scr, 0), stack(kk_scr, 0)
        vb = stack(p_scr, 5 * aw)
        st = st_scr[s0:s0 + sp].reshape(chains, hd, hd)
        blast = bb[:, blk - 1:blk, :]
        intra = []
        for p0 in range(0, blk, _SUBLANES):
            bi, qi = bb[:, p0:p0 + _SUBLANES, :], qg[:, p0:p0 + _SUBLANES, :]
            oi = jnp.zeros_like(bi)
            for jj in range(p0 + _SUBLANES):
                diff = bi - bb[:, jj:jj + 1, :]
                if jj >= p0:
                    diff = jnp.where(rowi >= jj - p0, diff, -jnp.inf)
                att = jnp.sum(qi * kkb[:, jj:jj + 1, :] * jnp.exp2(diff), axis=-1, keepdims=True)
                oi = oi + att * vb[:, jj:jj + 1, :]
            intra.append(oi)
        o = _bmm_nt(qg * jnp.exp2(bb), st) + (intra[0] if len(intra) == 1 else jnp.concatenate(intra, axis=1))
        kd = kkb * jnp.exp2(blast - bb)
        upd = jnp.stack([_mm_tn(vb[c], kd[c]) for c in range(chains)])
        st_scr[s0:s0 + sp] = (st * jnp.exp2(blast) + upd).reshape(sp, B_HEADS, hd, hd)
        for si, rs in enumerate(rows_of):
            for h in range(B_HEADS):
                o_scr[rs, h * hd:(h + 1) * hd] = o[si * B_HEADS + h]

    def recurrence_steps(part):
        return [functools.partial(time_block, part * pnb + sg * sp, k)
                for sg in range(pnb // sp) for k in range(tt // blk)]

    def finish(part):
        r = slice(part * prow, (part + 1) * prow)
        for h in range(B_HEADS):
            hs = slice(h * hd, (h + 1) * hd)
            o = o_scr[r, hs]
            on = o * lax.rsqrt(jnp.mean(o * o, axis=-1, keepdims=True) + NORM_EPS) * gn_ref[:, hs]
            y_scr[r, aw + h * hd:aw + (h + 1) * hd] = on * _silu(p_scr[r, 6 * aw + h * hd:6 * aw + (h + 1) * hd])

    def outproj_steps(part):
        r = slice(part * prow, (part + 1) * prow)
        sq = slice(part * pnb, (part + 1) * pnb)

        def step(c, e):
            y = y_scr[r, :].astype(_MXU_DTYPE)
            res = x[r, c:e] + jnp.dot(y, wout_ref[:, c:e], preferred_element_type=F32)
            xo_ref[sq, :, c:e] = res.reshape(pnb, tt, e - c)

        return [functools.partial(step, c, e) for c, e in col_blocks(d)]

    def run_interleaved(main, other):
        done = 0
        for i, step in enumerate(main):
            upto = (i + 1) * len(other) // len(main)
            for o_step in other[done:upto]:
                o_step()
            done = upto
            step()

    for step in project_steps(0):
        step()
    prepare(0)
    for part in range(n_parts):
        last = part == n_parts - 1
        other = [s for p in range(n_parts - 1) for s in outproj_steps(p)] if last else project_steps(part + 1)
        run_interleaved(recurrence_steps(part), other if n_parts > 1 else [])
        finish(part)
        if not last:
            prepare(part + 1)
    for step in outproj_steps(n_parts - 1):
        step()

    if one_tile:
        transpose_states(st_scr, hg_out_ref)
    else:
        pl.when(j == nj - 1)(lambda: transpose_states(st_scr, hg_out_ref))


def _even_layer(x, g, win, wout, cw, lb, gn, conv_in, hg_in, nb, tt, blk):
    b, t, d = x.shape
    rows = nb * tt
    return pl.pallas_call(
        functools.partial(_even_kernel, nb=nb, tt=tt, blk=blk, one_tile=(t == tt)),
        out_shape=(jax.ShapeDtypeStruct((b, t, d), F32),
                   jax.ShapeDtypeStruct(conv_in.shape, F32),
                   jax.ShapeDtypeStruct(hg_in.shape, F32)),
        grid=(b // nb, t // tt),
        in_specs=[
            pl.BlockSpec((nb, tt, d), lambda i, j: (i, j, 0)),
            _const_spec((1, d)),
            _const_spec(win.shape),
            _const_spec(wout.shape),
            _const_spec(cw.shape),
            _const_spec((1, A_WIDTH)),
            _const_spec((1, B_HEADS * B_DIM)),
            pl.BlockSpec((nb,) + conv_in.shape[1:], lambda i, j: (i, 0, 0)),
            pl.BlockSpec((nb,) + hg_in.shape[1:], lambda i, j: (i, 0, 0, 0)),
        ],
        out_specs=(
            pl.BlockSpec((nb, tt, d), lambda i, j: (i, j, 0)),
            pl.BlockSpec((nb,) + conv_in.shape[1:], lambda i, j: (i, 0, 0)),
            pl.BlockSpec((nb,) + hg_in.shape[1:], lambda i, j: (i, 0, 0, 0)),
        ),
        scratch_shapes=[
            pltpu.VMEM((rows, EVEN_IN), F32),
            pltpu.VMEM((rows, 2 * A_WIDTH), F32),
            pltpu.VMEM((nb, tt + _SUBLANES, A_WIDTH), F32),
            pltpu.VMEM((rows, A_WIDTH), F32),
            pltpu.VMEM((rows, A_WIDTH), F32),
            pltpu.VMEM((rows, A_WIDTH), F32),
            pltpu.VMEM((rows, A_WIDTH), F32),
            pltpu.VMEM((nb, B_HEADS, B_DIM, B_DIM), F32),
        ],
        compiler_params=pltpu.CompilerParams(dimension_semantics=("parallel", "arbitrary"),
                                             vmem_limit_bytes=_VMEM_LIMIT_BYTES),
        name="even_layer",
    )(x, g.reshape(1, d), win, wout, cw, lb.reshape(1, -1), gn.reshape(1, -1), conv_in, hg_in)


def _odd_kernel(x_ref, g_ref, win_ref, wout_ref, cos_ref, sin_ref, qdec_ref, kdec_ref, cdec_ref, dmask_ref,
                rgn_ref, mu_ref, w0_ref, w2_ref, a0_ref, a2_ref, g2_ref, kk_ref, ka_ref, rk_ref, lng_ref,
                lnb_ref, ret_in_ref, rw_in_ref, sh_in_ref,
                xo_ref, ret_out_ref, rw_out_ref, sh_out_ref,
                p_scr, y_scr, ext_scr, qr_scr, kr_scr, ir_scr,
                rt_scr, at_scr, bt_scr, kt_scr, vd_scr, cum_scr, od_scr, ah_scr, rh_scr, u0_scr, o0_scr,
                *, nb, tt, ch, gl, one_tile):
    j = pl.program_id(1)
    d = x_ref.shape[-1]
    rows = nb * tt
    cw = C_HEADS * C_DIM
    hd = C_DIM
    nd = D_HDIM
    n_groups = rows // gl
    cpg = gl // ch
    cps = tt // ch

    def load_state():
        ret_out_ref[...] = ret_in_ref[...]
        rw_out_ref[...] = rw_in_ref[...]
        sh_out_ref[...] = sh_in_ref[...]

    if one_tile and cps == 1:
        sh_out_ref[...] = sh_in_ref[...]
        ret_src, rw_src = ret_in_ref, rw_in_ref
    else:
        if one_tile:
            load_state()
        else:
            pl.when(j == 0)(load_state)
        ret_src, rw_src = ret_out_ref, rw_out_ref

    x = x_ref[...].reshape(rows, d)
    h_in = _rms(x, g_ref[...]).astype(_MXU_DTYPE)
    gpi = math.gcd(n_groups, _SEQS_PER_PASS)
    sp = math.gcd(nb, _SEQS_PER_PASS)

    def chain_rows(s0, k):
        return [pl.ds(pl.multiple_of((s0 + si) * tt + k * ch, _SUBLANES), ch) for si in range(sp)]

    _project(h_in, win_ref, p_scr, C_IN, ODD_IN)

    pd = p_scr[:, C_IN:ODD_IN]
    ext_scr[:, _SUBLANES:_SUBLANES + tt, :] = pd.reshape(nb, tt, D_IN)
    ext_scr[:, _SUBLANES - 1:_SUBLANES, :] = sh_out_ref[...]
    prev = ext_scr[:, _SUBLANES - 1:_SUBLANES - 1 + tt, :].reshape(rows, D_IN)
    sh_out_ref[...] = ext_scr[:, _SUBLANES + tt - 1:_SUBLANES + tt, :]
    pm = pd + mu_ref[...] * (prev - pd)
    dw = D_WIDTH
    r = pm[:, 0:dw]
    kd = pm[:, dw:2 * dw]
    vd = pm[:, 2 * dw:3 * dw]
    w_dn = pm[:, 3 * dw:3 * dw + 64]
    a_dn = pm[:, 3 * dw + 64:3 * dw + 128]
    g_dn = pm[:, 3 * dw + 128:3 * dw + 256]
    logdec = -math.exp(-0.5) * _sigmoid(w0_ref[...] + _mm(jnp.tanh(w_dn), w2_ref[...]))
    a = _sigmoid(a0_ref[...] + _mm(a_dn, a2_ref[...]))
    gate = _mm(_sigmoid(g_dn), g2_ref[...])
    kk0 = kd * kk_ref[...]
    kk = kk0 / jnp.maximum(jnp.sqrt(_head_sums(kk0 * kk0, nd)), 1e-12)
    kd2 = kd * (1.0 + (a - 1.0) * ka_ref[...])
    cum = _mm_exact_lhs(_block_tri(rows, ch).astype(F32), logdec)
    cum_scr[...] = cum
    rt_scr[...] = r * jnp.exp(cum)
    at_scr[...] = -kk * jnp.exp(cum - logdec)
    e_neg = jnp.exp(-cum)
    bt_scr[...] = kk * a * e_neg
    kt_scr[...] = kd2 * e_neg
    vd_scr[...] = vd
    bonus = _head_sums(r * kd2 * rk_ref[...], nd) * vd

    _project(h_in, win_ref, p_scr, 0, C_IN)
    cos = cos_ref[0]
    sin = sin_ref[0]
    for h in range(C_HEADS):
        hs = slice(h * hd, (h + 1) * hd)
        qh = p_scr[:, h * hd:(h + 1) * hd]
        kh = p_scr[:, cw + h * hd:cw + (h + 1) * hd]
        qr_scr[:, hs] = qh * cos + pltpu.roll(qh, hd // 2, axis=1) * sin
        kr_scr[:, hs] = (kh * cos + pltpu.roll(kh, hd // 2, axis=1) * sin) * (C_DIM ** -0.5)

    def ret_intra(it, carry):
        rows_g = [pl.ds(pl.multiple_of((it * gpi + g) * gl, _SUBLANES), gl) for g in range(gpi)]
        stack = lambda ref, col0: jnp.stack([ref[rg, col0 + h * hd:col0 + (h + 1) * hd]
                                             for rg in rows_g for h in range(C_HEADS)])
        dm = dmask_ref[...]
        dm = dm if gpi == 1 else jnp.concatenate([dm] * gpi, axis=0)
        o = _bmm(_bmm_nt(stack(qr_scr, 0), stack(kr_scr, 0)) * dm, stack(p_scr, 2 * cw))
        for g, rg in enumerate(rows_g):
            for h in range(C_HEADS):
                ir_scr[rg, h * hd:(h + 1) * hd] = o[g * C_HEADS + h]
        return carry

    lax.fori_loop(0, n_groups // gpi, ret_intra, 0)

    per_chain = lambda ref, r: jnp.stack([ref[r, h * hd:(h + 1) * hd] for _ in range(sp) for h in range(C_HEADS)])
    qdec_c = per_chain(qdec_ref, slice(0, ch))
    kdec_c = per_chain(kdec_ref, slice(0, ch))
    cdec_c = per_chain(cdec_ref, slice(0, 1))

    def ret_seq_group(sg, carry):
        s0 = sg * sp

        def ret_chunk(k, c2):
            rows_c = chain_rows(s0, k)
            stack = lambda ref, col0: jnp.stack([ref[rc, col0 + h * hd:col0 + (h + 1) * hd]
                                                 for rc in rows_c for h in range(C_HEADS)])
            qc, kc, vc = stack(qr_scr, 0), stack(kr_scr, 0), stack(p_scr, 2 * cw)
            st = ret_src[pl.ds(s0, sp)].reshape(sp * C_HEADS, hd, hd)
            inter = _bmm(qc, st) * qdec_c
            kcd = kc * kdec_c
            upd = jnp.stack([_mm_tn(kcd[c], vc[c]) for c in range(sp * C_HEADS)])
            ret_out_ref[pl.ds(s0, sp)] = (cdec_c * st + upd).reshape(sp, C_HEADS, hd, hd)
            for si, rc in enumerate(rows_c):
                for h in range(C_HEADS):
                    ir_scr[rc, h * hd:(h + 1) * hd] += inter[si * C_HEADS + h]
            return c2

        lax.fori_loop(0, cps, ret_chunk, 0)
        return carry

    lax.fori_loop(0, nb // sp, ret_seq_group, 0)

    for h in range(C_HEADS):
        hs = slice(h * hd, (h + 1) * hd)
        o = ir_scr[:, hs]
        on = o * lax.rsqrt(jnp.mean(o * o, axis=-1, keepdims=True) + NORM_EPS) * rgn_ref[:, hs]
        y_scr[:, hs] = on * _silu(p_scr[:, 3 * cw + h * hd:3 * cw + (h + 1) * hd])

    strict = _block_tri(gl, ch, strict=True)
    incl = _block_tri(gl, ch)
    eye = (lax.broadcasted_iota(jnp.int32, (gl, gl), 0) == lax.broadcasted_iota(jnp.int32, (gl, gl), 1)).astype(F32)
    n_dbl = int(math.log2(ch))

    def rw_prepare(it, carry):
        def heads(ref):
            return jnp.stack([ref[pl.ds(pl.multiple_of((it * gpi + g) * gl, _SUBLANES), gl), h * nd:(h + 1) * nd]
                              for g in range(gpi) for h in range(D_HEADS)])

        at, rt, bt, kt, vv = (heads(ref) for ref in (at_scr, rt_scr, bt_scr, kt_scr, vd_scr))
        sc = _bmm_nt(jnp.concatenate([at, rt], axis=1), jnp.concatenate([bt, kt], axis=1))
        n_ab = jnp.where(strict, sc[:, 0:gl, 0:gl], 0.0)
        a_ak = jnp.where(strict, sc[:, 0:gl, gl:2 * gl], 0.0)
        r_rb = jnp.where(incl, sc[:, gl:2 * gl, 0:gl], 0.0)
        r_rk = jnp.where(incl, sc[:, gl:2 * gl, gl:2 * gl], 0.0)
        xinv = eye + n_ab
        pw = n_ab
        for _ in range(n_dbl - 1):
            pw = _bmm(pw, pw)
            xinv = xinv + _bmm(pw, xinv)
        resid = eye - (xinv - _bmm_hi(n_ab, xinv))
        xinv = xinv + _bmm(xinv, resid)
        xa = _bmm(xinv, jnp.concatenate([at, _bmm(a_ak, vv)], axis=2))
        ah = xa[:, :, 0:nd]
        u0 = xa[:, :, nd:2 * nd]
        ra = _bmm(r_rb, xa)
        rh = rt + ra[:, :, 0:nd]
        o0 = ra[:, :, nd:2 * nd] + _bmm(r_rk, vv)
        for g in range(gpi):
            part = slice(g * D_HEADS, (g + 1) * D_HEADS)
            ah_scr[it * gpi + g] = ah[part]
            u0_scr[it * gpi + g] = u0[part]
            rh_scr[it * gpi + g] = rh[part]
            o0_scr[it * gpi + g] = o0[part]
        return carry

    lax.fori_loop(0, n_groups // gpi, rw_prepare, 0)

    n_chains = sp * D_HEADS

    def rw_seq_group(sg, carry):
        s0 = sg * sp

        def rw_chunk(k, c2):
            rows_c = chain_rows(s0, k)
            where = []
            for si in range(sp):
                cidx = (s0 + si) * cps + k
                where.append((cidx // cpg, pl.ds(pl.multiple_of((cidx % cpg) * ch, _SUBLANES), ch)))
            prep = lambda ref: jnp.stack([ref[gi, h, ls, :] for gi, ls in where for h in range(D_HEADS)])
            rowsl = lambda ref: jnp.stack([ref[rc, h * nd:(h + 1) * nd] for rc in rows_c for h in range(D_HEADS)])
            st = rw_src[pl.ds(s0, sp)].reshape(n_chains, nd, nd)
            ars = _bmm_nt(jnp.concatenate([prep(ah_scr), prep(rh_scr)], axis=1), st)
            uv = jnp.concatenate([ars[:, 0:ch, :] + prep(u0_scr), rowsl(vd_scr)], axis=1)
            bk = jnp.concatenate([rowsl(bt_scr), rowsl(kt_scr)], axis=1)
            upd = jnp.stack([_mm_tn(uv[c], bk[c]) for c in range(n_chains)])
            glast = jnp.stack([cum_scr[pl.ds(pl.multiple_of((s0 + si) * tt + (k + 1) * ch - _SUBLANES, _SUBLANES),
                                             _SUBLANES), h * nd:(h + 1) * nd][_SUBLANES - 1:_SUBLANES]
                               for si in range(sp) for h in range(D_HEADS)])
            rw_out_ref[pl.ds(s0, sp)] = ((st + upd) * jnp.exp(glast)).reshape(sp, D_HEADS, nd, nd)
            o = ars[:, ch:2 * ch, :] + prep(o0_scr)
            for si, rc in enumerate(rows_c):
                for h in range(D_HEADS):
                    od_scr[rc, h * nd:(h + 1) * nd] = o[si * D_HEADS + h]
            return c2

        lax.fori_loop(0, cps, rw_chunk, 0)
        return carry

    lax.fori_loop(0, nb // sp, rw_seq_group, 0)

    o_d = od_scr[...]
    mean = _head_sums(o_d, nd) * (1.0 / nd)
    xc = o_d - mean
    var = _head_sums(xc * xc, nd) * (1.0 / nd)
    on = xc * lax.rsqrt(var + RWKV_GN_EPS) * lng_ref[...] + lnb_ref[...]
    y_scr[:, cw:cw + dw] = (on + bonus) * gate

    xo_ref[...] = (x + _mm(y_scr[...], wout_ref[...])).reshape(nb, tt, d)


def _odd_layer(x, g, win, wout, tabs, rgn, rw, ret_in, rw_in, sh_in, nb, tt, ch, gl):
    b, t, d = x.shape
    rows = nb * tt
    cos, sin, qdec, kdec, cdec, dmask = tabs
    sh3 = sh_in.reshape(b, 1, D_IN)
    vecs = [rw[k].reshape(1, -1) for k in ("mu", "w0")] + [rw["w2"]] + [rw["a0"].reshape(1, -1), rw["a2"], rw["g2"]] + \
           [rw[k].reshape(1, -1) for k in ("k_k", "k_a", "r_k", "lnx_g", "lnx_b")]
    w512 = pltpu.VMEM((rows, D_WIDTH), F32)
    outs = pl.pallas_call(
        functools.partial(_odd_kernel, nb=nb, tt=tt, ch=ch, gl=gl, one_tile=(t == tt)),
        out_shape=(jax.ShapeDtypeStruct((b, t, d), F32),
                   jax.ShapeDtypeStruct(ret_in.shape, F32),
                   jax.ShapeDtypeStruct(rw_in.shape, F32),
                   jax.ShapeDtypeStruct(sh3.shape, F32)),
        grid=(b // nb, t // tt),
        in_specs=[
            pl.BlockSpec((nb, tt, d), lambda i, j: (i, j, 0)),
            _const_spec((1, d)),
            _const_spec(win.shape),
            _const_spec(wout.shape),
            pl.BlockSpec((1, rows, C_DIM), lambda i, j: (j, 0, 0)),
            pl.BlockSpec((1, rows, C_DIM), lambda i, j: (j, 0, 0)),
            _const_spec(qdec.shape),
            _const_spec(kdec.shape),
            _const_spec(cdec.shape),
            _const_spec(dmask.shape),
            _const_spec((1, C_HEADS * C_DIM)),
        ] + [_const_spec(v.shape) for v in vecs] + [
            pl.BlockSpec((nb,) + ret_in.shape[1:], lambda i, j: (i, 0, 0, 0)),
            pl.BlockSpec((nb,) + rw_in.shape[1:], lambda i, j: (i, 0, 0, 0)),
            pl.BlockSpec((nb, 1, D_IN), lambda i, j: (i, 0, 0)),
        ],
        out_specs=(
            pl.BlockSpec((nb, tt, d), lambda i, j: (i, j, 0)),
            pl.BlockSpec((nb,) + ret_in.shape[1:], lambda i, j: (i, 0, 0, 0)),
            pl.BlockSpec((nb,) + rw_in.shape[1:], lambda i, j: (i, 0, 0, 0)),
            pl.BlockSpec((nb, 1, D_IN), lambda i, j: (i, 0, 0)),
        ),
        scratch_shapes=[
            pltpu.VMEM((rows, ODD_IN), F32),
            pltpu.VMEM((rows, C_HEADS * C_DIM + D_WIDTH), F32),
            pltpu.VMEM((nb, tt + _SUBLANES, D_IN), F32),
        ] + [w512] * 10 + [pltpu.VMEM((rows // gl, D_HEADS, gl, D_HDIM), F32)] * 4,
        compiler_params=pltpu.CompilerParams(dimension_semantics=("parallel", "arbitrary"),
                                             vmem_limit_bytes=_VMEM_LIMIT_BYTES),
        name="odd_layer",
    )(x, g.reshape(1, d), win, wout, cos, sin, qdec, kdec, cdec, dmask, rgn.reshape(1, -1), *vecs,
      ret_in, rw_in, sh3)
    xo, ret_o, rw_o, sh_o = outs
    return xo, ret_o, rw_o, sh_o.reshape(b, D_IN)


def _odd_tables(pos0, t, nb, tt, ch, gl):
    half = C_DIM // 2
    inv = ROPE_BASE ** (-jnp.arange(half, dtype=F32) / half)
    pos = pos0 + jnp.arange(t, dtype=jnp.int32)
    ang = pos.astype(F32)[:, None] * inv[None, :]
    cos = jnp.cos(ang)
    sin = jnp.sin(ang)
    cosf = jnp.concatenate([cos, cos], axis=-1).reshape(t // tt, tt, C_DIM)
    sinf = jnp.concatenate([-sin, sin], axis=-1).reshape(t // tt, tt, C_DIM)
    cosf = jnp.tile(cosf, (1, nb, 1))
    sinf = jnp.tile(sinf, (1, nb, 1))
    lg = jnp.log1p(-jnp.exp2(-5.0 - jnp.arange(C_HEADS, dtype=F32)))[:, None]
    idx = jnp.arange(ch, dtype=F32)
    q_dec = jnp.exp(lg * (idx + 1.0))
    k_dec = jnp.exp(lg * (ch - 1.0 - idx))
    rel = idx[:, None] - idx[None, :]
    dm = jnp.where(rel >= 0, jnp.exp(lg[:, :, None] * jnp.maximum(rel, 0.0)), 0.0)
    c_dec = jnp.exp(lg * ch)
    cpg = gl // ch
    expand = lambda v: jnp.repeat(jnp.tile(v.T, (cpg, 1)), C_DIM, axis=1)
    qdec = expand(q_dec)
    kdec = expand(k_dec)
    cdec = jnp.repeat(c_dec.T, C_DIM, axis=1)
    gi = jnp.arange(gl)
    same = (gi[:, None] // ch) == (gi[None, :] // ch)
    dmask = jnp.where(same[None], jnp.tile(dm, (1, cpg, cpg)), 0.0)
    return cosf, sinf, qdec, kdec, cdec, dmask


def _prep_weights(ffn1_w_gu, ffn1_w_down, ffn2_w_gu, ffn2_w_down, even_w_in, even_w_out, odd_w_in, odd_w_out,
                  xattn_wq, xattn_wkv, xattn_wo):
    bf = lambda w: w.astype(_MXU_DTYPE)
    return dict(ffn1=(bf(ffn1_w_gu), bf(ffn1_w_down)), ffn2=(bf(ffn2_w_gu), bf(ffn2_w_down)),
                even_in=bf(even_w_in), even_out=bf(even_w_out), odd_in=bf(odd_w_in), odd_out=bf(odd_w_out),
                wq=bf(xattn_wq), wkv=bf(xattn_wkv), wo=bf(xattn_wo))


def _trunk(x, pos0, conv_in, hg_in, ret_in, rw_in, sh_in, mem_k, mem_v, W, P, cfg):
    b, t, d = x.shape
    depth = P["ffn1_norm"].shape[0]
    lb_all = jnp.cumsum(jax.nn.softmax(P["hgrn_lb"].astype(F32), axis=0), axis=0)
    convs, hgs, rets, rws, shs = [], [], [], [], []
    for l in range(depth):
        jl = l // 2
        wgu, wd = W["ffn1"]
        x = _ffn(x.reshape(b * t, d), P["ffn1_norm"][l], wgu, wd, l, None, cfg["tm"], cfg["tf"]).reshape(b, t, d)
        if l % 2 == 0:
            x, cb, sh = _even_layer(x, P["mix_norm"][l], W["even_in"][jl], W["even_out"][jl], P["conv_w"][jl],
                                    lb_all[jl], P["hgrn_gnorm"][jl], conv_in[jl], hg_in[jl],
                                    cfg["enb"], cfg["ett"], cfg["blk"])
            convs.append(cb)
            hgs.append(sh)
        else:
            tabs = _odd_tables(pos0, t, cfg["nb"], cfg["tt"], cfg["ch"], cfg["gl"])
            rw = {k: P["rwkv_" + k][jl] for k in ("mu", "w0", "w2", "a0", "a2", "g2", "k_k", "k_a", "r_k",
                                                  "lnx_g", "lnx_b")}
            x, sr, sw, ss = _odd_layer(x, P["mix_norm"][l], W["odd_in"][jl], W["odd_out"][jl], tabs,
                                       P["ret_gnorm"][jl], rw, ret_in[jl], rw_in[jl], sh_in[jl],
                                       cfg["nb"], cfg["tt"], cfg["ch"], cfg["gl"])
            rets.append(sr)
            rws.append(sw)
            shs.append(ss)
        x = _xattn(x, P["xattn_norm"][l], W["wq"], W["wo"], mem_k, mem_v, l, cfg["xnb"], cfg["xtt"])
        wgu, wd = W["ffn2"]
        fin = P["final_norm"] if l == depth - 1 else None
        x = _ffn(x.reshape(b * t, d), P["ffn2_norm"][l], wgu, wd, l, fin, cfg["tm"], cfg["tf"]).reshape(b, t, d)
    return x, jnp.stack(convs), jnp.stack(hgs), jnp.stack(rets), jnp.stack(rws), jnp.stack(shs)


_ROW_TILE = 2 * _MXU_WIDTH
_ODD_ROW_TILE = _MXU_WIDTH
_CHUNK = 64
_HGRN_BLOCK = 2 * _SUBLANES
_SHORT_ROWS = 128
_XATTN_SHORT_SEQS = 8


def _configs(b, t):
    if t >= _ROW_TILE // 2:
        nb = math.gcd(b, _SEQS_PER_PASS)
        enb = math.gcd(b, 2 * _SEQS_PER_PASS)
        return dict(tm=_ROW_TILE, tf=_MXU_WIDTH, nb=nb, tt=_ODD_ROW_TILE // nb, enb=enb, ett=_ROW_TILE // enb,
                    blk=_HGRN_BLOCK, ch=_CHUNK, gl=_CHUNK, xnb=1, xtt=min(t, _ROW_TILE))
    nb = min(b, _SHORT_ROWS // t)
    return dict(tm=min(b * t, _ROW_TILE), tf=_MXU_WIDTH, nb=nb, tt=t, enb=nb, ett=t, blk=t, ch=t, gl=nb * t,
                xnb=min(b, _XATTN_SHORT_SEQS), xtt=t)


def kernel(x_prompt, x_sample, state_conv, state_hgrn, state_ret, state_rwkv, state_shift, cache_mem_k, cache_mem_v, mem_prompt, ffn1_norm, ffn1_w_gu, ffn1_w_down, mix_norm, even_w_in, even_w_out, conv_w, hgrn_lb, hgrn_gnorm, odd_w_in, odd_w_out, ret_gnorm, rwkv_mu, rwkv_w0, rwkv_w2, rwkv_a0, rwkv_a2, rwkv_g2, rwkv_k_k, rwkv_k_a, rwkv_r_k, rwkv_lnx_g, rwkv_lnx_b, xattn_norm, mem_norm, xattn_wq, xattn_wkv, xattn_wo, ffn2_norm, ffn2_w_gu, ffn2_w_down, final_norm):
    P = dict(ffn1_norm=ffn1_norm, mix_norm=mix_norm, conv_w=conv_w, hgrn_lb=hgrn_lb, hgrn_gnorm=hgrn_gnorm,
             ret_gnorm=ret_gnorm, rwkv_mu=rwkv_mu, rwkv_w0=rwkv_w0, rwkv_w2=rwkv_w2, rwkv_a0=rwkv_a0,
             rwkv_a2=rwkv_a2, rwkv_g2=rwkv_g2, rwkv_k_k=rwkv_k_k, rwkv_k_a=rwkv_k_a,
             rwkv_r_k=rwkv_r_k.reshape(rwkv_r_k.shape[0], -1), rwkv_lnx_g=rwkv_lnx_g, rwkv_lnx_b=rwkv_lnx_b,
             xattn_norm=xattn_norm, ffn2_norm=ffn2_norm, final_norm=final_norm)
    W = _prep_weights(ffn1_w_gu, ffn1_w_down, ffn2_w_gu, ffn2_w_down, even_w_in, even_w_out, odd_w_in,
                      odd_w_out, xattn_wq, xattn_wkv, xattn_wo)
    bp, tp, d = x_prompt.shape
    bs, ts, _ = x_sample.shape
    depth = ffn1_norm.shape[0]

    mem_k_p, mem_v_p, mem_k_out, mem_v_out = _mem_kv(mem_prompt, mem_norm, W["wkv"], X_HEADS)

    z = lambda ref: jnp.zeros((ref.shape[0], bp) + ref.shape[2:], F32)
    y_p, conv_p, hg_p, ret_p, rw_p, sh_p = _trunk(
        x_prompt, 0, z(state_conv), z(state_hgrn), z(state_ret), z(state_rwkv), z(state_shift),
        mem_k_p, mem_v_p, W, P, _configs(bp, tp))
    y_s, conv_s, hg_s, ret_s, rw_s, sh_s = _trunk(
        x_sample, PAST_LEN, state_conv, state_hgrn, state_ret, state_rwkv, state_shift,
        cache_mem_k, cache_mem_v, W, P, _configs(bs, ts))
    return (y_p, y_s, conv_p, hg_p, ret_p, rw_p, sh_p, mem_k_out, mem_v_out,
            conv_s, hg_s, ret_s, rw_s, sh_s)
```

```python
import functools
import math

import jax
import jax.numpy as jnp
from jax import lax
from jax.experimental import pallas as pl
from jax.experimental.pallas import tpu as pltpu

F32 = jnp.float32
_MXU_DTYPE = jnp.bfloat16

PAST_LEN = 16384
NORM_EPS = 1e-6
RWKV_GN_EPS = 64e-5
ROPE_BASE = 10000.0

_VMEM_LIMIT_BYTES = 60 * 1024 * 1024
_SUBLANES = 8
_PACKED_ROWS = 16
_MXU_WIDTH = 256
_SEQS_PER_PASS = 4

A_WIDTH = 512
B_HEADS, B_DIM = 4, 128
C_HEADS, C_DIM = 4, 128
D_HEADS, D_HDIM = 8, 64
D_WIDTH = D_HEADS * D_HDIM
EVEN_IN = 7 * 512
C_IN = 4 * 512
D_IN = 3 * 512 + 64 + 64 + 128
ODD_IN = C_IN + D_IN
X_HEADS = 4


def _mm(a, b):
    return jnp.dot(a.astype(_MXU_DTYPE), b.astype(_MXU_DTYPE), preferred_element_type=F32)


def _mm_nt(a, b):
    return lax.dot_general(a.astype(_MXU_DTYPE), b.astype(_MXU_DTYPE), (((1,), (1,)), ((), ())),
                           preferred_element_type=F32)


def _mm_tn(a, b):
    k = a.shape[0]
    if k % _PACKED_ROWS:
        pad = _PACKED_ROWS - k % _PACKED_ROWS
        a = jnp.concatenate([a, jnp.zeros((pad, a.shape[1]), a.dtype)], axis=0)
        b = jnp.concatenate([b, jnp.zeros((pad, b.shape[1]), b.dtype)], axis=0)
    return lax.dot_general(a.astype(_MXU_DTYPE), b.astype(_MXU_DTYPE), (((0,), (0,)), ((), ())),
                           preferred_element_type=F32)


def _project(h, w_ref, out_ref, col0, col1, step=_MXU_WIDTH):
    for c in range(col0, col1, step):
        e = min(c + step, col1)
        out_ref[:, c:e] = jnp.dot(h, w_ref[:, c:e], preferred_element_type=F32)


def _split2(a):
    hi = a.astype(_MXU_DTYPE)
    lo = (a - hi.astype(F32)).astype(_MXU_DTYPE)
    return hi, lo


def _mm_hi(a, b):
    ah, al = _split2(a)
    bh, bl = _split2(b)
    d = functools.partial(jnp.dot, preferred_element_type=F32)
    return d(ah, bh) + d(ah, bl) + d(al, bh)


_BATCH_NN = (((2,), (1,)), ((0,), (0,)))
_BATCH_NT = (((2,), (2,)), ((0,), (0,)))


def _bmm(a, b):
    return lax.dot_general(a.astype(_MXU_DTYPE), b.astype(_MXU_DTYPE), _BATCH_NN, preferred_element_type=F32)


def _bmm_nt(a, b):
    return lax.dot_general(a.astype(_MXU_DTYPE), b.astype(_MXU_DTYPE), _BATCH_NT, preferred_element_type=F32)


def _bmm_hi(a, b):
    ah, al = _split2(a)
    bh, bl = _split2(b)
    d = functools.partial(lax.dot_general, dimension_numbers=_BATCH_NN, preferred_element_type=F32)
    return d(ah, bh) + d(ah, bl) + d(al, bh)


def _mm_exact_lhs(m01, x):
    m = m01.astype(_MXU_DTYPE)
    x0 = x.astype(_MXU_DTYPE)
    r1 = x - x0.astype(F32)
    x1 = r1.astype(_MXU_DTYPE)
    x2 = (r1 - x1.astype(F32)).astype(_MXU_DTYPE)
    d = functools.partial(jnp.dot, preferred_element_type=F32)
    return d(m, x0) + d(m, x1) + d(m, x2)


def _mm_exact_rhs(x, m01):
    m = m01.astype(_MXU_DTYPE)
    x0 = x.astype(_MXU_DTYPE)
    r1 = x - x0.astype(F32)
    x1 = r1.astype(_MXU_DTYPE)
    x2 = (r1 - x1.astype(F32)).astype(_MXU_DTYPE)
    d = functools.partial(jnp.dot, preferred_element_type=F32)
    return d(x0, m) + d(x1, m) + d(x2, m)


def _rms(x, g):
    return x * lax.rsqrt(jnp.mean(x * x, axis=-1, keepdims=True) + NORM_EPS) * g


def _sigmoid(x):
    return 1.0 / (1.0 + jnp.exp(-x))


def _silu(x):
    return x * _sigmoid(x)


def _softplus(x):
    return jnp.maximum(x, 0.0) + jnp.log1p(jnp.exp(-jnp.abs(x)))


def _block_tri(n, blk, strict=False):
    r = lax.broadcasted_iota(jnp.int32, (n, n), 0)
    c = lax.broadcasted_iota(jnp.int32, (n, n), 1)
    same = (r // blk) == (c // blk)
    low = (c < r) if strict else (c <= r)
    return same & low


def _head_sums(x, hd):
    width = min(_MXU_WIDTH, x.shape[1])
    r = lax.broadcasted_iota(jnp.int32, (width, width), 0)
    c = lax.broadcasted_iota(jnp.int32, (width, width), 1)
    ones = ((r // hd) == (c // hd)).astype(F32)
    parts = [_mm_exact_rhs(x[:, i:i + width], ones) for i in range(0, x.shape[1], width)]
    return parts[0] if len(parts) == 1 else jnp.concatenate(parts, axis=1)


def _ffn_kernel(xa_ref, xb_ref, g_ref, wgu_ref, wd_ref, fg_ref, oa_ref, ob_ref, h_scr, acc_scr, *, tf, final, steps_a):
    dff = wd_ref.shape[0]
    in_a = pl.program_id(0) < steps_a
    x = jnp.where(in_a, xa_ref[...], xb_ref[...])
    h_scr[...] = _rms(x, g_ref[...]).astype(_MXU_DTYPE)
    for c in range(dff // tf):
        h = h_scr[...]
        gate = jnp.dot(h, wgu_ref[:, c * tf:(c + 1) * tf], preferred_element_type=F32)
        up = jnp.dot(h, wgu_ref[:, dff + c * tf:dff + (c + 1) * tf], preferred_element_type=F32)
        act = (_silu(gate) * up).astype(_MXU_DTYPE)
        part = jnp.dot(act, wd_ref[c * tf:(c + 1) * tf, :], preferred_element_type=F32)
        if c == 0:
            acc_scr[...] = part
        else:
            acc_scr[...] += part
    y = x + 0.5 * acc_scr[...]
    if final:
        y = _rms(y, fg_ref[...])

    @pl.when(in_a)
    def _():
        oa_ref[...] = y

    @pl.when(jnp.logical_not(in_a))
    def _():
        ob_ref[...] = y


def _const_spec(shape):
    nd = len(shape)
    return pl.BlockSpec(shape, lambda *_: (0,) * nd, pipeline_mode=pl.Buffered(1))


def _layer_spec(stacked_shape, layer):
    nd = len(stacked_shape) - 1
    return pl.BlockSpec((None,) + tuple(stacked_shape[1:]), lambda *_: (layer,) + (0,) * nd,
                        pipeline_mode=pl.Buffered(1))


def _ffn(xa, xb, g, wgu, wd, layer, final_g, tm, tf):
    (na, d), nb_rows = xa.shape, xb.shape[0]
    assert na % tm == 0 and nb_rows % tm == 0
    steps_a, steps_b = na // tm, nb_rows // tm
    final = final_g is not None
    fg = final_g if final else g
    spec_a = pl.BlockSpec((tm, d), lambda i: (jnp.minimum(i, steps_a - 1), 0))
    spec_b = pl.BlockSpec((tm, d), lambda i: (jnp.maximum(i - steps_a, 0), 0))
    return pl.pallas_call(
        functools.partial(_ffn_kernel, tf=tf, final=final, steps_a=steps_a),
        out_shape=(jax.ShapeDtypeStruct((na, d), F32), jax.ShapeDtypeStruct((nb_rows, d), F32)),
        grid=(steps_a + steps_b,),
        in_specs=[
            spec_a,
            spec_b,
            _const_spec((1, d)),
            _layer_spec(wgu.shape, layer),
            _layer_spec(wd.shape, layer),
            _const_spec((1, d)),
        ],
        out_specs=(spec_a, spec_b),
        scratch_shapes=[pltpu.VMEM((tm, d), _MXU_DTYPE), pltpu.VMEM((tm, d), F32)],
        compiler_params=pltpu.CompilerParams(dimension_semantics=("arbitrary",),
                                             vmem_limit_bytes=_VMEM_LIMIT_BYTES),
        name="ffn",
    )(xa, xb, g.reshape(1, d), wgu, wd, fg.reshape(1, d))


def _mem_kv_kernel(x_ref, g_ref, w_ref, k_ref, v_ref, ks_ref, vs_ref):
    d = x_ref.shape[-1]
    n_heads, hd = ks_ref.shape[-2:]
    kv = _mm(_rms(x_ref[...], g_ref[...]), w_ref[...])
    k_ref[...] = kv[:, :d]
    v_ref[...] = kv[:, d:]
    for h in range(n_heads):
        ks_ref[:, h, :] = kv[:, h * hd:(h + 1) * hd]
        vs_ref[:, h, :] = kv[:, d + h * hd:d + (h + 1) * hd]


def _mem_kv(mem, g, w, n_heads):
    b, n_mem, d = mem.shape
    depth = w.shape[0]
    flat_spec = pl.BlockSpec((None, None, n_mem, d), lambda l, i: (l, i, 0, 0))
    split_spec = pl.BlockSpec((None, None, n_mem, n_heads, d // n_heads), lambda l, i: (l, i, 0, 0, 0))
    flat_shape = jax.ShapeDtypeStruct((depth, b, n_mem, d), F32)
    split_shape = jax.ShapeDtypeStruct((depth, b, n_mem, n_heads, d // n_heads), F32)
    return pl.pallas_call(
        _mem_kv_kernel,
        out_shape=(flat_shape, flat_shape, split_shape, split_shape),
        grid=(depth, b),
        in_specs=[pl.BlockSpec((None, n_mem, d), lambda l, i: (i, 0, 0)),
                  pl.BlockSpec((None, 1, d), lambda l, i: (l, 0, 0)),
                  pl.BlockSpec((None, d, 2 * d), lambda l, i: (l, 0, 0))],
        out_specs=(flat_spec, flat_spec, split_spec, split_spec),
        compiler_params=pltpu.CompilerParams(dimension_semantics=("parallel", "parallel"),
                                             vmem_limit_bytes=_VMEM_LIMIT_BYTES),
        name="mem_kv",
    )(mem, g.reshape(depth, 1, d), w)


def _xattn_kernel(x_ref, g_ref, wq_ref, wo_ref, mk_ref, mv_ref, o_ref, q_scr, a_scr, *cache_scr,
                  nb, tt, n_heads, layer, head_split):
    d = x_ref.shape[-1]
    hd = d // n_heads
    rows = nb * tt
    scale = hd ** -0.5

    if head_split:
        kbuf, vbuf, sem = cache_scr
        i = pl.program_id(0)
        slot = i % 2

        def slab_copies(step, to_slot):
            seqs = pl.ds(step * nb, nb)
            return [pltpu.make_async_copy(src.at[layer, seqs, :, h, :], buf.at[to_slot, h], sem.at[to_slot, kv, h])
                    for kv, (src, buf) in enumerate(((mk_ref, kbuf), (mv_ref, vbuf))) for h in range(n_heads)]

        @pl.when(i == 0)
        def _():
            for cp in slab_copies(0, 0):
                cp.start()

        @pl.when(i + 1 < pl.num_programs(0))
        def _():
            for cp in slab_copies(i + 1, 1 - slot):
                cp.start()

        keys = lambda s, h: kbuf[slot, h, s]
        vals = lambda s, h: vbuf[slot, h, s]
    else:
        keys = lambda s, h: mk_ref[s, :, h * hd:(h + 1) * hd]
        vals = lambda s, h: mv_ref[s, :, h * hd:(h + 1) * hd]

    x = x_ref[...].reshape(rows, d)
    q_scr[...] = _mm(_rms(x, g_ref[...]), wq_ref[...])

    if head_split:
        for cp in slab_copies(i, slot):
            cp.wait()

    spp = math.gcd(nb, 2)

    def seq_body(sg, carry):
        chains = [(sg * spp + si, pl.ds(pl.multiple_of((sg * spp + si) * tt, _SUBLANES), tt),
                   h, slice(h * hd, (h + 1) * hd)) for si in range(spp) for h in range(n_heads)]
        scs = [_mm_nt(q_scr[rs, hs], keys(s, h)) * scale for s, rs, h, hs in chains]
        es = [jnp.exp(sc - jnp.max(sc, axis=-1, keepdims=True)) for sc in scs]
        prs = [e / jnp.sum(e, axis=-1, keepdims=True) for e in es]
        for pr, (s, rs, h, hs) in zip(prs, chains):
            a_scr[rs, hs] = _mm(pr, vals(s, h))
        return carry

    lax.fori_loop(0, nb // spp, seq_body, 0)
    o_ref[...] = (x + _mm(a_scr[...], wo_ref[...])).reshape(nb, tt, d)


def _xattn(x, g, wq, wo, mk, mv, layer, nb, tt):
    b, t, d = x.shape
    n_mem = mk.shape[2]
    rows = nb * tt
    head_split = mk.ndim == 5
    if head_split:
        assert t == tt and mk.shape[3] == X_HEADS
        mem_spec = pl.BlockSpec(memory_space=pl.ANY)
        slab = pltpu.VMEM((2, X_HEADS, nb, n_mem, d // X_HEADS), F32)
        cache_scr = [slab, slab, pltpu.SemaphoreType.DMA((2, 2, X_HEADS))]
        semantics = ("arbitrary", "arbitrary")
    else:
        mem_spec = pl.BlockSpec((None, nb, n_mem, d), lambda i, j: (layer, i, 0, 0))
        cache_scr = []
        semantics = ("parallel", "parallel")
    return pl.pallas_call(
        functools.partial(_xattn_kernel, nb=nb, tt=tt, n_heads=X_HEADS, layer=layer, head_split=head_split),
        out_shape=jax.ShapeDtypeStruct((b, t, d), F32),
        grid=(b // nb, t // tt),
        in_specs=[
            pl.BlockSpec((nb, tt, d), lambda i, j: (i, j, 0)),
            _const_spec((1, d)),
            _layer_spec(wq.shape, layer),
            _layer_spec(wo.shape, layer),
            mem_spec,
            mem_spec,
        ],
        out_specs=pl.BlockSpec((nb, tt, d), lambda i, j: (i, j, 0)),
        scratch_shapes=[pltpu.VMEM((rows, d), F32), pltpu.VMEM((rows, d), F32)] + cache_scr,
        compiler_params=pltpu.CompilerParams(dimension_semantics=semantics,
                                             vmem_limit_bytes=_VMEM_LIMIT_BYTES),
        name="xattn",
    )(x, g.reshape(1, d), wq, wo, mk, mv)


def _even_kernel(x_ref, g_ref, win_ref, wout_ref, cw_ref, lb_ref, gn_ref, conv_in_ref, hg_in_ref,
                 xo_ref, conv_out_ref, hg_out_ref,
                 p_scr, y_scr, ext_scr, b_scr, qg_scr, kk_scr, o_scr, st_scr, *, nb, tt, blk, one_tile):
    j = pl.program_id(1)
    nj = pl.num_programs(1)
    d = x_ref.shape[-1]
    rows = nb * tt
    aw = A_WIDTH
    hd = B_DIM

    def transpose_states(src, dst):
        if one_tile:
            for s in range(nb):
                for h in range(B_HEADS):
                    dst[s, h] = src[s, h].T
        else:
            def body(s, carry):
                for h in range(B_HEADS):
                    dst[s, h] = src[s, h].T
                return carry

            lax.fori_loop(0, nb, body, 0)

    def load_state():
        conv_out_ref[...] = conv_in_ref[...]
        transpose_states(hg_in_ref, st_scr)

    if one_tile:
        load_state()
    else:
        pl.when(j == 0)(load_state)

    x = x_ref[...].reshape(rows, d)
    h_in = _rms(x, g_ref[...]).astype(_MXU_DTYPE)
    n_parts = 2 if (nb % 2 == 0 and rows >= 2 * _MXU_WIDTH) else 1
    pnb = nb // n_parts
    prow = pnb * tt
    lb = lb_ref[...]
    cw = cw_ref[...]
    rowi = lax.broadcasted_iota(jnp.int32, (1, _SUBLANES, hd), 1)
    sp = math.gcd(pnb, _SEQS_PER_PASS * _SUBLANES // blk)
    chains = sp * B_HEADS
    col_blocks = lambda width: [(c, min(c + _MXU_WIDTH, width)) for c in range(0, width, _MXU_WIDTH)]

    def project_steps(part):
        r = slice(part * prow, (part + 1) * prow)

        def step(c, e):
            p_scr[r, c:e] = jnp.dot(h_in[r], win_ref[:, c:e], preferred_element_type=F32)

        return [functools.partial(step, c, e) for c, e in col_blocks(EVEN_IN)]

    def prepare(part):
        r = slice(part * prow, (part + 1) * prow)
        sq = slice(part * pnb, (part + 1) * pnb)
        u = p_scr[r, 2 * aw:3 * aw] * p_scr[r, 0:aw]
        ext_scr[sq, _SUBLANES:_SUBLANES + tt, :] = u.reshape(pnb, tt, aw)
        ext_scr[sq, _SUBLANES - 2:_SUBLANES, :] = conv_out_ref[sq]
        conv = (cw[0:1] * ext_scr[sq, _SUBLANES - 2:_SUBLANES - 2 + tt, :]
                + cw[1:2] * ext_scr[sq, _SUBLANES - 1:_SUBLANES - 1 + tt, :]
                + cw[2:3] * ext_scr[sq, _SUBLANES:_SUBLANES + tt, :])
        y_scr[r, 0:aw] = p_scr[r, aw:2 * aw] * conv.reshape(prow, aw)
        conv_out_ref[sq] = ext_scr[sq, _SUBLANES + tt - 2:_SUBLANES + tt, :]
        f = lb + (1.0 - lb) * _sigmoid(p_scr[r, 4 * aw:5 * aw])
        kk_scr[r, :] = 1.0 - f
        b_scr[r, :] = _mm_exact_lhs(_block_tri(prow, blk).astype(F32), jnp.log2(f))
        qg_scr[r, :] = _silu(p_scr[r, 3 * aw:4 * aw])

    def time_block(s0, k):
        rows_of = [slice((s0 + si) * tt + k * blk, (s0 + si) * tt + (k + 1) * blk) for si in range(sp)]
        stack = lambda ref, col0: jnp.stack([ref[rs, col0 + h * hd:col0 + (h + 1) * hd]
                                             for rs in rows_of for h in range(B_HEADS)])
        bb, qg, kkb = stack(b_scr, 0), stack(qg_scr, 0), stack(kk_scr, 0)
        vb = stack(p_scr, 5 * aw)
        st = st_scr[s0:s0 + sp].reshape(chains, hd, hd)
        blast = bb[:, blk - 1:blk, :]
        intra = []
        for p0 in range(0, blk, _SUBLANES):
            bi, qi = bb[:, p0:p0 + _SUBLANES, :], qg[:, p0:p0 + _SUBLANES, :]
            oi = jnp.zeros_like(bi)
            for jj in range(p0 + _SUBLANES):
                diff = bi - bb[:, jj:jj + 1, :]
                if jj >= p0:
                    diff = jnp.where(rowi >= jj - p0, diff, -jnp.inf)
                att = jnp.sum(qi * kkb[:, jj:jj + 1, :] * jnp.exp2(diff), axis=-1, keepdims=True)
                oi = oi + att * vb[:, jj:jj + 1, :]
            intra.append(oi)
        o = _bmm_nt(qg * jnp.exp2(bb), st) + (intra[0] if len(intra) == 1 else jnp.concatenate(intra, axis=1))
        kd = kkb * jnp.exp2(blast - bb)
        upd = jnp.stack([_mm_tn(vb[c], kd[c]) for c in range(chains)])
        st_scr[s0:s0 + sp] = (st * jnp.exp2(blast) + upd).reshape(sp, B_HEADS, hd, hd)
        for si, rs in enumerate(rows_of):
            for h in range(B_HEADS):
                o_scr[rs, h * hd:(h + 1) * hd] = o[si * B_HEADS + h]

    def recurrence_steps(part):
        return [functools.partial(time_block, part * pnb + sg * sp, k)
                for sg in range(pnb // sp) for k in range(tt // blk)]

    def finish(part):
        r = slice(part * prow, (part + 1) * prow)
        for h in range(B_HEADS):
            hs = slice(h * hd, (h + 1) * hd)
            o = o_scr[r, hs]
            on = o * lax.rsqrt(jnp.mean(o * o, axis=-1, keepdims=True) + NORM_EPS) * gn_ref[:, hs]
            y_scr[r, aw + h * hd:aw + (h + 1) * hd] = on * _silu(p_scr[r, 6 * aw + h * hd:6 * aw + (h + 1) * hd])

    def outproj_steps(part):
        r = slice(part * prow, (part + 1) * prow)
        sq = slice(part * pnb, (part + 1) * pnb)

        def step(c, e):
            y = y_scr[r, :].astype(_MXU_DTYPE)
            res = x[r, c:e] + jnp.dot(y, wout_ref[:, c:e], preferred_element_type=F32)
            xo_ref[sq, :, c:e] = res.reshape(pnb, tt, e - c)

        return [functools.partial(step, c, e) for c, e in col_blocks(d)]

    def run_interleaved(main, other):
        done = 0
        for i, step in enumerate(main):
            upto = (i + 1) * len(other) // len(main)
            for o_step in other[done:upto]:
                o_step()
            done = upto
            step()

    for step in project_steps(0):
        step()
    prepare(0)
    for part in range(n_parts):
        last = part == n_parts - 1
        other = [s for p in range(n_parts - 1) for s in outproj_steps(p)] if last else project_steps(part + 1)
        run_interleaved(recurrence_steps(part), other if n_parts > 1 else [])
        finish(part)
        if not last:
            prepare(part + 1)
    for step in outproj_steps(n_parts - 1):
        step()

    if one_tile:
        transpose_states(st_scr, hg_out_ref)
    else:
        pl.when(j == nj - 1)(lambda: transpose_states(st_scr, hg_out_ref))


def _even_layer(x, g, win, wout, cw, lb, gn, conv_in, hg_in, nb, tt, blk):
    b, t, d = x.shape
    rows = nb * tt
    return pl.pallas_call(
        functools.partial(_even_kernel, nb=nb, tt=tt, blk=blk, one_tile=(t == tt)),
        out_shape=(jax.ShapeDtypeStruct((b, t, d), F32),
                   jax.ShapeDtypeStruct(conv_in.shape, F32),
                   jax.ShapeDtypeStruct(hg_in.shape, F32)),
        grid=(b // nb, t // tt),
        in_specs=[
            pl.BlockSpec((nb, tt, d), lambda i, j: (i, j, 0)),
            _const_spec((1, d)),
            _const_spec(win.shape),
            _const_spec(wout.shape),
            _const_spec(cw.shape),
            _const_spec((1, A_WIDTH)),
            _const_spec((1, B_HEADS * B_DIM)),
            pl.BlockSpec((nb,) + conv_in.shape[1:], lambda i, j: (i, 0, 0)),
            pl.BlockSpec((nb,) + hg_in.shape[1:], lambda i, j: (i, 0, 0, 0)),
        ],
        out_specs=(
            pl.BlockSpec((nb, tt, d), lambda i, j: (i, j, 0)),
            pl.BlockSpec((nb,) + conv_in.shape[1:], lambda i, j: (i, 0, 0)),
            pl.BlockSpec((nb,) + hg_in.shape[1:], lambda i, j: (i, 0, 0, 0)),
        ),
        scratch_shapes=[
            pltpu.VMEM((rows, EVEN_IN), F32),
            pltpu.VMEM((rows, 2 * A_WIDTH), F32),
            pltpu.VMEM((nb, tt + _SUBLANES, A_WIDTH), F32),
            pltpu.VMEM((rows, A_WIDTH), F32),
            pltpu.VMEM((rows, A_WIDTH), F32),
            pltpu.VMEM((rows, A_WIDTH), F32),
            pltpu.VMEM((rows, A_WIDTH), F32),
            pltpu.VMEM((nb, B_HEADS, B_DIM, B_DIM), F32),
        ],
        compiler_params=pltpu.CompilerParams(dimension_semantics=("parallel", "arbitrary"),
                                             vmem_limit_bytes=_VMEM_LIMIT_BYTES),
        name="even_layer",
    )(x, g.reshape(1, d), win, wout, cw, lb.reshape(1, -1), gn.reshape(1, -1), conv_in, hg_in)


def _odd_kernel(x_ref, g_ref, win_ref, wout_ref, cos_ref, sin_ref, qdec_ref, kdec_ref, cdec_ref, dmask_ref,
                rgn_ref, mu_ref, w0_ref, w2_ref, a0_ref, a2_ref, g2_ref, kk_ref, ka_ref, rk_ref, lng_ref,
                lnb_ref, ret_in_ref, rw_in_ref, sh_in_ref,
                xo_ref, ret_out_ref, rw_out_ref, sh_out_ref,
                p_scr, y_scr, ext_scr, qr_scr, kr_scr, ir_scr,
                rt_scr, at_scr, bt_scr, kt_scr, vd_scr, cum_scr, od_scr, ah_scr, rh_scr, u0_scr, o0_scr,
                *, nb, tt, ch, gl, one_tile):
    j = pl.program_id(1)
    d = x_ref.shape[-1]
    rows = nb * tt
    cw = C_HEADS * C_DIM
    hd = C_DIM
    nd = D_HDIM
    n_groups = rows // gl
    cpg = gl // ch
    cps = tt // ch

    def load_state():
        ret_out_ref[...] = ret_in_ref[...]
        rw_out_ref[...] = rw_in_ref[...]
        sh_out_ref[...] = sh_in_ref[...]

    if one_tile and cps == 1:
        sh_out_ref[...] = sh_in_ref[...]
        ret_src, rw_src = ret_in_ref, rw_in_ref
    else:
        if one_tile:
            load_state()
        else:
            pl.when(j == 0)(load_state)
        ret_src, rw_src = ret_out_ref, rw_out_ref

    x = x_ref[...].reshape(rows, d)
    h_in = _rms(x, g_ref[...]).astype(_MXU_DTYPE)
    gpi = math.gcd(n_groups, _SEQS_PER_PASS)
    sp = math.gcd(nb, _SEQS_PER_PASS)

    def chain_rows(s0, k):
        return [pl.ds(pl.multiple_of((s0 + si) * tt + k * ch, _SUBLANES), ch) for si in range(sp)]

    _project(h_in, win_ref, p_scr, C_IN, ODD_IN)

    pd = p_scr[:, C_IN:ODD_IN]
    ext_scr[:, _SUBLANES:_SUBLANES + tt, :] = pd.reshape(nb, tt, D_IN)
    ext_scr[:, _SUBLANES - 1:_SUBLANES, :] = sh_out_ref[...]
    prev = ext_scr[:, _SUBLANES - 1:_SUBLANES - 1 + tt, :].reshape(rows, D_IN)
    sh_out_ref[...] = ext_scr[:, _SUBLANES + tt - 1:_SUBLANES + tt, :]
    pm = pd + mu_ref[...] * (prev - pd)
    dw = D_WIDTH
    r = pm[:, 0:dw]
    kd = pm[:, dw:2 * dw]
    vd = pm[:, 2 * dw:3 * dw]
    w_dn = pm[:, 3 * dw:3 * dw + 64]
    a_dn = pm[:, 3 * dw + 64:3 * dw + 128]
    g_dn = pm[:, 3 * dw + 128:3 * dw + 256]
    logdec = -math.exp(-0.5) * _sigmoid(w0_ref[...] + _mm(jnp.tanh(w_dn), w2_ref[...]))
    a = _sigmoid(a0_ref[...] + _mm(a_dn, a2_ref[...]))
    gate = _mm(_sigmoid(g_dn), g2_ref[...])
    kk0 = kd * kk_ref[...]
    kk = kk0 / jnp.maximum(jnp.sqrt(_head_sums(kk0 * kk0, nd)), 1e-12)
    kd2 = kd * (1.0 + (a - 1.0) * ka_ref[...])
    cum = _mm_exact_lhs(_block_tri(rows, ch).astype(F32), logdec)
    cum_scr[...] = cum
    rt_scr[...] = r * jnp.exp(cum)
    at_scr[...] = -kk * jnp.exp(cum - logdec)
    e_neg = jnp.exp(-cum)
    bt_scr[...] = kk * a * e_neg
    kt_scr[...] = kd2 * e_neg
    vd_scr[...] = vd
    bonus = _head_sums(r * kd2 * rk_ref[...], nd) * vd

    strict = _block_tri(gl, ch, strict=True)
    incl = _block_tri(gl, ch)
    eye = (lax.broadcasted_iota(jnp.int32, (gl, gl), 0) == lax.broadcasted_iota(jnp.int32, (gl, gl), 1)).astype(F32)
    n_dbl = int(math.log2(ch))

    def rw_prepare(it, carry):
        def heads(ref):
            return jnp.stack([ref[pl.ds(pl.multiple_of((it * gpi + g) * gl, _SUBLANES), gl), h * nd:(h + 1) * nd]
                              for g in range(gpi) for h in range(D_HEADS)])

        at, rt, bt, kt, vv = (heads(ref) for ref in (at_scr, rt_scr, bt_scr, kt_scr, vd_scr))
        sc = _bmm_nt(jnp.concatenate([at, rt], axis=1), jnp.concatenate([bt, kt], axis=1))
        n_ab = jnp.where(strict, sc[:, 0:gl, 0:gl], 0.0)
        a_ak = jnp.where(strict, sc[:, 0:gl, gl:2 * gl], 0.0)
        r_rb = jnp.where(incl, sc[:, gl:2 * gl, 0:gl], 0.0)
        r_rk = jnp.where(incl, sc[:, gl:2 * gl, gl:2 * gl], 0.0)
        xinv = eye + n_ab
        pw = n_ab
        for _ in range(n_dbl - 1):
            pw = _bmm(pw, pw)
            xinv = xinv + _bmm(pw, xinv)
        resid = eye - (xinv - _bmm_hi(n_ab, xinv))
        xinv = xinv + _bmm(xinv, resid)
        xa = _bmm(xinv, jnp.concatenate([at, _bmm(a_ak, vv)], axis=2))
        ah = xa[:, :, 0:nd]
        u0 = xa[:, :, nd:2 * nd]
        ra = _bmm(r_rb, xa)
        rh = rt + ra[:, :, 0:nd]
        o0 = ra[:, :, nd:2 * nd] + _bmm(r_rk, vv)
        for g in range(gpi):
            part = slice(g * D_HEADS, (g + 1) * D_HEADS)
            ah_scr[it * gpi + g] = ah[part]
            u0_scr[it * gpi + g] = u0[part]
            rh_scr[it * gpi + g] = rh[part]
            o0_scr[it * gpi + g] = o0[part]
        return carry

    lax.fori_loop(0, n_groups // gpi, rw_prepare, 0)

    _project(h_in, win_ref, p_scr, 0, C_IN)
    cos = cos_ref[0]
    sin = sin_ref[0]
    for h in range(C_HEADS):
        hs = slice(h * hd, (h + 1) * hd)
        qh = p_scr[:, h * hd:(h + 1) * hd]
        kh = p_scr[:, cw + h * hd:cw + (h + 1) * hd]
        qr_scr[:, hs] = qh * cos + pltpu.roll(qh, hd // 2, axis=1) * sin
        kr_scr[:, hs] = (kh * cos + pltpu.roll(kh, hd // 2, axis=1) * sin) * (C_DIM ** -0.5)

    def ret_intra(it, carry):
        rows_g = [pl.ds(pl.multiple_of((it * gpi + g) * gl, _SUBLANES), gl) for g in range(gpi)]
        stack = lambda ref, col0: jnp.stack([ref[rg, col0 + h * hd:col0 + (h + 1) * hd]
                                             for rg in rows_g for h in range(C_HEADS)])
        dm = dmask_ref[...]
        dm = dm if gpi == 1 else jnp.concatenate([dm] * gpi, axis=0)
        o = _bmm(_bmm_nt(stack(qr_scr, 0), stack(kr_scr, 0)) * dm, stack(p_scr, 2 * cw))
        for g, rg in enumerate(rows_g):
            for h in range(C_HEADS):
                ir_scr[rg, h * hd:(h + 1) * hd] = o[g * C_HEADS + h]
        return carry

    lax.fori_loop(0, n_groups // gpi, ret_intra, 0)

    per_chain = lambda ref, r: jnp.stack([ref[r, h * hd:(h + 1) * hd] for _ in range(sp) for h in range(C_HEADS)])
    qdec_c = per_chain(qdec_ref, slice(0, ch))
    kdec_c = per_chain(kdec_ref, slice(0, ch))
    cdec_c = per_chain(cdec_ref, slice(0, 1))

    def ret_seq_group(sg, carry):
        s0 = sg * sp

        def ret_chunk(k, c2):
            rows_c = chain_rows(s0, k)
            stack = lambda ref, col0: jnp.stack([ref[rc, col0 + h * hd:col0 + (h + 1) * hd]
                                                 for rc in rows_c for h in range(C_HEADS)])
            qc, kc, vc = stack(qr_scr, 0), stack(kr_scr, 0), stack(p_scr, 2 * cw)
            st = ret_src[pl.ds(s0, sp)].reshape(sp * C_HEADS, hd, hd)
            inter = _bmm(qc, st) * qdec_c
            kcd = kc * kdec_c
            upd = jnp.stack([_mm_tn(kcd[c], vc[c]) for c in range(sp * C_HEADS)])
            ret_out_ref[pl.ds(s0, sp)] = (cdec_c * st + upd).reshape(sp, C_HEADS, hd, hd)
            for si, rc in enumerate(rows_c):
                for h in range(C_HEADS):
                    ir_scr[rc, h * hd:(h + 1) * hd] += inter[si * C_HEADS + h]
            return c2

        lax.fori_loop(0, cps, ret_chunk, 0)
        return carry

    lax.fori_loop(0, nb // sp, ret_seq_group, 0)

    for h in range(C_HEADS):
        hs = slice(h * hd, (h + 1) * hd)
        o = ir_scr[:, hs]
        on = o * lax.rsqrt(jnp.mean(o * o, axis=-1, keepdims=True) + NORM_EPS) * rgn_ref[:, hs]
        y_scr[:, hs] = on * _silu(p_scr[:, 3 * cw + h * hd:3 * cw + (h + 1) * hd])

    n_chains = sp * D_HEADS

    def rw_seq_group(sg, carry):
        s0 = sg * sp

        def rw_chunk(k, c2):
            rows_c = chain_rows(s0, k)
            where = []
            for si in range(sp):
                cidx = (s0 + si) * cps + k
                where.append((cidx // cpg, pl.ds(pl.multiple_of((cidx % cpg) * ch, _SUBLANES), ch)))
            prep = lambda ref: jnp.stack([ref[gi, h, ls, :] for gi, ls in where for h in range(D_HEADS)])
            rowsl = lambda ref: jnp.stack([ref[rc, h * nd:(h + 1) * nd] for rc in rows_c for h in range(D_HEADS)])
            st = rw_src[pl.ds(s0, sp)].reshape(n_chains, nd, nd)
            ars = _bmm_nt(jnp.concatenate([prep(ah_scr), prep(rh_scr)], axis=1), st)
            uv = jnp.concatenate([ars[:, 0:ch, :] + prep(u0_scr), rowsl(vd_scr)], axis=1)
            bk = jnp.concatenate([rowsl(bt_scr), rowsl(kt_scr)], axis=1)
            upd = jnp.stack([_mm_tn(uv[c], bk[c]) for c in range(n_chains)])
            glast = jnp.stack([cum_scr[pl.ds(pl.multiple_of((s0 + si) * tt + (k + 1) * ch - _SUBLANES, _SUBLANES),
                                             _SUBLANES), h * nd:(h + 1) * nd][_SUBLANES - 1:_SUBLANES]
                               for si in range(sp) for h in range(D_HEADS)])
            rw_out_ref[pl.ds(s0, sp)] = ((st + upd) * jnp.exp(glast)).reshape(sp, D_HEADS, nd, nd)
            o = ars[:, ch:2 * ch, :] + prep(o0_scr)
            for si, rc in enumerate(rows_c):
                for h in range(D_HEADS):
                    od_scr[rc, h * nd:(h + 1) * nd] = o[si * D_HEADS + h]
            return c2

        lax.fori_loop(0, cps, rw_chunk, 0)
        return carry

    lax.fori_loop(0, nb // sp, rw_seq_group, 0)

    o_d = od_scr[...]
    mean = _head_sums(o_d, nd) * (1.0 / nd)
    xc = o_d - mean
    var = _head_sums(xc * xc, nd) * (1.0 / nd)
    on = xc * lax.rsqrt(var + RWKV_GN_EPS) * lng_ref[...] + lnb_ref[...]
    y_scr[:, cw:cw + dw] = (on + bonus) * gate

    xo_ref[...] = (x + _mm(y_scr[...], wout_ref[...])).reshape(nb, tt, d)


def _odd_layer(x, g, win, wout, tabs, rgn, rw, ret_in, rw_in, sh_in, nb, tt, ch, gl):
    b, t, d = x.shape
    rows = nb * tt
    cos, sin, qdec, kdec, cdec, dmask = tabs
    sh3 = sh_in.reshape(b, 1, D_IN)
    vecs = [rw[k].reshape(1, -1) for k in ("mu", "w0")] + [rw["w2"]] + [rw["a0"].reshape(1, -1), rw["a2"], rw["g2"]] + \
           [rw[k].reshape(1, -1) for k in ("k_k", "k_a", "r_k", "lnx_g", "lnx_b")]
    w512 = pltpu.VMEM((rows, D_WIDTH), F32)
    outs = pl.pallas_call(
        functools.partial(_odd_kernel, nb=nb, tt=tt, ch=ch, gl=gl, one_tile=(t == tt)),
        out_shape=(jax.ShapeDtypeStruct((b, t, d), F32),
                   jax.ShapeDtypeStruct(ret_in.shape, F32),
                   jax.ShapeDtypeStruct(rw_in.shape, F32),
                   jax.ShapeDtypeStruct(sh3.shape, F32)),
        grid=(b // nb, t // tt),
        in_specs=[
            pl.BlockSpec((nb, tt, d), lambda i, j: (i, j, 0)),
            _const_spec((1, d)),
            _const_spec(win.shape),
            _const_spec(wout.shape),
            pl.BlockSpec((1, rows, C_DIM), lambda i, j: (j, 0, 0)),
            pl.BlockSpec((1, rows, C_DIM), lambda i, j: (j, 0, 0)),
            _const_spec(qdec.shape),
            _const_spec(kdec.shape),
            _const_spec(cdec.shape),
            _const_spec(dmask.shape),
            _const_spec((1, C_HEADS * C_DIM)),
        ] + [_const_spec(v.shape) for v in vecs] + [
            pl.BlockSpec((nb,) + ret_in.shape[1:], lambda i, j: (i, 0, 0, 0)),
            pl.BlockSpec((nb,) + rw_in.shape[1:], lambda i, j: (i, 0, 0, 0)),
            pl.BlockSpec((nb, 1, D_IN), lambda i, j: (i, 0, 0)),
        ],
        out_specs=(
            pl.BlockSpec((nb, tt, d), lambda i, j: (i, j, 0)),
            pl.BlockSpec((nb,) + ret_in.shape[1:], lambda i, j: (i, 0, 0, 0)),
            pl.BlockSpec((nb,) + rw_in.shape[1:], lambda i, j: (i, 0, 0, 0)),
            pl.BlockSpec((nb, 1, D_IN), lambda i, j: (i, 0, 0)),
        ),
        scratch_shapes=[
            pltpu.VMEM((rows, ODD_IN), F32),
            pltpu.VMEM((rows, C_HEADS * C_DIM + D_WIDTH), F32),
            pltpu.VMEM((nb, tt + _SUBLANES, D_IN), F32),
        ] + [w512] * 10 + [pltpu.VMEM((rows // gl, D_HEADS, gl, D_HDIM), F32)] * 4,
        compiler_params=pltpu.CompilerParams(dimension_semantics=("parallel", "arbitrary"),
                                             vmem_limit_bytes=_VMEM_LIMIT_BYTES),
        name="odd_layer",
    )(x, g.reshape(1, d), win, wout, cos, sin, qdec, kdec, cdec, dmask, rgn.reshape(1, -1), *vecs,
      ret_in, rw_in, sh3)
    xo, ret_o, rw_o, sh_o = outs
    return xo, ret_o, rw_o, sh_o.reshape(b, D_IN)


def _odd_tables(pos0, t, nb, tt, ch, gl):
    half = C_DIM // 2
    inv = ROPE_BASE ** (-jnp.arange(half, dtype=F32) / half)
    pos = pos0 + jnp.arange(t, dtype=jnp.int32)
    ang = pos.astype(F32)[:, None] * inv[None, :]
    cos = jnp.cos(ang)
    sin = jnp.sin(ang)
    cosf = jnp.concatenate([cos, cos], axis=-1).reshape(t // tt, tt, C_DIM)
    sinf = jnp.concatenate([-sin, sin], axis=-1).reshape(t // tt, tt, C_DIM)
    cosf = jnp.tile(cosf, (1, nb, 1))
    sinf = jnp.tile(sinf, (1, nb, 1))
    lg = jnp.log1p(-jnp.exp2(-5.0 - jnp.arange(C_HEADS, dtype=F32)))[:, None]
    idx = jnp.arange(ch, dtype=F32)
    q_dec = jnp.exp(lg * (idx + 1.0))
    k_dec = jnp.exp(lg * (ch - 1.0 - idx))
    rel = idx[:, None] - idx[None, :]
    dm = jnp.where(rel >= 0, jnp.exp(lg[:, :, None] * jnp.maximum(rel, 0.0)), 0.0)
    c_dec = jnp.exp(lg * ch)
    cpg = gl // ch
    expand = lambda v: jnp.repeat(jnp.tile(v.T, (cpg, 1)), C_DIM, axis=1)
    qdec = expand(q_dec)
    kdec = expand(k_dec)
    cdec = jnp.repeat(c_dec.T, C_DIM, axis=1)
    gi = jnp.arange(gl)
    same = (gi[:, None] // ch) == (gi[None, :] // ch)
    dmask = jnp.where(same[None], jnp.tile(dm, (1, cpg, cpg)), 0.0)
    return cosf, sinf, qdec, kdec, cdec, dmask


def _prep_weights(ffn1_w_gu, ffn1_w_down, ffn2_w_gu, ffn2_w_down, even_w_in, even_w_out, odd_w_in, odd_w_out,
                  xattn_wq, xattn_wkv, xattn_wo):
    bf = lambda w: w.astype(_MXU_DTYPE)
    return dict(ffn1=(bf(ffn1_w_gu), bf(ffn1_w_down)), ffn2=(bf(ffn2_w_gu), bf(ffn2_w_down)),
                even_in=bf(even_w_in), even_out=bf(even_w_out), odd_in=bf(odd_w_in), odd_out=bf(odd_w_out),
                wq=bf(xattn_wq), wkv=bf(xattn_wkv), wo=bf(xattn_wo))


def _trunks(groups, W, P):
    depth = P["ffn1_norm"].shape[0]
    d = groups[0]["x"].shape[-1]
    lb_all = jnp.cumsum(jax.nn.softmax(P["hgrn_lb"].astype(F32), axis=0), axis=0)
    xs = [g["x"] for g in groups]
    new = [dict(conv=[], hg=[], ret=[], rw=[], sh=[]) for _ in groups]
    tm = min(g["cfg"]["tm"] for g in groups)
    tf = groups[0]["cfg"]["tf"]

    def ffn_both(xs, norm, weights, l, fin):
        flat = [x.reshape(-1, d) for x in xs]
        out = _ffn(flat[0], flat[1], norm, weights[0], weights[1], l, fin, tm, tf)
        return [o.reshape(x.shape) for o, x in zip(out, xs)]

    for l in range(depth):
        jl = l // 2
        xs = ffn_both(xs, P["ffn1_norm"][l], W["ffn1"], l, None)
        for gi, g in enumerate(groups):
            cfg, x = g["cfg"], xs[gi]
            if l % 2 == 0:
                x, cb, sh = _even_layer(x, P["mix_norm"][l], W["even_in"][jl], W["even_out"][jl], P["conv_w"][jl],
                                        lb_all[jl], P["hgrn_gnorm"][jl], g["conv"][jl], g["hg"][jl],
                                        cfg["enb"], cfg["ett"], cfg["blk"])
                new[gi]["conv"].append(cb)
                new[gi]["hg"].append(sh)
            else:
                tabs = _odd_tables(g["pos0"], x.shape[1], cfg["nb"], cfg["tt"], cfg["ch"], cfg["gl"])
                rw = {k: P["rwkv_" + k][jl] for k in ("mu", "w0", "w2", "a0", "a2", "g2", "k_k", "k_a", "r_k",
                                                      "lnx_g", "lnx_b")}
                x, sr, sw, ss = _odd_layer(x, P["mix_norm"][l], W["odd_in"][jl], W["odd_out"][jl], tabs,
                                           P["ret_gnorm"][jl], rw, g["ret"][jl], g["rw"][jl], g["sh"][jl],
                                           cfg["nb"], cfg["tt"], cfg["ch"], cfg["gl"])
                new[gi]["ret"].append(sr)
                new[gi]["rw"].append(sw)
                new[gi]["sh"].append(ss)
            xs[gi] = _xattn(x, P["xattn_norm"][l], W["wq"], W["wo"], g["mem_k"], g["mem_v"], l,
                            cfg["xnb"], cfg["xtt"])
        xs = ffn_both(xs, P["ffn2_norm"][l], W["ffn2"], l, P["final_norm"] if l == depth - 1 else None)
    return [(x,) + tuple(jnp.stack(n[k]) for k in ("conv", "hg", "ret", "rw", "sh")) for x, n in zip(xs, new)]


_ROW_TILE = 2 * _MXU_WIDTH
_ODD_ROW_TILE = _MXU_WIDTH
_CHUNK = 64
_HGRN_BLOCK = 2 * _SUBLANES
_SHORT_ROWS = 128
_XATTN_SHORT_SEQS = 8


def _configs(b, t):
    if t >= _ROW_TILE // 2:
        nb = math.gcd(b, _SEQS_PER_PASS)
        enb = math.gcd(b, 2 * _SEQS_PER_PASS)
        return dict(tm=_ROW_TILE, tf=_MXU_WIDTH, nb=nb, tt=_ODD_ROW_TILE // nb, enb=enb, ett=_ROW_TILE // enb,
                    blk=_HGRN_BLOCK, ch=_CHUNK, gl=_CHUNK, xnb=1, xtt=min(t, _ROW_TILE))
    nb = min(b, _SHORT_ROWS // t)
    return dict(tm=min(b * t, _ROW_TILE), tf=_MXU_WIDTH, nb=nb, tt=t, enb=nb, ett=t, blk=t, ch=t, gl=nb * t,
                xnb=min(b, _XATTN_SHORT_SEQS), xtt=t)


def kernel(x_prompt, x_sample, state_conv, state_hgrn, state_ret, state_rwkv, state_shift, cache_mem_k, cache_mem_v, mem_prompt, ffn1_norm, ffn1_w_gu, ffn1_w_down, mix_norm, even_w_in, even_w_out, conv_w, hgrn_lb, hgrn_gnorm, odd_w_in, odd_w_out, ret_gnorm, rwkv_mu, rwkv_w0, rwkv_w2, rwkv_a0, rwkv_a2, rwkv_g2, rwkv_k_k, rwkv_k_a, rwkv_r_k, rwkv_lnx_g, rwkv_lnx_b, xattn_norm, mem_norm, xattn_wq, xattn_wkv, xattn_wo, ffn2_norm, ffn2_w_gu, ffn2_w_down, final_norm):
    P = dict(ffn1_norm=ffn1_norm, mix_norm=mix_norm, conv_w=conv_w, hgrn_lb=hgrn_lb, hgrn_gnorm=hgrn_gnorm,
             ret_gnorm=ret_gnorm, rwkv_mu=rwkv_mu, rwkv_w0=rwkv_w0, rwkv_w2=rwkv_w2, rwkv_a0=rwkv_a0,
             rwkv_a2=rwkv_a2, rwkv_g2=rwkv_g2, rwkv_k_k=rwkv_k_k, rwkv_k_a=rwkv_k_a,
             rwkv_r_k=rwkv_r_k.reshape(rwkv_r_k.shape[0], -1), rwkv_lnx_g=rwkv_lnx_g, rwkv_lnx_b=rwkv_lnx_b,
             xattn_norm=xattn_norm, ffn2_norm=ffn2_norm, final_norm=final_norm)
    W = _prep_weights(ffn1_w_gu, ffn1_w_down, ffn2_w_gu, ffn2_w_down, even_w_in, even_w_out, odd_w_in,
                      odd_w_out, xattn_wq, xattn_wkv, xattn_wo)
    bp, tp, d = x_prompt.shape
    bs, ts, _ = x_sample.shape
    depth = ffn1_norm.shape[0]

    mem_k_p, mem_v_p, mem_k_out, mem_v_out = _mem_kv(mem_prompt, mem_norm, W["wkv"], X_HEADS)

    z = lambda ref: jnp.zeros((ref.shape[0], bp) + ref.shape[2:], F32)
    prompt = dict(x=x_prompt, pos0=0, conv=z(state_conv), hg=z(state_hgrn), ret=z(state_ret), rw=z(state_rwkv),
                  sh=z(state_shift), mem_k=mem_k_p, mem_v=mem_v_p, cfg=_configs(bp, tp))
    sample = dict(x=x_sample, pos0=PAST_LEN, conv=state_conv, hg=state_hgrn, ret=state_ret, rw=state_rwkv,
                  sh=state_shift, mem_k=cache_mem_k, mem_v=cache_mem_v, cfg=_configs(bs, ts))
    (y_p, conv_p, hg_p, ret_p, rw_p, sh_p), (y_s, conv_s, hg_s, ret_s, rw_s, sh_s) = _trunks([prompt, sample], W, P)
    return (y_p, y_s, conv_p, hg_p, ret_p, rw_p, sh_p, mem_k_out, mem_v_out,
            conv_s, hg_s, ret_s, rw_s, sh_s)
```

```python
import functools
import math

import jax
import jax.numpy as jnp
from jax import lax
from jax.experimental import pallas as pl
from jax.experimental.pallas import tpu as pltpu

F32 = jnp.float32
_MXU_DTYPE = jnp.bfloat16

PAST_LEN = 16384
NORM_EPS = 1e-6
RWKV_GN_EPS = 64e-5
ROPE_BASE = 10000.0

_VMEM_LIMIT_BYTES = 60 * 1024 * 1024
_SUBLANES = 8
_PACKED_ROWS = 16
_MXU_WIDTH = 256
_SEQS_PER_PASS = 4

A_WIDTH = 512
B_HEADS, B_DIM = 4, 128
C_HEADS, C_DIM = 4, 128
D_HEADS, D_HDIM = 8, 64
D_WIDTH = D_HEADS * D_HDIM
EVEN_IN = 7 * 512
C_IN = 4 * 512
D_IN = 3 * 512 + 64 + 64 + 128
ODD_IN = C_IN + D_IN
X_HEADS = 4


def _mm(a, b):
    return jnp.dot(a.astype(_MXU_DTYPE), b.astype(_MXU_DTYPE), preferred_element_type=F32)


def _mm_nt(a, b):
    return lax.dot_general(a.astype(_MXU_DTYPE), b.astype(_MXU_DTYPE), (((1,), (1,)), ((), ())),
                           preferred_element_type=F32)


def _mm_tn(a, b):
    k = a.shape[0]
    if k % _PACKED_ROWS:
        pad = _PACKED_ROWS - k % _PACKED_ROWS
        a = jnp.concatenate([a, jnp.zeros((pad, a.shape[1]), a.dtype)], axis=0)
        b = jnp.concatenate([b, jnp.zeros((pad, b.shape[1]), b.dtype)], axis=0)
    return lax.dot_general(a.astype(_MXU_DTYPE), b.astype(_MXU_DTYPE), (((0,), (0,)), ((), ())),
                           preferred_element_type=F32)


def _project(h, w_ref, out_ref, col0, col1, step=_MXU_WIDTH):
    for c in range(col0, col1, step):
        e = min(c + step, col1)
        out_ref[:, c:e] = jnp.dot(h, w_ref[:, c:e], preferred_element_type=F32)


def _split2(a):
    hi = a.astype(_MXU_DTYPE)
    lo = (a - hi.astype(F32)).astype(_MXU_DTYPE)
    return hi, lo


def _mm_hi(a, b):
    ah, al = _split2(a)
    bh, bl = _split2(b)
    d = functools.partial(jnp.dot, preferred_element_type=F32)
    return d(ah, bh) + d(ah, bl) + d(al, bh)


_BATCH_NN = (((2,), (1,)), ((0,), (0,)))
_BATCH_NT = (((2,), (2,)), ((0,), (0,)))


def _bmm(a, b):
    return lax.dot_general(a.astype(_MXU_DTYPE), b.astype(_MXU_DTYPE), _BATCH_NN, preferred_element_type=F32)


def _bmm_nt(a, b):
    return lax.dot_general(a.astype(_MXU_DTYPE), b.astype(_MXU_DTYPE), _BATCH_NT, preferred_element_type=F32)


def _bmm_hi(a, b):
    ah, al = _split2(a)
    bh, bl = _split2(b)
    d = functools.partial(lax.dot_general, dimension_numbers=_BATCH_NN, preferred_element_type=F32)
    return d(ah, bh) + d(ah, bl) + d(al, bh)


def _mm_exact_lhs(m01, x):
    m = m01.astype(_MXU_DTYPE)
    x0 = x.astype(_MXU_DTYPE)
    r1 = x - x0.astype(F32)
    x1 = r1.astype(_MXU_DTYPE)
    x2 = (r1 - x1.astype(F32)).astype(_MXU_DTYPE)
    d = functools.partial(jnp.dot, preferred_element_type=F32)
    return d(m, x0) + d(m, x1) + d(m, x2)


def _mm_exact_rhs(x, m01):
    m = m01.astype(_MXU_DTYPE)
    x0 = x.astype(_MXU_DTYPE)
    r1 = x - x0.astype(F32)
    x1 = r1.astype(_MXU_DTYPE)
    x2 = (r1 - x1.astype(F32)).astype(_MXU_DTYPE)
    d = functools.partial(jnp.dot, preferred_element_type=F32)
    return d(x0, m) + d(x1, m) + d(x2, m)


def _rms(x, g):
    return x * lax.rsqrt(jnp.mean(x * x, axis=-1, keepdims=True) + NORM_EPS) * g


def _sigmoid(x):
    return 1.0 / (1.0 + jnp.exp(-x))


def _silu(x):
    return x * _sigmoid(x)


def _softplus(x):
    return jnp.maximum(x, 0.0) + jnp.log1p(jnp.exp(-jnp.abs(x)))


def _block_tri(n, blk, strict=False):
    r = lax.broadcasted_iota(jnp.int32, (n, n), 0)
    c = lax.broadcasted_iota(jnp.int32, (n, n), 1)
    same = (r // blk) == (c // blk)
    low = (c < r) if strict else (c <= r)
    return same & low


def _head_sums(x, hd):
    width = min(_MXU_WIDTH, x.shape[1])
    r = lax.broadcasted_iota(jnp.int32, (width, width), 0)
    c = lax.broadcasted_iota(jnp.int32, (width, width), 1)
    ones = ((r // hd) == (c // hd)).astype(F32)
    parts = [_mm_exact_rhs(x[:, i:i + width], ones) for i in range(0, x.shape[1], width)]
    return parts[0] if len(parts) == 1 else jnp.concatenate(parts, axis=1)


def _ffn_kernel(xa_ref, xb_ref, g_ref, wgu_ref, wd_ref, fg_ref, oa_ref, ob_ref, h_scr, acc_scr, *, tf, final, steps_a):
    dff = wd_ref.shape[0]
    in_a = pl.program_id(0) < steps_a
    x = jnp.where(in_a, xa_ref[...], xb_ref[...])
    h_scr[...] = _rms(x, g_ref[...]).astype(_MXU_DTYPE)
    for c in range(dff // tf):
        h = h_scr[...]
        gate = jnp.dot(h, wgu_ref[:, c * tf:(c + 1) * tf], preferred_element_type=F32)
        up = jnp.dot(h, wgu_ref[:, dff + c * tf:dff + (c + 1) * tf], preferred_element_type=F32)
        act = (_silu(gate) * up).astype(_MXU_DTYPE)
        part = jnp.dot(act, wd_ref[c * tf:(c + 1) * tf, :], preferred_element_type=F32)
        if c == 0:
            acc_scr[...] = part
        else:
            acc_scr[...] += part
    y = x + 0.5 * acc_scr[...]
    if final:
        y = _rms(y, fg_ref[...])

    @pl.when(in_a)
    def _():
        oa_ref[...] = y

    @pl.when(jnp.logical_not(in_a))
    def _():
        ob_ref[...] = y


def _const_spec(shape):
    nd = len(shape)
    return pl.BlockSpec(shape, lambda *_: (0,) * nd, pipeline_mode=pl.Buffered(1))


def _layer_spec(stacked_shape, layer):
    nd = len(stacked_shape) - 1
    return pl.BlockSpec((None,) + tuple(stacked_shape[1:]), lambda *_: (layer,) + (0,) * nd,
                        pipeline_mode=pl.Buffered(1))


def _ffn(xa, xb, g, wgu, wd, layer, final_g, tm, tf):
    (na, d), nb_rows = xa.shape, xb.shape[0]
    assert na % tm == 0 and nb_rows % tm == 0
    steps_a, steps_b = na // tm, nb_rows // tm
    final = final_g is not None
    fg = final_g if final else g
    spec_a = pl.BlockSpec((tm, d), lambda i: (jnp.minimum(i, steps_a - 1), 0))
    spec_b = pl.BlockSpec((tm, d), lambda i: (jnp.maximum(i - steps_a, 0), 0))
    return pl.pallas_call(
        functools.partial(_ffn_kernel, tf=tf, final=final, steps_a=steps_a),
        out_shape=(jax.ShapeDtypeStruct((na, d), F32), jax.ShapeDtypeStruct((nb_rows, d), F32)),
        grid=(steps_a + steps_b,),
        in_specs=[
            spec_a,
            spec_b,
            _const_spec((1, d)),
            _layer_spec(wgu.shape, layer),
            _layer_spec(wd.shape, layer),
            _const_spec((1, d)),
        ],
        out_specs=(spec_a, spec_b),
        scratch_shapes=[pltpu.VMEM((tm, d), _MXU_DTYPE), pltpu.VMEM((tm, d), F32)],
        compiler_params=pltpu.CompilerParams(dimension_semantics=("arbitrary",),
                                             vmem_limit_bytes=_VMEM_LIMIT_BYTES),
        name="ffn",
    )(xa, xb, g.reshape(1, d), wgu, wd, fg.reshape(1, d))


def _mem_kv_kernel(x_ref, g_ref, w_ref, k_ref, v_ref, ks_ref, vs_ref):
    d = x_ref.shape[-1]
    n_heads, hd = ks_ref.shape[-2:]
    kv = _mm(_rms(x_ref[...], g_ref[...]), w_ref[...])
    k_ref[...] = kv[:, :d]
    v_ref[...] = kv[:, d:]
    for h in range(n_heads):
        ks_ref[:, h, :] = kv[:, h * hd:(h + 1) * hd]
        vs_ref[:, h, :] = kv[:, d + h * hd:d + (h + 1) * hd]


def _mem_kv(mem, g, w, n_heads):
    b, n_mem, d = mem.shape
    depth = w.shape[0]
    flat_spec = pl.BlockSpec((None, None, n_mem, d), lambda l, i: (l, i, 0, 0))
    split_spec = pl.BlockSpec((None, None, n_mem, n_heads, d // n_heads), lambda l, i: (l, i, 0, 0, 0))
    flat_shape = jax.ShapeDtypeStruct((depth, b, n_mem, d), F32)
    split_shape = jax.ShapeDtypeStruct((depth, b, n_mem, n_heads, d // n_heads), F32)
    return pl.pallas_call(
        _mem_kv_kernel,
        out_shape=(flat_shape, flat_shape, split_shape, split_shape),
        grid=(depth, b),
        in_specs=[pl.BlockSpec((None, n_mem, d), lambda l, i: (i, 0, 0)),
                  pl.BlockSpec((None, 1, d), lambda l, i: (l, 0, 0)),
                  pl.BlockSpec((None, d, 2 * d), lambda l, i: (l, 0, 0))],
        out_specs=(flat_spec, flat_spec, split_spec, split_spec),
        compiler_params=pltpu.CompilerParams(dimension_semantics=("parallel", "parallel"),
                                             vmem_limit_bytes=_VMEM_LIMIT_BYTES),
        name="mem_kv",
    )(mem, g.reshape(depth, 1, d), w)


def _xattn_kernel(x_ref, g_ref, wq_ref, wo_ref, mk_ref, mv_ref, o_ref, q_scr, a_scr, *cache_scr,
                  nb, tt, n_heads, layer, head_split):
    d = x_ref.shape[-1]
    hd = d // n_heads
    rows = nb * tt
    scale = hd ** -0.5

    if head_split:
        kbuf, vbuf, sem = cache_scr
        i = pl.program_id(0)
        slot = i % 2

        def slab_copies(step, to_slot):
            seqs = pl.ds(step * nb, nb)
            return [pltpu.make_async_copy(src.at[layer, seqs, :, h, :], buf.at[to_slot, h], sem.at[to_slot, kv, h])
                    for kv, (src, buf) in enumerate(((mk_ref, kbuf), (mv_ref, vbuf))) for h in range(n_heads)]

        @pl.when(i == 0)
        def _():
            for cp in slab_copies(0, 0):
                cp.start()

        @pl.when(i + 1 < pl.num_programs(0))
        def _():
            for cp in slab_copies(i + 1, 1 - slot):
                cp.start()

        keys = lambda s, h: kbuf[slot, h, s]
        vals = lambda s, h: vbuf[slot, h, s]
    else:
        keys = lambda s, h: mk_ref[s, :, h * hd:(h + 1) * hd]
        vals = lambda s, h: mv_ref[s, :, h * hd:(h + 1) * hd]

    x = x_ref[...].reshape(rows, d)
    q_scr[...] = _mm(_rms(x, g_ref[...]), wq_ref[...])

    if head_split:
        for cp in slab_copies(i, slot):
            cp.wait()

    spp = math.gcd(nb, 2)

    def seq_body(sg, carry):
        chains = [(sg * spp + si, pl.ds(pl.multiple_of((sg * spp + si) * tt, _SUBLANES), tt),
                   h, slice(h * hd, (h + 1) * hd)) for si in range(spp) for h in range(n_heads)]
        scs = [_mm_nt(q_scr[rs, hs], keys(s, h)) * scale for s, rs, h, hs in chains]
        es = [jnp.exp(sc - jnp.max(sc, axis=-1, keepdims=True)) for sc in scs]
        prs = [e / jnp.sum(e, axis=-1, keepdims=True) for e in es]
        for pr, (s, rs, h, hs) in zip(prs, chains):
            a_scr[rs, hs] = _mm(pr, vals(s, h))
        return carry

    lax.fori_loop(0, nb // spp, seq_body, 0)
    o_ref[...] = (x + _mm(a_scr[...], wo_ref[...])).reshape(nb, tt, d)


def _xattn(x, g, wq, wo, mk, mv, layer, nb, tt):
    b, t, d = x.shape
    n_mem = mk.shape[2]
    rows = nb * tt
    head_split = mk.ndim == 5
    if head_split:
        assert t == tt and mk.shape[3] == X_HEADS
        mem_spec = pl.BlockSpec(memory_space=pl.ANY)
        slab = pltpu.VMEM((2, X_HEADS, nb, n_mem, d // X_HEADS), F32)
        cache_scr = [slab, slab, pltpu.SemaphoreType.DMA((2, 2, X_HEADS))]
        semantics = ("arbitrary", "arbitrary")
    else:
        mem_spec = pl.BlockSpec((None, nb, n_mem, d), lambda i, j: (layer, i, 0, 0))
        cache_scr = []
        semantics = ("parallel", "parallel")
    return pl.pallas_call(
        functools.partial(_xattn_kernel, nb=nb, tt=tt, n_heads=X_HEADS, layer=layer, head_split=head_split),
        out_shape=jax.ShapeDtypeStruct((b, t, d), F32),
        grid=(b // nb, t // tt),
        in_specs=[
            pl.BlockSpec((nb, tt, d), lambda i, j: (i, j, 0)),
            _const_spec((1, d)),
            _layer_spec(wq.shape, layer),
            _layer_spec(wo.shape, layer),
            mem_spec,
            mem_spec,
        ],
        out_specs=pl.BlockSpec((nb, tt, d), lambda i, j: (i, j, 0)),
        scratch_shapes=[pltpu.VMEM((rows, d), F32), pltpu.VMEM((rows, d), F32)] + cache_scr,
        compiler_params=pltpu.CompilerParams(dimension_semantics=semantics,
                                             vmem_limit_bytes=_VMEM_LIMIT_BYTES),
        name="xattn",
    )(x, g.reshape(1, d), wq, wo, mk, mv)


def _even_kernel(x_ref, g_ref, win_ref, wout_ref, cw_ref, lb_ref, gn_ref, conv_in_ref, hg_in_ref,
                 xo_ref, conv_out_ref, hg_out_ref,
                 p_scr, y_scr, ext_scr, b_scr, qg_scr, kk_scr, o_scr, st_scr, *, nb, tt, blk, one_tile):
    j = pl.program_id(1)
    nj = pl.num_programs(1)
    d = x_ref.shape[-1]
    rows = nb * tt
    aw = A_WIDTH
    hd = B_DIM

    def transpose_states(src, dst):
        if one_tile:
            for s in range(nb):
                for h in range(B_HEADS):
                    dst[s, h] = src[s, h].T
        else:
            def body(s, carry):
                for h in range(B_HEADS):
                    dst[s, h] = src[s, h].T
                return carry

            lax.fori_loop(0, nb, body, 0)

    def load_state():
        conv_out_ref[...] = conv_in_ref[...]
        transpose_states(hg_in_ref, st_scr)

    if one_tile:
        load_state()
    else:
        pl.when(j == 0)(load_state)

    x = x_ref[...].reshape(rows, d)
    h_in = _rms(x, g_ref[...]).astype(_MXU_DTYPE)
    n_parts = 2 if (nb % 2 == 0 and rows >= 2 * _MXU_WIDTH) else 1
    pnb = nb // n_parts
    prow = pnb * tt
    lb = lb_ref[...]
    cw = cw_ref[...]
    rowi = lax.broadcasted_iota(jnp.int32, (1, _SUBLANES, hd), 1)
    sp = math.gcd(pnb, _SEQS_PER_PASS * _SUBLANES // blk)
    chains = sp * B_HEADS
    col_blocks = lambda width: [(c, min(c + _MXU_WIDTH, width)) for c in range(0, width, _MXU_WIDTH)]

    def project_steps(part):
        r = slice(part * prow, (part + 1) * prow)

        def step(c, e):
            p_scr[r, c:e] = jnp.dot(h_in[r], win_ref[:, c:e], preferred_element_type=F32)

        return [functools.partial(step, c, e) for c, e in col_blocks(EVEN_IN)]

    def prepare(part):
        r = slice(part * prow, (part + 1) * prow)
        sq = slice(part * pnb, (part + 1) * pnb)
        u = p_scr[r, 2 * aw:3 * aw] * p_scr[r, 0:aw]
        ext_scr[sq, _SUBLANES:_SUBLANES + tt, :] = u.reshape(pnb, tt, aw)
        ext_scr[sq, _SUBLANES - 2:_SUBLANES, :] = conv_out_ref[sq]
        conv = (cw[0:1] * ext_scr[sq, _SUBLANES - 2:_SUBLANES - 2 + tt, :]
                + cw[1:2] * ext_scr[sq, _SUBLANES - 1:_SUBLANES - 1 + tt, :]
                + cw[2:3] * ext_scr[sq, _SUBLANES:_SUBLANES + tt, :])
        y_scr[r, 0:aw] = p_scr[r, aw:2 * aw] * conv.reshape(prow, aw)
        conv_out_ref[sq] = ext_scr[sq, _SUBLANES + tt - 2:_SUBLANES + tt, :]
        f = lb + (1.0 - lb) * _sigmoid(p_scr[r, 4 * aw:5 * aw])
        kk_scr[r, :] = 1.0 - f
        b_scr[r, :] = _mm_exact_lhs(_block_tri(prow, blk).astype(F32), jnp.log2(f))
        qg_scr[r, :] = _silu(p_scr[r, 3 * aw:4 * aw])

    def time_block(s0, k):
        rows_of = [slice((s0 + si) * tt + k * blk, (s0 + si) * tt + (k + 1) * blk) for si in range(sp)]
        stack = lambda ref, col0: jnp.stack([ref[rs, col0 + h * hd:col0 + (h + 1) * hd]
                                             for rs in rows_of for h in range(B_HEADS)])
        bb, qg, kkb = stack(b_scr, 0), stack(qg_scr, 0), stack(kk_scr, 0)
        vb = stack(p_scr, 5 * aw)
        st = st_scr[s0:s0 + sp].reshape(chains, hd, hd)
        blast = bb[:, blk - 1:blk, :]
        intra = []
        for p0 in range(0, blk, _SUBLANES):
            bi, qi = bb[:, p0:p0 + _SUBLANES, :], qg[:, p0:p0 + _SUBLANES, :]
            oi = jnp.zeros_like(bi)
            for jj in range(p0 + _SUBLANES):
                diff = bi - bb[:, jj:jj + 1, :]
                if jj >= p0:
                    diff = jnp.where(rowi >= jj - p0, diff, -jnp.inf)
                att = jnp.sum(qi * kkb[:, jj:jj + 1, :] * jnp.exp2(diff), axis=-1, keepdims=True)
                oi = oi + att * vb[:, jj:jj + 1, :]
            intra.append(oi)
        o = _bmm_nt(qg * jnp.exp2(bb), st) + (intra[0] if len(intra) == 1 else jnp.concatenate(intra, axis=1))
        kd = kkb * jnp.exp2(blast - bb)
        upd = jnp.stack([_mm_tn(vb[c], kd[c]) for c in range(chains)])
        st_scr[s0:s0 + sp] = (st * jnp.exp2(blast) + upd).reshape(sp, B_HEADS, hd, hd)
        for si, rs in enumerate(rows_of):
            for h in range(B_HEADS):
                o_scr[rs, h * hd:(h + 1) * hd] = o[si * B_HEADS + h]

    def recurrence_steps(part):
        return [functools.partial(time_block, part * pnb + sg * sp, k)
                for sg in range(pnb // sp) for k in range(tt // blk)]

    def finish(part):
        r = slice(part * prow, (part + 1) * prow)
        for h in range(B_HEADS):
            hs = slice(h * hd, (h + 1) * hd)
            o = o_scr[r, hs]
            on = o * lax.rsqrt(jnp.mean(o * o, axis=-1, keepdims=True) + NORM_EPS) * gn_ref[:, hs]
            y_scr[r, aw + h * hd:aw + (h + 1) * hd] = on * _silu(p_scr[r, 6 * aw + h * hd:6 * aw + (h + 1) * hd])

    def outproj_steps(part):
        r = slice(part * prow, (part + 1) * prow)
        sq = slice(part * pnb, (part + 1) * pnb)

        def step(c, e):
            y = y_scr[r, :].astype(_MXU_DTYPE)
            res = x[r, c:e] + jnp.dot(y, wout_ref[:, c:e], preferred_element_type=F32)
            xo_ref[sq, :, c:e] = res.reshape(pnb, tt, e - c)

        return [functools.partial(step, c, e) for c, e in col_blocks(d)]

    def run_interleaved(main, other):
        done = 0
        for i, step in enumerate(main):
            upto = (i + 1) * len(other) // len(main)
            for o_step in other[done:upto]:
                o_step()
            done = upto
            step()

    for step in project_steps(0):
        step()
    prepare(0)
    for part in range(n_parts):
        last = part == n_parts - 1
        other = [s for p in range(n_parts - 1) for s in outproj_steps(p)] if last else project_steps(part + 1)
        run_interleaved(recurrence_steps(part), other if n_parts > 1 else [])
        finish(part)
        if not last:
            prepare(part + 1)
    for step in outproj_steps(n_parts - 1):
        step()

    if one_tile:
        transpose_states(st_scr, hg_out_ref)
    else:
        pl.when(j == nj - 1)(lambda: transpose_states(st_scr, hg_out_ref))


def _even_layer(x, g, win, wout, cw, lb, gn, conv_in, hg_in, nb, tt, blk):
    b, t, d = x.shape
    rows = nb * tt
    return pl.pallas_call(
        functools.partial(_even_kernel, nb=nb, tt=tt, blk=blk, one_tile=(t == tt)),
        out_shape=(jax.ShapeDtypeStruct((b, t, d), F32),
                   jax.ShapeDtypeStruct(conv_in.shape, F32),
                   jax.ShapeDtypeStruct(hg_in.shape, F32)),
        grid=(b // nb, t // tt),
        in_specs=[
            pl.BlockSpec((nb, tt, d), lambda i, j: (i, j, 0)),
            _const_spec((1, d)),
            _const_spec(win.shape),
            _const_spec(wout.shape),
            _const_spec(cw.shape),
            _const_spec((1, A_WIDTH)),
            _const_spec((1, B_HEADS * B_DIM)),
            pl.BlockSpec((nb,) + conv_in.shape[1:], lambda i, j: (i, 0, 0)),
            pl.BlockSpec((nb,) + hg_in.shape[1:], lambda i, j: (i, 0, 0, 0)),
        ],
        out_specs=(
            pl.BlockSpec((nb, tt, d), lambda i, j: (i, j, 0)),
            pl.BlockSpec((nb,) + conv_in.shape[1:], lambda i, j: (i, 0, 0)),
            pl.BlockSpec((nb,) + hg_in.shape[1:], lambda i, j: (i, 0, 0, 0)),
        ),
        scratch_shapes=[
            pltpu.VMEM((rows, EVEN_IN), F32),
            pltpu.VMEM((rows, 2 * A_WIDTH), F32),
            pltpu.VMEM((nb, tt + _SUBLANES, A_WIDTH), F32),
            pltpu.VMEM((rows, A_WIDTH), F32),
            pltpu.VMEM((rows, A_WIDTH), F32),
            pltpu.VMEM((rows, A_WIDTH), F32),
            pltpu.VMEM((rows, A_WIDTH), F32),
            pltpu.VMEM((nb, B_HEADS, B_DIM, B_DIM), F32),
        ],
        compiler_params=pltpu.CompilerParams(dimension_semantics=("parallel", "arbitrary"),
                                             vmem_limit_bytes=_VMEM_LIMIT_BYTES),
        name="even_layer",
    )(x, g.reshape(1, d), win, wout, cw, lb.reshape(1, -1), gn.reshape(1, -1), conv_in, hg_in)


def _odd_kernel(x_ref, g_ref, win_ref, wout_ref, cos_ref, sin_ref, qdec_ref, kdec_ref, cdec_ref, dmask_ref,
                rgn_ref, mu_ref, w0_ref, w2_ref, a0_ref, a2_ref, g2_ref, kk_ref, ka_ref, rk_ref, lng_ref,
                lnb_ref, ret_in_ref, rw_in_ref, sh_in_ref,
                xo_ref, ret_out_ref, rw_out_ref, sh_out_ref,
                p_scr, y_scr, ext_scr, qr_scr, kr_scr, ir_scr,
                rt_scr, at_scr, bt_scr, kt_scr, vd_scr, cum_scr, od_scr, ah_scr, rh_scr, u0_scr, o0_scr,
                *, nb, tt, ch, gl, one_tile):
    j = pl.program_id(1)
    d = x_ref.shape[-1]
    rows = nb * tt
    cw = C_HEADS * C_DIM
    hd = C_DIM
    nd = D_HDIM
    n_groups = rows // gl
    cpg = gl // ch
    cps = tt // ch

    def load_state():
        ret_out_ref[...] = ret_in_ref[...]
        rw_out_ref[...] = rw_in_ref[...]
        sh_out_ref[...] = sh_in_ref[...]

    if one_tile and cps == 1:
        sh_out_ref[...] = sh_in_ref[...]
        ret_src, rw_src = ret_in_ref, rw_in_ref
    else:
        if one_tile:
            load_state()
        else:
            pl.when(j == 0)(load_state)
        ret_src, rw_src = ret_out_ref, rw_out_ref

    x = x_ref[...].reshape(rows, d)
    h_in = _rms(x, g_ref[...]).astype(_MXU_DTYPE)
    gpi = math.gcd(n_groups, _SEQS_PER_PASS)
    sp = math.gcd(nb, _SEQS_PER_PASS)

    def chain_rows(s0, k):
        return [pl.ds(pl.multiple_of((s0 + si) * tt + k * ch, _SUBLANES), ch) for si in range(sp)]

    _project(h_in, win_ref, p_scr, C_IN, ODD_IN)

    pd = p_scr[:, C_IN:ODD_IN]
    ext_scr[:, _SUBLANES:_SUBLANES + tt, :] = pd.reshape(nb, tt, D_IN)
    ext_scr[:, _SUBLANES - 1:_SUBLANES, :] = sh_out_ref[...]
    prev = ext_scr[:, _SUBLANES - 1:_SUBLANES - 1 + tt, :].reshape(rows, D_IN)
    sh_out_ref[...] = ext_scr[:, _SUBLANES + tt - 1:_SUBLANES + tt, :]
    pm = pd + mu_ref[...] * (prev - pd)
    dw = D_WIDTH
    r = pm[:, 0:dw]
    kd = pm[:, dw:2 * dw]
    vd = pm[:, 2 * dw:3 * dw]
    w_dn = pm[:, 3 * dw:3 * dw + 64]
    a_dn = pm[:, 3 * dw + 64:3 * dw + 128]
    g_dn = pm[:, 3 * dw + 128:3 * dw + 256]
    logdec = -math.exp(-0.5) * _sigmoid(w0_ref[...] + _mm(jnp.tanh(w_dn), w2_ref[...]))
    a = _sigmoid(a0_ref[...] + _mm(a_dn, a2_ref[...]))
    gate = _mm(_sigmoid(g_dn), g2_ref[...])
    kk0 = kd * kk_ref[...]
    kk = kk0 / jnp.maximum(jnp.sqrt(_head_sums(kk0 * kk0, nd)), 1e-12)
    kd2 = kd * (1.0 + (a - 1.0) * ka_ref[...])
    cum = _mm_exact_lhs(_block_tri(rows, ch).astype(F32), logdec)
    cum_scr[...] = cum
    rt_scr[...] = r * jnp.exp(cum)
    at_scr[...] = -kk * jnp.exp(cum - logdec)
    e_neg = jnp.exp(-cum)
    bt_scr[...] = kk * a * e_neg
    kt_scr[...] = kd2 * e_neg
    vd_scr[...] = vd
    bonus = _head_sums(r * kd2 * rk_ref[...], nd) * vd

    strict = _block_tri(gl, ch, strict=True)
    incl = _block_tri(gl, ch)
    eye = (lax.broadcasted_iota(jnp.int32, (gl, gl), 0) == lax.broadcasted_iota(jnp.int32, (gl, gl), 1)).astype(F32)
    n_dbl = int(math.log2(ch))

    def rw_prepare(it, carry):
        def heads(ref):
            return jnp.stack([ref[pl.ds(pl.multiple_of((it * gpi + g) * gl, _SUBLANES), gl), h * nd:(h + 1) * nd]
                              for g in range(gpi) for h in range(D_HEADS)])

        at, rt, bt, kt, vv = (heads(ref) for ref in (at_scr, rt_scr, bt_scr, kt_scr, vd_scr))
        sc = _bmm_nt(jnp.concatenate([at, rt], axis=1), jnp.concatenate([bt, kt], axis=1))
        n_ab = jnp.where(strict, sc[:, 0:gl, 0:gl], 0.0)
        a_ak = jnp.where(strict, sc[:, 0:gl, gl:2 * gl], 0.0)
        r_rb = jnp.where(incl, sc[:, gl:2 * gl, 0:gl], 0.0)
        r_rk = jnp.where(incl, sc[:, gl:2 * gl, gl:2 * gl], 0.0)
        xinv = eye + n_ab
        pw = n_ab
        for _ in range(n_dbl - 1):
            pw = _bmm(pw, pw)
            xinv = xinv + _bmm(pw, xinv)
        resid = eye - (xinv - _bmm_hi(n_ab, xinv))
        xinv = xinv + _bmm(xinv, resid)
        xa = _bmm(xinv, jnp.concatenate([at, _bmm(a_ak, vv)], axis=2))
        ah = xa[:, :, 0:nd]
        u0 = xa[:, :, nd:2 * nd]
        ra = _bmm(r_rb, xa)
        rh = rt + ra[:, :, 0:nd]
        o0 = ra[:, :, nd:2 * nd] + _bmm(r_rk, vv)
        for g in range(gpi):
            part = slice(g * D_HEADS, (g + 1) * D_HEADS)
            ah_scr[it * gpi + g] = ah[part]
            u0_scr[it * gpi + g] = u0[part]
            rh_scr[it * gpi + g] = rh[part]
            o0_scr[it * gpi + g] = o0[part]
        return carry

    lax.fori_loop(0, n_groups // gpi, rw_prepare, 0)

    _project(h_in, win_ref, p_scr, 0, C_IN)
    cos = cos_ref[0]
    sin = sin_ref[0]
    for h in range(C_HEADS):
        hs = slice(h * hd, (h + 1) * hd)
        qh = p_scr[:, h * hd:(h + 1) * hd]
        kh = p_scr[:, cw + h * hd:cw + (h + 1) * hd]
        qr_scr[:, hs] = qh * cos + pltpu.roll(qh, hd // 2, axis=1) * sin
        kr_scr[:, hs] = (kh * cos + pltpu.roll(kh, hd // 2, axis=1) * sin) * (C_DIM ** -0.5)

    def ret_intra(it, carry):
        rows_g = [pl.ds(pl.multiple_of((it * gpi + g) * gl, _SUBLANES), gl) for g in range(gpi)]
        stack = lambda ref, col0: jnp.stack([ref[rg, col0 + h * hd:col0 + (h + 1) * hd]
                                             for rg in rows_g for h in range(C_HEADS)])
        dm = dmask_ref[...]
        dm = dm if gpi == 1 else jnp.concatenate([dm] * gpi, axis=0)
        o = _bmm(_bmm_nt(stack(qr_scr, 0), stack(kr_scr, 0)) * dm, stack(p_scr, 2 * cw))
        for g, rg in enumerate(rows_g):
            for h in range(C_HEADS):
                ir_scr[rg, h * hd:(h + 1) * hd] = o[g * C_HEADS + h]
        return carry

    lax.fori_loop(0, n_groups // gpi, ret_intra, 0)

    per_chain = lambda ref, r: jnp.stack([ref[r, h * hd:(h + 1) * hd] for _ in range(sp) for h in range(C_HEADS)])
    qdec_c = per_chain(qdec_ref, slice(0, ch))
    kdec_c = per_chain(kdec_ref, slice(0, ch))
    cdec_c = per_chain(cdec_ref, slice(0, 1))

    def ret_seq_group(sg, carry):
        s0 = sg * sp

        def ret_chunk(k, c2):
            rows_c = chain_rows(s0, k)
            stack = lambda ref, col0: jnp.stack([ref[rc, col0 + h * hd:col0 + (h + 1) * hd]
                                                 for rc in rows_c for h in range(C_HEADS)])
            qc, kc, vc = stack(qr_scr, 0), stack(kr_scr, 0), stack(p_scr, 2 * cw)
            st = ret_src[pl.ds(s0, sp)].reshape(sp * C_HEADS, hd, hd)
            inter = _bmm(qc, st) * qdec_c
            kcd = kc * kdec_c
            upd = jnp.stack([_mm_tn(kcd[c], vc[c]) for c in range(sp * C_HEADS)])
            ret_out_ref[pl.ds(s0, sp)] = (cdec_c * st + upd).reshape(sp, C_HEADS, hd, hd)
            for si, rc in enumerate(rows_c):
                for h in range(C_HEADS):
                    ir_scr[rc, h * hd:(h + 1) * hd] += inter[si * C_HEADS + h]
            return c2

        lax.fori_loop(0, cps, ret_chunk, 0)
        return carry

    lax.fori_loop(0, nb // sp, ret_seq_group, 0)

    for h in range(C_HEADS):
        hs = slice(h * hd, (h + 1) * hd)
        o = ir_scr[:, hs]
        on = o * lax.rsqrt(jnp.mean(o * o, axis=-1, keepdims=True) + NORM_EPS) * rgn_ref[:, hs]
        y_scr[:, hs] = on * _silu(p_scr[:, 3 * cw + h * hd:3 * cw + (h + 1) * hd])

    n_chains = sp * D_HEADS

    def rw_seq_group(sg, carry):
        s0 = sg * sp

        def rw_chunk(k, c2):
            rows_c = chain_rows(s0, k)
            where = []
            for si in range(sp):
                cidx = (s0 + si) * cps + k
                where.append((cidx // cpg, pl.ds(pl.multiple_of((cidx % cpg) * ch, _SUBLANES), ch)))
            prep = lambda ref: jnp.stack([ref[gi, h, ls, :] for gi, ls in where for h in range(D_HEADS)])
            rowsl = lambda ref: jnp.stack([ref[rc, h * nd:(h + 1) * nd] for rc in rows_c for h in range(D_HEADS)])
            st = rw_src[pl.ds(s0, sp)].reshape(n_chains, nd, nd)
            ars = _bmm_nt(jnp.concatenate([prep(ah_scr), prep(rh_scr)], axis=1), st)
            uv = jnp.concatenate([ars[:, 0:ch, :] + prep(u0_scr), rowsl(vd_scr)], axis=1)
            bk = jnp.concatenate([rowsl(bt_scr), rowsl(kt_scr)], axis=1)
            upd = jnp.stack([_mm_tn(uv[c], bk[c]) for c in range(n_chains)])
            glast = jnp.stack([cum_scr[pl.ds(pl.multiple_of((s0 + si) * tt + (k + 1) * ch - _SUBLANES, _SUBLANES),
                                             _SUBLANES), h * nd:(h + 1) * nd][_SUBLANES - 1:_SUBLANES]
                               for si in range(sp) for h in range(D_HEADS)])
            rw_out_ref[pl.ds(s0, sp)] = ((st + upd) * jnp.exp(glast)).reshape(sp, D_HEADS, nd, nd)
            o = ars[:, ch:2 * ch, :] + prep(o0_scr)
            for si, rc in enumerate(rows_c):
                for h in range(D_HEADS):
                    od_scr[rc, h * nd:(h + 1) * nd] = o[si * D_HEADS + h]
            return c2

        lax.fori_loop(0, cps, rw_chunk, 0)
        return carry

    lax.fori_loop(0, nb // sp, rw_seq_group, 0)

    o_d = od_scr[...]
    mean = _head_sums(o_d, nd) * (1.0 / nd)
    xc = o_d - mean
    var = _head_sums(xc * xc, nd) * (1.0 / nd)
    on = xc * lax.rsqrt(var + RWKV_GN_EPS) * lng_ref[...] + lnb_ref[...]
    y_scr[:, cw:cw + dw] = (on + bonus) * gate

    xo_ref[...] = (x + _mm(y_scr[...], wout_ref[...])).reshape(nb, tt, d)


def _odd_layer(x, g, win, wout, tabs, rgn, rw, ret_in, rw_in, sh_in, nb, tt, ch, gl):
    b, t, d = x.shape
    rows = nb * tt
    cos, sin, qdec, kdec, cdec, dmask = tabs
    sh3 = sh_in.reshape(b, 1, D_IN)
    vecs = [rw[k].reshape(1, -1) for k in ("mu", "w0")] + [rw["w2"]] + [rw["a0"].reshape(1, -1), rw["a2"], rw["g2"]] + \
           [rw[k].reshape(1, -1) for k in ("k_k", "k_a", "r_k", "lnx_g", "lnx_b")]
    w512 = pltpu.VMEM((rows, D_WIDTH), F32)
    outs = pl.pallas_call(
        functools.partial(_odd_kernel, nb=nb, tt=tt, ch=ch, gl=gl, one_tile=(t == tt)),
        out_shape=(jax.ShapeDtypeStruct((b, t, d), F32),
                   jax.ShapeDtypeStruct(ret_in.shape, F32),
                   jax.ShapeDtypeStruct(rw_in.shape, F32),
                   jax.ShapeDtypeStruct(sh3.shape, F32)),
        grid=(b // nb, t // tt),
        in_specs=[
            pl.BlockSpec((nb, tt, d), lambda i, j: (i, j, 0)),
            _const_spec((1, d)),
            _const_spec(win.shape),
            _const_spec(wout.shape),
            pl.BlockSpec((1, rows, C_DIM), lambda i, j: (j, 0, 0)),
            pl.BlockSpec((1, rows, C_DIM), lambda i, j: (j, 0, 0)),
            _const_spec(qdec.shape),
            _const_spec(kdec.shape),
            _const_spec(cdec.shape),
            _const_spec(dmask.shape),
            _const_spec((1, C_HEADS * C_DIM)),
        ] + [_const_spec(v.shape) for v in vecs] + [
            pl.BlockSpec((nb,) + ret_in.shape[1:], lambda i, j: (i, 0, 0, 0)),
            pl.BlockSpec((nb,) + rw_in.shape[1:], lambda i, j: (i, 0, 0, 0)),
            pl.BlockSpec((nb, 1, D_IN), lambda i, j: (i, 0, 0)),
        ],
        out_specs=(
            pl.BlockSpec((nb, tt, d), lambda i, j: (i, j, 0)),
            pl.BlockSpec((nb,) + ret_in.shape[1:], lambda i, j: (i, 0, 0, 0)),
            pl.BlockSpec((nb,) + rw_in.shape[1:], lambda i, j: (i, 0, 0, 0)),
            pl.BlockSpec((nb, 1, D_IN), lambda i, j: (i, 0, 0)),
        ),
        scratch_shapes=[
            pltpu.VMEM((rows, ODD_IN), F32),
            pltpu.VMEM((rows, C_HEADS * C_DIM + D_WIDTH), F32),
            pltpu.VMEM((nb, tt + _SUBLANES, D_IN), F32),
        ] + [w512] * 10 + [pltpu.VMEM((rows // gl, D_HEADS, gl, D_HDIM), F32)] * 4,
        compiler_params=pltpu.CompilerParams(dimension_semantics=("parallel", "arbitrary"),
                                             vmem_limit_bytes=_VMEM_LIMIT_BYTES),
        name="odd_layer",
    )(x, g.reshape(1, d), win, wout, cos, sin, qdec, kdec, cdec, dmask, rgn.reshape(1, -1), *vecs,
      ret_in, rw_in, sh3)
    xo, ret_o, rw_o, sh_o = outs
    return xo, ret_o, rw_o, sh_o.reshape(b, D_IN)


def _odd_tables(pos0, t, nb, tt, ch, gl):
    half = C_DIM // 2
    inv = ROPE_BASE ** (-jnp.arange(half, dtype=F32) / half)
    pos = pos0 + jnp.arange(t, dtype=jnp.int32)
    ang = pos.astype(F32)[:, None] * inv[None, :]
    cos = jnp.cos(ang)
    sin = jnp.sin(ang)
    cosf = jnp.concatenate([cos, cos], axis=-1).reshape(t // tt, tt, C_DIM)
    sinf = jnp.concatenate([-sin, sin], axis=-1).reshape(t // tt, tt, C_DIM)
    cosf = jnp.tile(cosf, (1, nb, 1))
    sinf = jnp.tile(sinf, (1, nb, 1))
    lg = jnp.log1p(-jnp.exp2(-5.0 - jnp.arange(C_HEADS, dtype=F32)))[:, None]
    idx = jnp.arange(ch, dtype=F32)
    q_dec = jnp.exp(lg * (idx + 1.0))
    k_dec = jnp.exp(lg * (ch - 1.0 - idx))
    rel = idx[:, None] - idx[None, :]
    dm = jnp.where(rel >= 0, jnp.exp(lg[:, :, None] * jnp.maximum(rel, 0.0)), 0.0)
    c_dec = jnp.exp(lg * ch)
    cpg = gl // ch
    expand = lambda v: jnp.repeat(jnp.tile(v.T, (cpg, 1)), C_DIM, axis=1)
    qdec = expand(q_dec)
    kdec = expand(k_dec)
    cdec = jnp.repeat(c_dec.T, C_DIM, axis=1)
    gi = jnp.arange(gl)
    same = (gi[:, None] // ch) == (gi[None, :] // ch)
    dmask = jnp.where(same[None], jnp.tile(dm, (1, cpg, cpg)), 0.0)
    return cosf, sinf, qdec, kdec, cdec, dmask


def _prep_weights(ffn1_w_gu, ffn1_w_down, ffn2_w_gu, ffn2_w_down, even_w_in, even_w_out, odd_w_in, odd_w_out,
                  xattn_wq, xattn_wkv, xattn_wo):
    bf = lambda w: w.astype(_MXU_DTYPE)
    return dict(ffn1=(bf(ffn1_w_gu), bf(ffn1_w_down)), ffn2=(bf(ffn2_w_gu), bf(ffn2_w_down)),
                even_in=bf(even_w_in), even_out=bf(even_w_out), odd_in=bf(odd_w_in), odd_out=bf(odd_w_out),
                wq=bf(xattn_wq), wkv=bf(xattn_wkv), wo=bf(xattn_wo))


def _trunks(groups, W, P):
    depth = P["ffn1_norm"].shape[0]
    d = groups[0]["x"].shape[-1]
    lb_all = jnp.cumsum(jax.nn.softmax(P["hgrn_lb"].astype(F32), axis=0), axis=0)
    xs = [g["x"] for g in groups]
    new = [dict(conv=[], hg=[], ret=[], rw=[], sh=[]) for _ in groups]
    tm = min(g["cfg"]["tm"] for g in groups)
    tf = groups[0]["cfg"]["tf"]

    def ffn_both(xs, norm, weights, l, fin):
        flat = [x.reshape(-1, d) for x in xs]
        out = _ffn(flat[0], flat[1], norm, weights[0], weights[1], l, fin, tm, tf)
        return [o.reshape(x.shape) for o, x in zip(out, xs)]

    for l in range(depth):
        jl = l // 2
        xs = ffn_both(xs, P["ffn1_norm"][l], W["ffn1"], l, None)
        for gi, g in enumerate(groups):
            cfg, x = g["cfg"], xs[gi]
            if l % 2 == 0:
                x, cb, sh = _even_layer(x, P["mix_norm"][l], W["even_in"][jl], W["even_out"][jl], P["conv_w"][jl],
                                        lb_all[jl], P["hgrn_gnorm"][jl], g["conv"][jl], g["hg"][jl],
                                        cfg["enb"], cfg["ett"], cfg["blk"])
                new[gi]["conv"].append(cb)
                new[gi]["hg"].append(sh)
            else:
                tabs = _odd_tables(g["pos0"], x.shape[1], cfg["nb"], cfg["tt"], cfg["ch"], cfg["gl"])
                rw = {k: P["rwkv_" + k][jl] for k in ("mu", "w0", "w2", "a0", "a2", "g2", "k_k", "k_a", "r_k",
                                                      "lnx_g", "lnx_b")}
                x, sr, sw, ss = _odd_layer(x, P["mix_norm"][l], W["odd_in"][jl], W["odd_out"][jl], tabs,
                                           P["ret_gnorm"][jl], rw, g["ret"][jl], g["rw"][jl], g["sh"][jl],
                                           cfg["nb"], cfg["tt"], cfg["ch"], cfg["gl"])
                new[gi]["ret"].append(sr)
                new[gi]["rw"].append(sw)
                new[gi]["sh"].append(ss)
            xs[gi] = _xattn(x, P["xattn_norm"][l], W["wq"], W["wo"], g["mem_k"], g["mem_v"], l,
                            cfg["xnb"], cfg["xtt"])
        xs = ffn_both(xs, P["ffn2_norm"][l], W["ffn2"], l, P["final_norm"] if l == depth - 1 else None)
    return [(x,) + tuple(jnp.stack(n[k]) for k in ("conv", "hg", "ret", "rw", "sh")) for x, n in zip(xs, new)]


_ROW_TILE = 2 * _MXU_WIDTH
_WIDE_ROW_TILE = 2 * _ROW_TILE
_ODD_ROW_TILE = _MXU_WIDTH
_CHUNK = 64
_HGRN_BLOCK = 2 * _SUBLANES
_SHORT_ROWS = 128
_XATTN_SHORT_SEQS = 8


def _configs(b, t):
    if t >= _ROW_TILE // 2:
        nb = math.gcd(b, _SEQS_PER_PASS)
        enb = math.gcd(b, 2 * _SEQS_PER_PASS)
        return dict(tm=_WIDE_ROW_TILE, tf=_MXU_WIDTH, nb=nb, tt=_ODD_ROW_TILE // nb, enb=enb, ett=_ROW_TILE // enb,
                    blk=_HGRN_BLOCK, ch=_CHUNK, gl=_CHUNK, xnb=1, xtt=min(t, _WIDE_ROW_TILE))
    nb = min(b, _SHORT_ROWS // t)
    return dict(tm=min(b * t, _WIDE_ROW_TILE), tf=_MXU_WIDTH, nb=nb, tt=t, enb=nb, ett=t, blk=t, ch=t, gl=nb * t,
                xnb=min(b, _XATTN_SHORT_SEQS), xtt=t)


def kernel(x_prompt, x_sample, state_conv, state_hgrn, state_ret, state_rwkv, state_shift, cache_mem_k, cache_mem_v, mem_prompt, ffn1_norm, ffn1_w_gu, ffn1_w_down, mix_norm, even_w_in, even_w_out, conv_w, hgrn_lb, hgrn_gnorm, odd_w_in, odd_w_out, ret_gnorm, rwkv_mu, rwkv_w0, rwkv_w2, rwkv_a0, rwkv_a2, rwkv_g2, rwkv_k_k, rwkv_k_a, rwkv_r_k, rwkv_lnx_g, rwkv_lnx_b, xattn_norm, mem_norm, xattn_wq, xattn_wkv, xattn_wo, ffn2_norm, ffn2_w_gu, ffn2_w_down, final_norm):
    P = dict(ffn1_norm=ffn1_norm, mix_norm=mix_norm, conv_w=conv_w, hgrn_lb=hgrn_lb, hgrn_gnorm=hgrn_gnorm,
             ret_gnorm=ret_gnorm, rwkv_mu=rwkv_mu, rwkv_w0=rwkv_w0, rwkv_w2=rwkv_w2, rwkv_a0=rwkv_a0,
             rwkv_a2=rwkv_a2, rwkv_g2=rwkv_g2, rwkv_k_k=rwkv_k_k, rwkv_k_a=rwkv_k_a,
             rwkv_r_k=rwkv_r_k.reshape(rwkv_r_k.shape[0], -1), rwkv_lnx_g=rwkv_lnx_g, rwkv_lnx_b=rwkv_lnx_b,
             xattn_norm=xattn_norm, ffn2_norm=ffn2_norm, final_norm=final_norm)
    W = _prep_weights(ffn1_w_gu, ffn1_w_down, ffn2_w_gu, ffn2_w_down, even_w_in, even_w_out, odd_w_in,
                      odd_w_out, xattn_wq, xattn_wkv, xattn_wo)
    bp, tp, d = x_prompt.shape
    bs, ts, _ = x_sample.shape
    depth = ffn1_norm.shape[0]

    mem_k_p, mem_v_p, mem_k_out, mem_v_out = _mem_kv(mem_prompt, mem_norm, W["wkv"], X_HEADS)

    z = lambda ref: jnp.zeros((ref.shape[0], bp) + ref.shape[2:], F32)
    prompt = dict(x=x_prompt, pos0=0, conv=z(state_conv), hg=z(state_hgrn), ret=z(state_ret), rw=z(state_rwkv),
                  sh=z(state_shift), mem_k=mem_k_p, mem_v=mem_v_p, cfg=_configs(bp, tp))
    sample = dict(x=x_sample, pos0=PAST_LEN, conv=state_conv, hg=state_hgrn, ret=state_ret, rw=state_rwkv,
                  sh=state_shift, mem_k=cache_mem_k, mem_v=cache_mem_v, cfg=_configs(bs, ts))
    (y_p, conv_p, hg_p, ret_p, rw_p, sh_p), (y_s, conv_s, hg_s, ret_s, rw_s, sh_s) = _trunks([prompt, sample], W, P)
    return (y_p, y_s, conv_p, hg_p, ret_p, rw_p, sh_p, mem_k_out, mem_v_out,
            conv_s, hg_s, ret_s, rw_s, sh_s)
```

```python
import functools
import math

import jax
import jax.numpy as jnp
from jax import lax
from jax.experimental import pallas as pl
from jax.experimental.pallas import tpu as pltpu

F32 = jnp.float32
_MXU_DTYPE = jnp.bfloat16

PAST_LEN = 16384
NORM_EPS = 1e-6
RWKV_GN_EPS = 64e-5
ROPE_BASE = 10000.0

_VMEM_LIMIT_BYTES = 60 * 1024 * 1024
_SUBLANES = 8
_PACKED_ROWS = 16
_MXU_WIDTH = 256
_SEQS_PER_PASS = 4

A_WIDTH = 512
B_HEADS, B_DIM = 4, 128
C_HEADS, C_DIM = 4, 128
D_HEADS, D_HDIM = 8, 64
D_WIDTH = D_HEADS * D_HDIM
EVEN_IN = 7 * 512
C_IN = 4 * 512
D_IN = 3 * 512 + 64 + 64 + 128
ODD_IN = C_IN + D_IN
X_HEADS = 4


def _mm(a, b):
    return jnp.dot(a.astype(_MXU_DTYPE), b.astype(_MXU_DTYPE), preferred_element_type=F32)


def _mm_nt(a, b):
    return lax.dot_general(a.astype(_MXU_DTYPE), b.astype(_MXU_DTYPE), (((1,), (1,)), ((), ())),
                           preferred_element_type=F32)


def _mm_tn(a, b):
    k = a.shape[0]
    if k % _PACKED_ROWS:
        pad = _PACKED_ROWS - k % _PACKED_ROWS
        a = jnp.concatenate([a, jnp.zeros((pad, a.shape[1]), a.dtype)], axis=0)
        b = jnp.concatenate([b, jnp.zeros((pad, b.shape[1]), b.dtype)], axis=0)
    return lax.dot_general(a.astype(_MXU_DTYPE), b.astype(_MXU_DTYPE), (((0,), (0,)), ((), ())),
                           preferred_element_type=F32)


def _project(h, w_ref, out_ref, col0, col1, step=_MXU_WIDTH):
    for c in range(col0, col1, step):
        e = min(c + step, col1)
        out_ref[:, c:e] = jnp.dot(h, w_ref[:, c:e], preferred_element_type=F32)


def _split2(a):
    hi = a.astype(_MXU_DTYPE)
    lo = (a - hi.astype(F32)).astype(_MXU_DTYPE)
    return hi, lo


def _mm_hi(a, b):
    ah, al = _split2(a)
    bh, bl = _split2(b)
    d = functools.partial(jnp.dot, preferred_element_type=F32)
    return d(ah, bh) + d(ah, bl) + d(al, bh)


_BATCH_NN = (((2,), (1,)), ((0,), (0,)))
_BATCH_NT = (((2,), (2,)), ((0,), (0,)))


def _bmm(a, b):
    return lax.dot_general(a.astype(_MXU_DTYPE), b.astype(_MXU_DTYPE), _BATCH_NN, preferred_element_type=F32)


def _bmm_nt(a, b):
    return lax.dot_general(a.astype(_MXU_DTYPE), b.astype(_MXU_DTYPE), _BATCH_NT, preferred_element_type=F32)


def _bmm_hi(a, b):
    ah, al = _split2(a)
    bh, bl = _split2(b)
    d = functools.partial(lax.dot_general, dimension_numbers=_BATCH_NN, preferred_element_type=F32)
    return d(ah, bh) + d(ah, bl) + d(al, bh)


def _mm_exact_lhs(m01, x):
    m = m01.astype(_MXU_DTYPE)
    x0 = x.astype(_MXU_DTYPE)
    r1 = x - x0.astype(F32)
    x1 = r1.astype(_MXU_DTYPE)
    x2 = (r1 - x1.astype(F32)).astype(_MXU_DTYPE)
    d = functools.partial(jnp.dot, preferred_element_type=F32)
    return d(m, x0) + d(m, x1) + d(m, x2)


def _mm_exact_rhs(x, m01):
    m = m01.astype(_MXU_DTYPE)
    x0 = x.astype(_MXU_DTYPE)
    r1 = x - x0.astype(F32)
    x1 = r1.astype(_MXU_DTYPE)
    x2 = (r1 - x1.astype(F32)).astype(_MXU_DTYPE)
    d = functools.partial(jnp.dot, preferred_element_type=F32)
    return d(x0, m) + d(x1, m) + d(x2, m)


def _rms(x, g):
    return x * lax.rsqrt(jnp.mean(x * x, axis=-1, keepdims=True) + NORM_EPS) * g


def _sigmoid(x):
    return 1.0 / (1.0 + jnp.exp(-x))


def _silu(x):
    return x * _sigmoid(x)


def _softplus(x):
    return jnp.maximum(x, 0.0) + jnp.log1p(jnp.exp(-jnp.abs(x)))


def _block_tri(n, blk, strict=False):
    r = lax.broadcasted_iota(jnp.int32, (n, n), 0)
    c = lax.broadcasted_iota(jnp.int32, (n, n), 1)
    same = (r // blk) == (c // blk)
    low = (c < r) if strict else (c <= r)
    return same & low


def _head_sums(x, hd):
    width = min(_MXU_WIDTH, x.shape[1])
    r = lax.broadcasted_iota(jnp.int32, (width, width), 0)
    c = lax.broadcasted_iota(jnp.int32, (width, width), 1)
    ones = ((r // hd) == (c // hd)).astype(F32)
    parts = [_mm_exact_rhs(x[:, i:i + width], ones) for i in range(0, x.shape[1], width)]
    return parts[0] if len(parts) == 1 else jnp.concatenate(parts, axis=1)


def _ffn_kernel(xa_ref, xb_ref, g_ref, wgu_ref, wd_ref, fg_ref, oa_ref, ob_ref, h_scr, acc_scr, *, tf, final, steps_a):
    dff = wd_ref.shape[0]
    in_a = pl.program_id(0) < steps_a
    x = jnp.where(in_a, xa_ref[...], xb_ref[...])
    h_scr[...] = _rms(x, g_ref[...]).astype(_MXU_DTYPE)
    for c in range(dff // tf):
        h = h_scr[...]
        gate = jnp.dot(h, wgu_ref[:, c * tf:(c + 1) * tf], preferred_element_type=F32)
        up = jnp.dot(h, wgu_ref[:, dff + c * tf:dff + (c + 1) * tf], preferred_element_type=F32)
        act = (_silu(gate) * up).astype(_MXU_DTYPE)
        part = jnp.dot(act, wd_ref[c * tf:(c + 1) * tf, :], preferred_element_type=F32)
        if c == 0:
            acc_scr[...] = part
        else:
            acc_scr[...] += part
    y = x + 0.5 * acc_scr[...]
    if final:
        y = _rms(y, fg_ref[...])

    @pl.when(in_a)
    def _():
        oa_ref[...] = y

    @pl.when(jnp.logical_not(in_a))
    def _():
        ob_ref[...] = y


def _const_spec(shape):
    nd = len(shape)
    return pl.BlockSpec(shape, lambda *_: (0,) * nd, pipeline_mode=pl.Buffered(1))


def _layer_spec(stacked_shape, layer):
    nd = len(stacked_shape) - 1
    return pl.BlockSpec((None,) + tuple(stacked_shape[1:]), lambda *_: (layer,) + (0,) * nd,
                        pipeline_mode=pl.Buffered(1))


def _ffn(xa, xb, g, wgu, wd, layer, final_g, tm, tf):
    (na, d), nb_rows = xa.shape, xb.shape[0]
    assert na % tm == 0 and nb_rows % tm == 0
    steps_a, steps_b = na // tm, nb_rows // tm
    final = final_g is not None
    fg = final_g if final else g
    spec_a = pl.BlockSpec((tm, d), lambda i: (jnp.minimum(i, steps_a - 1), 0))
    spec_b = pl.BlockSpec((tm, d), lambda i: (jnp.maximum(i - steps_a, 0), 0))
    return pl.pallas_call(
        functools.partial(_ffn_kernel, tf=tf, final=final, steps_a=steps_a),
        out_shape=(jax.ShapeDtypeStruct((na, d), F32), jax.ShapeDtypeStruct((nb_rows, d), F32)),
        grid=(steps_a + steps_b,),
        in_specs=[
            spec_a,
            spec_b,
            _const_spec((1, d)),
            _layer_spec(wgu.shape, layer),
            _layer_spec(wd.shape, layer),
            _const_spec((1, d)),
        ],
        out_specs=(spec_a, spec_b),
        scratch_shapes=[pltpu.VMEM((tm, d), _MXU_DTYPE), pltpu.VMEM((tm, d), F32)],
        compiler_params=pltpu.CompilerParams(dimension_semantics=("arbitrary",),
                                             vmem_limit_bytes=_VMEM_LIMIT_BYTES),
        name="ffn",
    )(xa, xb, g.reshape(1, d), wgu, wd, fg.reshape(1, d))


def _mem_kv_kernel(x_ref, g_ref, w_ref, k_ref, v_ref, ks_ref, vs_ref):
    d = x_ref.shape[-1]
    n_heads, hd = ks_ref.shape[-2:]
    kv = _mm(_rms(x_ref[...], g_ref[...]), w_ref[...])
    k_ref[...] = kv[:, :d]
    v_ref[...] = kv[:, d:]
    for h in range(n_heads):
        ks_ref[:, h, :] = kv[:, h * hd:(h + 1) * hd]
        vs_ref[:, h, :] = kv[:, d + h * hd:d + (h + 1) * hd]


def _mem_kv(mem, g, w, n_heads):
    b, n_mem, d = mem.shape
    depth = w.shape[0]
    flat_spec = pl.BlockSpec((None, None, n_mem, d), lambda l, i: (l, i, 0, 0))
    split_spec = pl.BlockSpec((None, None, n_mem, n_heads, d // n_heads), lambda l, i: (l, i, 0, 0, 0))
    flat_shape = jax.ShapeDtypeStruct((depth, b, n_mem, d), F32)
    split_shape = jax.ShapeDtypeStruct((depth, b, n_mem, n_heads, d // n_heads), F32)
    return pl.pallas_call(
        _mem_kv_kernel,
        out_shape=(flat_shape, flat_shape, split_shape, split_shape),
        grid=(depth, b),
        in_specs=[pl.BlockSpec((None, n_mem, d), lambda l, i: (i, 0, 0)),
                  pl.BlockSpec((None, 1, d), lambda l, i: (l, 0, 0)),
                  pl.BlockSpec((None, d, 2 * d), lambda l, i: (l, 0, 0))],
        out_specs=(flat_spec, flat_spec, split_spec, split_spec),
        compiler_params=pltpu.CompilerParams(dimension_semantics=("parallel", "parallel"),
                                             vmem_limit_bytes=_VMEM_LIMIT_BYTES),
        name="mem_kv",
    )(mem, g.reshape(depth, 1, d), w)


def _xattn_kernel(x_ref, g_ref, wq_ref, wo_ref, mk_ref, mv_ref, o_ref, q_scr, a_scr, *cache_scr,
                  nb, tt, n_heads, layer, head_split):
    d = x_ref.shape[-1]
    hd = d // n_heads
    rows = nb * tt
    scale = hd ** -0.5

    if head_split:
        kbuf, vbuf, sem = cache_scr
        i = pl.program_id(0)
        slot = i % 2

        def slab_copies(step, to_slot):
            seqs = pl.ds(step * nb, nb)
            return [pltpu.make_async_copy(src.at[layer, seqs, :, h, :], buf.at[to_slot, h], sem.at[to_slot, kv, h])
                    for kv, (src, buf) in enumerate(((mk_ref, kbuf), (mv_ref, vbuf))) for h in range(n_heads)]

        def start_all(copies):
            for n, cp in enumerate(copies):
                cp.start(priority=n % 2)

        @pl.when(i == 0)
        def _():
            start_all(slab_copies(0, 0))

        @pl.when(i + 1 < pl.num_programs(0))
        def _():
            start_all(slab_copies(i + 1, 1 - slot))

        keys = lambda s, h: kbuf[slot, h, s]
        vals = lambda s, h: vbuf[slot, h, s]
    else:
        keys = lambda s, h: mk_ref[s, :, h * hd:(h + 1) * hd]
        vals = lambda s, h: mv_ref[s, :, h * hd:(h + 1) * hd]

    x = x_ref[...].reshape(rows, d)
    q_scr[...] = _mm(_rms(x, g_ref[...]), wq_ref[...])

    if head_split:
        for cp in slab_copies(i, slot):
            cp.wait()

    spp = math.gcd(nb, _SEQS_PER_PASS)

    def seq_body(sg, carry):
        chains = [(sg * spp + si, pl.ds(pl.multiple_of((sg * spp + si) * tt, _SUBLANES), tt),
                   h, slice(h * hd, (h + 1) * hd)) for si in range(spp) for h in range(n_heads)]
        scs = [_mm_nt(q_scr[rs, hs], keys(s, h)) * scale for s, rs, h, hs in chains]
        es = [jnp.exp(sc - jnp.max(sc, axis=-1, keepdims=True)) for sc in scs]
        prs = [e / jnp.sum(e, axis=-1, keepdims=True) for e in es]
        for pr, (s, rs, h, hs) in zip(prs, chains):
            a_scr[rs, hs] = _mm(pr, vals(s, h))
        return carry

    lax.fori_loop(0, nb // spp, seq_body, 0)
    o_ref[...] = (x + _mm(a_scr[...], wo_ref[...])).reshape(nb, tt, d)


def _xattn(x, g, wq, wo, mk, mv, layer, nb, tt):
    b, t, d = x.shape
    n_mem = mk.shape[2]
    rows = nb * tt
    head_split = mk.ndim == 5
    if head_split:
        assert t == tt and mk.shape[3] == X_HEADS
        mem_spec = pl.BlockSpec(memory_space=pl.ANY)
        slab = pltpu.VMEM((2, X_HEADS, nb, n_mem, d // X_HEADS), F32)
        cache_scr = [slab, slab, pltpu.SemaphoreType.DMA((2, 2, X_HEADS))]
        semantics = ("arbitrary", "arbitrary")
    else:
        mem_spec = pl.BlockSpec((None, nb, n_mem, d), lambda i, j: (layer, i, 0, 0))
        cache_scr = []
        semantics = ("parallel", "parallel")
    return pl.pallas_call(
        functools.partial(_xattn_kernel, nb=nb, tt=tt, n_heads=X_HEADS, layer=layer, head_split=head_split),
        out_shape=jax.ShapeDtypeStruct((b, t, d), F32),
        grid=(b // nb, t // tt),
        in_specs=[
            pl.BlockSpec((nb, tt, d), lambda i, j: (i, j, 0)),
            _const_spec((1, d)),
            _layer_spec(wq.shape, layer),
            _layer_spec(wo.shape, layer),
            mem_spec,
            mem_spec,
        ],
        out_specs=pl.BlockSpec((nb, tt, d), lambda i, j: (i, j, 0)),
        scratch_shapes=[pltpu.VMEM((rows, d), F32), pltpu.VMEM((rows, d), F32)] + cache_scr,
        compiler_params=pltpu.CompilerParams(dimension_semantics=semantics,
                                             vmem_limit_bytes=_VMEM_LIMIT_BYTES),
        name="xattn",
    )(x, g.reshape(1, d), wq, wo, mk, mv)


def _even_kernel(x_ref, g_ref, win_ref, wout_ref, cw_ref, lb_ref, gn_ref, conv_in_ref, hg_in_ref,
                 xo_ref, conv_out_ref, hg_out_ref,
                 p_scr, y_scr, ext_scr, b_scr, qg_scr, kk_scr, o_scr, st_scr, *, nb, tt, blk, one_tile):
    j = pl.program_id(1)
    nj = pl.num_programs(1)
    d = x_ref.shape[-1]
    rows = nb * tt
    aw = A_WIDTH
    hd = B_DIM

    def transpose_states(src, dst):
        if one_tile:
            for s in range(nb):
                for h in range(B_HEADS):
                    dst[s, h] = src[s, h].T
        else:
            def body(s, carry):
                for h in range(B_HEADS):
                    dst[s, h] = src[s, h].T
                return carry

            lax.fori_loop(0, nb, body, 0)

    def load_state():
        conv_out_ref[...] = conv_in_ref[...]
        transpose_states(hg_in_ref, st_scr)

    if one_tile:
        load_state()
    else:
        pl.when(j == 0)(load_state)

    x = x_ref[...].reshape(rows, d)
    h_in = _rms(x, g_ref[...]).astype(_MXU_DTYPE)
    n_parts = 2 if (nb % 2 == 0 and rows >= 2 * _MXU_WIDTH) else 1
    pnb = nb // n_parts
    prow = pnb * tt
    lb = lb_ref[...]
    cw = cw_ref[...]
    rowi = lax.broadcasted_iota(jnp.int32, (1, _SUBLANES, hd), 1)
    sp = math.gcd(pnb, _SEQS_PER_PASS * _SUBLANES // blk)
    chains = sp * B_HEADS
    col_blocks = lambda width: [(c, min(c + _MXU_WIDTH, width)) for c in range(0, width, _MXU_WIDTH)]

    def project_steps(part):
        r = slice(part * prow, (part + 1) * prow)

        def step(c, e):
            p_scr[r, c:e] = jnp.dot(h_in[r], win_ref[:, c:e], preferred_element_type=F32)

        return [functools.partial(step, c, e) for c, e in col_blocks(EVEN_IN)]

    def prepare(part):
        r = slice(part * prow, (part + 1) * prow)
        sq = slice(part * pnb, (part + 1) * pnb)
        u = p_scr[r, 2 * aw:3 * aw] * p_scr[r, 0:aw]
        ext_scr[sq, _SUBLANES:_SUBLANES + tt, :] = u.reshape(pnb, tt, aw)
        ext_scr[sq, _SUBLANES - 2:_SUBLANES, :] = conv_out_ref[sq]
        conv = (cw[0:1] * ext_scr[sq, _SUBLANES - 2:_SUBLANES - 2 + tt, :]
                + cw[1:2] * ext_scr[sq, _SUBLANES - 1:_SUBLANES - 1 + tt, :]
                + cw[2:3] * ext_scr[sq, _SUBLANES:_SUBLANES + tt, :])
        y_scr[r, 0:aw] = p_scr[r, aw:2 * aw] * conv.reshape(prow, aw)
        conv_out_ref[sq] = ext_scr[sq, _SUBLANES + tt - 2:_SUBLANES + tt, :]
        f = lb + (1.0 - lb) * _sigmoid(p_scr[r, 4 * aw:5 * aw])
        kk_scr[r, :] = 1.0 - f
        b_scr[r, :] = _mm_exact_lhs(_block_tri(prow, blk).astype(F32), jnp.log2(f))
        qg_scr[r, :] = _silu(p_scr[r, 3 * aw:4 * aw])

    def time_block(s0, k):
        rows_of = [slice((s0 + si) * tt + k * blk, (s0 + si) * tt + (k + 1) * blk) for si in range(sp)]
        stack = lambda ref, col0: jnp.stack([ref[rs, col0 + h * hd:col0 + (h + 1) * hd]
                                             for rs in rows_of for h in range(B_HEADS)])
        bb, qg, kkb = stack(b_scr, 0), stack(qg_scr, 0), stack(kk_scr, 0)
        vb = stack(p_scr, 5 * aw)
        st = st_scr[s0:s0 + sp].reshape(chains, hd, hd)
        blast = bb[:, blk - 1:blk, :]
        intra = []
        for p0 in range(0, blk, _SUBLANES):
            bi, qi = bb[:, p0:p0 + _SUBLANES, :], qg[:, p0:p0 + _SUBLANES, :]
            oi = jnp.zeros_like(bi)
            for jj in range(p0 + _SUBLANES):
                diff = bi - bb[:, jj:jj + 1, :]
                if jj >= p0:
                    diff = jnp.where(rowi >= jj - p0, diff, -jnp.inf)
                att = jnp.sum(qi * kkb[:, jj:jj + 1, :] * jnp.exp2(diff), axis=-1, keepdims=True)
                oi = oi + att * vb[:, jj:jj + 1, :]
            intra.append(oi)
        o = _bmm_nt(qg * jnp.exp2(bb), st) + (intra[0] if len(intra) == 1 else jnp.concatenate(intra, axis=1))
        kd = kkb * jnp.exp2(blast - bb)
        upd = jnp.stack([_mm_tn(vb[c], kd[c]) for c in range(chains)])
        st_scr[s0:s0 + sp] = (st * jnp.exp2(blast) + upd).reshape(sp, B_HEADS, hd, hd)
        for si, rs in enumerate(rows_of):
            for h in range(B_HEADS):
                o_scr[rs, h * hd:(h + 1) * hd] = o[si * B_HEADS + h]

    def recurrence_steps(part):
        return [functools.partial(time_block, part * pnb + sg * sp, k)
                for sg in range(pnb // sp) for k in range(tt // blk)]

    def finish(part):
        r = slice(part * prow, (part + 1) * prow)
        for h in range(B_HEADS):
            hs = slice(h * hd, (h + 1) * hd)
            o = o_scr[r, hs]
            on = o * lax.rsqrt(jnp.mean(o * o, axis=-1, keepdims=True) + NORM_EPS) * gn_ref[:, hs]
            y_scr[r, aw + h * hd:aw + (h + 1) * hd] = on * _silu(p_scr[r, 6 * aw + h * hd:6 * aw + (h + 1) * hd])

    def outproj_steps(part):
        r = slice(part * prow, (part + 1) * prow)
        sq = slice(part * pnb, (part + 1) * pnb)

        def step(c, e):
            y = y_scr[r, :].astype(_MXU_DTYPE)
            res = x[r, c:e] + jnp.dot(y, wout_ref[:, c:e], preferred_element_type=F32)
            xo_ref[sq, :, c:e] = res.reshape(pnb, tt, e - c)

        return [functools.partial(step, c, e) for c, e in col_blocks(d)]

    def run_interleaved(main, other):
        done = 0
        for i, step in enumerate(main):
            upto = (i + 1) * len(other) // len(main)
            for o_step in other[done:upto]:
                o_step()
            done = upto
            step()

    for step in project_steps(0):
        step()
    prepare(0)
    for part in range(n_parts):
        last = part == n_parts - 1
        other = [s for p in range(n_parts - 1) for s in outproj_steps(p)] if last else project_steps(part + 1)
        run_interleaved(recurrence_steps(part), other if n_parts > 1 else [])
        finish(part)
        if not last:
            prepare(part + 1)
    for step in outproj_steps(n_parts - 1):
        step()

    if one_tile:
        transpose_states(st_scr, hg_out_ref)
    else:
        pl.when(j == nj - 1)(lambda: transpose_states(st_scr, hg_out_ref))


def _even_layer(x, g, win, wout, cw, lb, gn, conv_in, hg_in, nb, tt, blk):
    b, t, d = x.shape
    rows = nb * tt
    return pl.pallas_call(
        functools.partial(_even_kernel, nb=nb, tt=tt, blk=blk, one_tile=(t == tt)),
        out_shape=(jax.ShapeDtypeStruct((b, t, d), F32),
                   jax.ShapeDtypeStruct(conv_in.shape, F32),
                   jax.ShapeDtypeStruct(hg_in.shape, F32)),
        grid=(b // nb, t // tt),
        in_specs=[
            pl.BlockSpec((nb, tt, d), lambda i, j: (i, j, 0)),
            _const_spec((1, d)),
            _const_spec(win.shape),
            _const_spec(wout.shape),
            _const_spec(cw.shape),
            _const_spec((1, A_WIDTH)),
            _const_spec((1, B_HEADS * B_DIM)),
            pl.BlockSpec((nb,) + conv_in.shape[1:], lambda i, j: (i, 0, 0)),
            pl.BlockSpec((nb,) + hg_in.shape[1:], lambda i, j: (i, 0, 0, 0)),
        ],
        out_specs=(
            pl.BlockSpec((nb, tt, d), lambda i, j: (i, j, 0)),
            pl.BlockSpec((nb,) + conv_in.shape[1:], lambda i, j: (i, 0, 0)),
            pl.BlockSpec((nb,) + hg_in.shape[1:], lambda i, j: (i, 0, 0, 0)),
        ),
        scratch_shapes=[
            pltpu.VMEM((rows, EVEN_IN), F32),
            pltpu.VMEM((rows, 2 * A_WIDTH), F32),
            pltpu.VMEM((nb, tt + _SUBLANES, A_WIDTH), F32),
            pltpu.VMEM((rows, A_WIDTH), F32),
            pltpu.VMEM((rows, A_WIDTH), F32),
            pltpu.VMEM((rows, A_WIDTH), F32),
            pltpu.VMEM((rows, A_WIDTH), F32),
            pltpu.VMEM((nb, B_HEADS, B_DIM, B_DIM), F32),
        ],
        compiler_params=pltpu.CompilerParams(dimension_semantics=("parallel", "arbitrary"),
                                             vmem_limit_bytes=_VMEM_LIMIT_BYTES),
        name="even_layer",
    )(x, g.reshape(1, d), win, wout, cw, lb.reshape(1, -1), gn.reshape(1, -1), conv_in, hg_in)


def _odd_kernel(x_ref, g_ref, win_ref, wout_ref, cos_ref, sin_ref, qdec_ref, kdec_ref, cdec_ref, dmask_ref,
                rgn_ref, mu_ref, w0_ref, w2_ref, a0_ref, a2_ref, g2_ref, kk_ref, ka_ref, rk_ref, lng_ref,
                lnb_ref, ret_in_ref, rw_in_ref, sh_in_ref,
                xo_ref, ret_out_ref, rw_out_ref, sh_out_ref,
                p_scr, y_scr, ext_scr, qr_scr, kr_scr, ir_scr,
                rt_scr, at_scr, bt_scr, kt_scr, vd_scr, cum_scr, od_scr, ah_scr, rh_scr, u0_scr, o0_scr,
                *, nb, tt, ch, gl, one_tile):
    j = pl.program_id(1)
    d = x_ref.shape[-1]
    rows = nb * tt
    cw = C_HEADS * C_DIM
    hd = C_DIM
    nd = D_HDIM
    n_groups = rows // gl
    cpg = gl // ch
    cps = tt // ch

    def load_state():
        ret_out_ref[...] = ret_in_ref[...]
        rw_out_ref[...] = rw_in_ref[...]
        sh_out_ref[...] = sh_in_ref[...]

    if one_tile and cps == 1:
        sh_out_ref[...] = sh_in_ref[...]
        ret_src, rw_src = ret_in_ref, rw_in_ref
    else:
        if one_tile:
            load_state()
        else:
            pl.when(j == 0)(load_state)
        ret_src, rw_src = ret_out_ref, rw_out_ref

    x = x_ref[...].reshape(rows, d)
    h_in = _rms(x, g_ref[...]).astype(_MXU_DTYPE)
    gpi = math.gcd(n_groups, _SEQS_PER_PASS)
    sp = math.gcd(nb, _SEQS_PER_PASS)

    def chain_rows(s0, k):
        return [pl.ds(pl.multiple_of((s0 + si) * tt + k * ch, _SUBLANES), ch) for si in range(sp)]

    _project(h_in, win_ref, p_scr, C_IN, ODD_IN)

    pd = p_scr[:, C_IN:ODD_IN]
    ext_scr[:, _SUBLANES:_SUBLANES + tt, :] = pd.reshape(nb, tt, D_IN)
    ext_scr[:, _SUBLANES - 1:_SUBLANES, :] = sh_out_ref[...]
    prev = ext_scr[:, _SUBLANES - 1:_SUBLANES - 1 + tt, :].reshape(rows, D_IN)
    sh_out_ref[...] = ext_scr[:, _SUBLANES + tt - 1:_SUBLANES + tt, :]
    pm = pd + mu_ref[...] * (prev - pd)
    dw = D_WIDTH
    r = pm[:, 0:dw]
    kd = pm[:, dw:2 * dw]
    vd = pm[:, 2 * dw:3 * dw]
    w_dn = pm[:, 3 * dw:3 * dw + 64]
    a_dn = pm[:, 3 * dw + 64:3 * dw + 128]
    g_dn = pm[:, 3 * dw + 128:3 * dw + 256]
    logdec = -math.exp(-0.5) * _sigmoid(w0_ref[...] + _mm(jnp.tanh(w_dn), w2_ref[...]))
    a = _sigmoid(a0_ref[...] + _mm(a_dn, a2_ref[...]))
    gate = _mm(_sigmoid(g_dn), g2_ref[...])
    kk0 = kd * kk_ref[...]
    kk = kk0 / jnp.maximum(jnp.sqrt(_head_sums(kk0 * kk0, nd)), 1e-12)
    kd2 = kd * (1.0 + (a - 1.0) * ka_ref[...])
    cum = _mm_exact_lhs(_block_tri(rows, ch).astype(F32), logdec)
    cum_scr[...] = cum
    rt_scr[...] = r * jnp.exp(cum)
    at_scr[...] = -kk * jnp.exp(cum - logdec)
    e_neg = jnp.exp(-cum)
    bt_scr[...] = kk * a * e_neg
    kt_scr[...] = kd2 * e_neg
    vd_scr[...] = vd
    bonus = _head_sums(r * kd2 * rk_ref[...], nd) * vd

    strict = _block_tri(gl, ch, strict=True)
    incl = _block_tri(gl, ch)
    eye = (lax.broadcasted_iota(jnp.int32, (gl, gl), 0) == lax.broadcasted_iota(jnp.int32, (gl, gl), 1)).astype(F32)
    n_dbl = int(math.log2(ch))

    def rw_prepare(it, carry):
        def heads(ref):
            return jnp.stack([ref[pl.ds(pl.multiple_of((it * gpi + g) * gl, _SUBLANES), gl), h * nd:(h + 1) * nd]
                              for g in range(gpi) for h in range(D_HEADS)])

        at, rt, bt, kt, vv = (heads(ref) for ref in (at_scr, rt_scr, bt_scr, kt_scr, vd_scr))
        sc = _bmm_nt(jnp.concatenate([at, rt], axis=1), jnp.concatenate([bt, kt], axis=1))
        n_ab = jnp.where(strict, sc[:, 0:gl, 0:gl], 0.0)
        a_ak = jnp.where(strict, sc[:, 0:gl, gl:2 * gl], 0.0)
        r_rb = jnp.where(incl, sc[:, gl:2 * gl, 0:gl], 0.0)
        r_rk = jnp.where(incl, sc[:, gl:2 * gl, gl:2 * gl], 0.0)
        xinv = eye + n_ab
        pw = n_ab
        for _ in range(n_dbl - 1):
            pw = _bmm(pw, pw)
            xinv = xinv + _bmm(pw, xinv)
        resid = eye - (xinv - _bmm_hi(n_ab, xinv))
        xinv = xinv + _bmm(xinv, resid)
        xa = _bmm(xinv, jnp.concatenate([at, _bmm(a_ak, vv)], axis=2))
        ah = xa[:, :, 0:nd]
        u0 = xa[:, :, nd:2 * nd]
        ra = _bmm(r_rb, xa)
        rh = rt + ra[:, :, 0:nd]
        o0 = ra[:, :, nd:2 * nd] + _bmm(r_rk, vv)
        for g in range(gpi):
            part = slice(g * D_HEADS, (g + 1) * D_HEADS)
            ah_scr[it * gpi + g] = ah[part]
            u0_scr[it * gpi + g] = u0[part]
            rh_scr[it * gpi + g] = rh[part]
            o0_scr[it * gpi + g] = o0[part]
        return carry

    lax.fori_loop(0, n_groups // gpi, rw_prepare, 0)

    _project(h_in, win_ref, p_scr, 0, C_IN)
    cos = cos_ref[0]
    sin = sin_ref[0]
    for h in range(C_HEADS):
        hs = slice(h * hd, (h + 1) * hd)
        qh = p_scr[:, h * hd:(h + 1) * hd]
        kh = p_scr[:, cw + h * hd:cw + (h + 1) * hd]
        qr_scr[:, hs] = qh * cos + pltpu.roll(qh, hd // 2, axis=1) * sin
        kr_scr[:, hs] = (kh * cos + pltpu.roll(kh, hd // 2, axis=1) * sin) * (C_DIM ** -0.5)

    def ret_intra(it, carry):
        rows_g = [pl.ds(pl.multiple_of((it * gpi + g) * gl, _SUBLANES), gl) for g in range(gpi)]
        stack = lambda ref, col0: jnp.stack([ref[rg, col0 + h * hd:col0 + (h + 1) * hd]
                                             for rg in rows_g for h in range(C_HEADS)])
        dm = dmask_ref[...]
        dm = dm if gpi == 1 else jnp.concatenate([dm] * gpi, axis=0)
        o = _bmm(_bmm_nt(stack(qr_scr, 0), stack(kr_scr, 0)) * dm, stack(p_scr, 2 * cw))
        for g, rg in enumerate(rows_g):
            for h in range(C_HEADS):
                ir_scr[rg, h * hd:(h + 1) * hd] = o[g * C_HEADS + h]
        return carry

    lax.fori_loop(0, n_groups // gpi, ret_intra, 0)

    per_chain = lambda ref, r: jnp.stack([ref[r, h * hd:(h + 1) * hd] for _ in range(sp) for h in range(C_HEADS)])
    qdec_c = per_chain(qdec_ref, slice(0, ch))
    kdec_c = per_chain(kdec_ref, slice(0, ch))
    cdec_c = per_chain(cdec_ref, slice(0, 1))

    def ret_seq_group(sg, carry):
        s0 = sg * sp

        def ret_chunk(k, c2):
            rows_c = chain_rows(s0, k)
            stack = lambda ref, col0: jnp.stack([ref[rc, col0 + h * hd:col0 + (h + 1) * hd]
                                                 for rc in rows_c for h in range(C_HEADS)])
            qc, kc, vc = stack(qr_scr, 0), stack(kr_scr, 0), stack(p_scr, 2 * cw)
            st = ret_src[pl.ds(s0, sp)].reshape(sp * C_HEADS, hd, hd)
            inter = _bmm(qc, st) * qdec_c
            kcd = kc * kdec_c
            upd = jnp.stack([_mm_tn(kcd[c], vc[c]) for c in range(sp * C_HEADS)])
            ret_out_ref[pl.ds(s0, sp)] = (cdec_c * st + upd).reshape(sp, C_HEADS, hd, hd)
            for si, rc in enumerate(rows_c):
                for h in range(C_HEADS):
                    ir_scr[rc, h * hd:(h + 1) * hd] += inter[si * C_HEADS + h]
            return c2

        lax.fori_loop(0, cps, ret_chunk, 0)
        return carry

    lax.fori_loop(0, nb // sp, ret_seq_group, 0)

    for h in range(C_HEADS):
        hs = slice(h * hd, (h + 1) * hd)
        o = ir_scr[:, hs]
        on = o * lax.rsqrt(jnp.mean(o * o, axis=-1, keepdims=True) + NORM_EPS) * rgn_ref[:, hs]
        y_scr[:, hs] = on * _silu(p_scr[:, 3 * cw + h * hd:3 * cw + (h + 1) * hd])

    n_chains = sp * D_HEADS

    def rw_seq_group(sg, carry):
        s0 = sg * sp

        def rw_chunk(k, c2):
            rows_c = chain_rows(s0, k)
            where = []
            for si in range(sp):
                cidx = (s0 + si) * cps + k
                where.append((cidx // cpg, pl.ds(pl.multiple_of((cidx % cpg) * ch, _SUBLANES), ch)))
            prep = lambda ref: jnp.stack([ref[gi, h, ls, :] for gi, ls in where for h in range(D_HEADS)])
            rowsl = lambda ref: jnp.stack([ref[rc, h * nd:(h + 1) * nd] for rc in rows_c for h in range(D_HEADS)])
            st = rw_src[pl.ds(s0, sp)].reshape(n_chains, nd, nd)
            ars = _bmm_nt(jnp.concatenate([prep(ah_scr), prep(rh_scr)], axis=1), st)
            uv = jnp.concatenate([ars[:, 0:ch, :] + prep(u0_scr), rowsl(vd_scr)], axis=1)
            bk = jnp.concatenate([rowsl(bt_scr), rowsl(kt_scr)], axis=1)
            upd = jnp.stack([_mm_tn(uv[c], bk[c]) for c in range(n_chains)])
            glast = jnp.stack([cum_scr[pl.ds(pl.multiple_of((s0 + si) * tt + (k + 1) * ch - _SUBLANES, _SUBLANES),
                                             _SUBLANES), h * nd:(h + 1) * nd][_SUBLANES - 1:_SUBLANES]
                               for si in range(sp) for h in range(D_HEADS)])
            rw_out_ref[pl.ds(s0, sp)] = ((st + upd) * jnp.exp(glast)).reshape(sp, D_HEADS, nd, nd)
            o = ars[:, ch:2 * ch, :] + prep(o0_scr)
            for si, rc in enumerate(rows_c):
                for h in range(D_HEADS):
                    od_scr[rc, h * nd:(h + 1) * nd] = o[si * D_HEADS + h]
            return c2

        lax.fori_loop(0, cps, rw_chunk, 0)
        return carry

    lax.fori_loop(0, nb // sp, rw_seq_group, 0)

    o_d = od_scr[...]
    mean = _head_sums(o_d, nd) * (1.0 / nd)
    xc = o_d - mean
    var = _head_sums(xc * xc, nd) * (1.0 / nd)
    on = xc * lax.rsqrt(var + RWKV_GN_EPS) * lng_ref[...] + lnb_ref[...]
    y_scr[:, cw:cw + dw] = (on + bonus) * gate

    xo_ref[...] = (x + _mm(y_scr[...], wout_ref[...])).reshape(nb, tt, d)


def _odd_layer(x, g, win, wout, tabs, rgn, rw, ret_in, rw_in, sh_in, nb, tt, ch, gl):
    b, t, d = x.shape
    rows = nb * tt
    cos, sin, qdec, kdec, cdec, dmask = tabs
    sh3 = sh_in.reshape(b, 1, D_IN)
    vecs = [rw[k].reshape(1, -1) for k in ("mu", "w0")] + [rw["w2"]] + [rw["a0"].reshape(1, -1), rw["a2"], rw["g2"]] + \
           [rw[k].reshape(1, -1) for k in ("k_k", "k_a", "r_k", "lnx_g", "lnx_b")]
    w512 = pltpu.VMEM((rows, D_WIDTH), F32)
    outs = pl.pallas_call(
        functools.partial(_odd_kernel, nb=nb, tt=tt, ch=ch, gl=gl, one_tile=(t == tt)),
        out_shape=(jax.ShapeDtypeStruct((b, t, d), F32),
                   jax.ShapeDtypeStruct(ret_in.shape, F32),
                   jax.ShapeDtypeStruct(rw_in.shape, F32),
                   jax.ShapeDtypeStruct(sh3.shape, F32)),
        grid=(b // nb, t // tt),
        in_specs=[
            pl.BlockSpec((nb, tt, d), lambda i, j: (i, j, 0)),
            _const_spec((1, d)),
            _const_spec(win.shape),
            _const_spec(wout.shape),
            pl.BlockSpec((1, rows, C_DIM), lambda i, j: (j, 0, 0)),
            pl.BlockSpec((1, rows, C_DIM), lambda i, j: (j, 0, 0)),
            _const_spec(qdec.shape),
            _const_spec(kdec.shape),
            _const_spec(cdec.shape),
            _const_spec(dmask.shape),
            _const_spec((1, C_HEADS * C_DIM)),
        ] + [_const_spec(v.shape) for v in vecs] + [
            pl.BlockSpec((nb,) + ret_in.shape[1:], lambda i, j: (i, 0, 0, 0)),
            pl.BlockSpec((nb,) + rw_in.shape[1:], lambda i, j: (i, 0, 0, 0)),
            pl.BlockSpec((nb, 1, D_IN), lambda i, j: (i, 0, 0)),
        ],
        out_specs=(
            pl.BlockSpec((nb, tt, d), lambda i, j: (i, j, 0)),
            pl.BlockSpec((nb,) + ret_in.shape[1:], lambda i, j: (i, 0, 0, 0)),
            pl.BlockSpec((nb,) + rw_in.shape[1:], lambda i, j: (i, 0, 0, 0)),
            pl.BlockSpec((nb, 1, D_IN), lambda i, j: (i, 0, 0)),
        ),
        scratch_shapes=[
            pltpu.VMEM((rows, ODD_IN), F32),
            pltpu.VMEM((rows, C_HEADS * C_DIM + D_WIDTH), F32),
            pltpu.VMEM((nb, tt + _SUBLANES, D_IN), F32),
        ] + [w512] * 10 + [pltpu.VMEM((rows // gl, D_HEADS, gl, D_HDIM), F32)] * 4,
        compiler_params=pltpu.CompilerParams(dimension_semantics=("parallel", "arbitrary"),
                                             vmem_limit_bytes=_VMEM_LIMIT_BYTES),
        name="odd_layer",
    )(x, g.reshape(1, d), win, wout, cos, sin, qdec, kdec, cdec, dmask, rgn.reshape(1, -1), *vecs,
      ret_in, rw_in, sh3)
    xo, ret_o, rw_o, sh_o = outs
    return xo, ret_o, rw_o, sh_o.reshape(b, D_IN)


def _odd_tables(pos0, t, nb, tt, ch, gl):
    half = C_DIM // 2
    inv = ROPE_BASE ** (-jnp.arange(half, dtype=F32) / half)
    pos = pos0 + jnp.arange(t, dtype=jnp.int32)
    ang = pos.astype(F32)[:, None] * inv[None, :]
    cos = jnp.cos(ang)
    sin = jnp.sin(ang)
    cosf = jnp.concatenate([cos, cos], axis=-1).reshape(t // tt, tt, C_DIM)
    sinf = jnp.concatenate([-sin, sin], axis=-1).reshape(t // tt, tt, C_DIM)
    cosf = jnp.tile(cosf, (1, nb, 1))
    sinf = jnp.tile(sinf, (1, nb, 1))
    lg = jnp.log1p(-jnp.exp2(-5.0 - jnp.arange(C_HEADS, dtype=F32)))[:, None]
    idx = jnp.arange(ch, dtype=F32)
    q_dec = jnp.exp(lg * (idx + 1.0))
    k_dec = jnp.exp(lg * (ch - 1.0 - idx))
    rel = idx[:, None] - idx[None, :]
    dm = jnp.where(rel >= 0, jnp.exp(lg[:, :, None] * jnp.maximum(rel, 0.0)), 0.0)
    c_dec = jnp.exp(lg * ch)
    cpg = gl // ch
    expand = lambda v: jnp.repeat(jnp.tile(v.T, (cpg, 1)), C_DIM, axis=1)
    qdec = expand(q_dec)
    kdec = expand(k_dec)
    cdec = jnp.repeat(c_dec.T, C_DIM, axis=1)
    gi = jnp.arange(gl)
    same = (gi[:, None] // ch) == (gi[None, :] // ch)
    dmask = jnp.where(same[None], jnp.tile(dm, (1, cpg, cpg)), 0.0)
    return cosf, sinf, qdec, kdec, cdec, dmask


def _prep_weights(ffn1_w_gu, ffn1_w_down, ffn2_w_gu, ffn2_w_down, even_w_in, even_w_out, odd_w_in, odd_w_out,
                  xattn_wq, xattn_wkv, xattn_wo):
    bf = lambda w: w.astype(_MXU_DTYPE)
    return dict(ffn1=(bf(ffn1_w_gu), bf(ffn1_w_down)), ffn2=(bf(ffn2_w_gu), bf(ffn2_w_down)),
                even_in=bf(even_w_in), even_out=bf(even_w_out), odd_in=bf(odd_w_in), odd_out=bf(odd_w_out),
                wq=bf(xattn_wq), wkv=bf(xattn_wkv), wo=bf(xattn_wo))


def _trunks(groups, W, P):
    depth = P["ffn1_norm"].shape[0]
    d = groups[0]["x"].shape[-1]
    lb_all = jnp.cumsum(jax.nn.softmax(P["hgrn_lb"].astype(F32), axis=0), axis=0)
    xs = [g["x"] for g in groups]
    new = [dict(conv=[], hg=[], ret=[], rw=[], sh=[]) for _ in groups]
    tm = min(g["cfg"]["tm"] for g in groups)
    tf = groups[0]["cfg"]["tf"]

    def ffn_both(xs, norm, weights, l, fin):
        flat = [x.reshape(-1, d) for x in xs]
        out = _ffn(flat[0], flat[1], norm, weights[0], weights[1], l, fin, tm, tf)
        return [o.reshape(x.shape) for o, x in zip(out, xs)]

    for l in range(depth):
        jl = l // 2
        xs = ffn_both(xs, P["ffn1_norm"][l], W["ffn1"], l, None)
        for gi, g in enumerate(groups):
            cfg, x = g["cfg"], xs[gi]
            if l % 2 == 0:
                x, cb, sh = _even_layer(x, P["mix_norm"][l], W["even_in"][jl], W["even_out"][jl], P["conv_w"][jl],
                                        lb_all[jl], P["hgrn_gnorm"][jl], g["conv"][jl], g["hg"][jl],
                                        cfg["enb"], cfg["ett"], cfg["blk"])
                new[gi]["conv"].append(cb)
                new[gi]["hg"].append(sh)
            else:
                tabs = _odd_tables(g["pos0"], x.shape[1], cfg["nb"], cfg["tt"], cfg["ch"], cfg["gl"])
                rw = {k: P["rwkv_" + k][jl] for k in ("mu", "w0", "w2", "a0", "a2", "g2", "k_k", "k_a", "r_k",
                                                      "lnx_g", "lnx_b")}
                x, sr, sw, ss = _odd_layer(x, P["mix_norm"][l], W["odd_in"][jl], W["odd_out"][jl], tabs,
                                           P["ret_gnorm"][jl], rw, g["ret"][jl], g["rw"][jl], g["sh"][jl],
                                           cfg["nb"], cfg["tt"], cfg["ch"], cfg["gl"])
                new[gi]["ret"].append(sr)
                new[gi]["rw"].append(sw)
                new[gi]["sh"].append(ss)
            xs[gi] = _xattn(x, P["xattn_norm"][l], W["wq"], W["wo"], g["mem_k"], g["mem_v"], l,
                            cfg["xnb"], cfg["xtt"])
        xs = ffn_both(xs, P["ffn2_norm"][l], W["ffn2"], l, P["final_norm"] if l == depth - 1 else None)
    return [(x,) + tuple(jnp.stack(n[k]) for k in ("conv", "hg", "ret", "rw", "sh")) for x, n in zip(xs, new)]


_ROW_TILE = 2 * _MXU_WIDTH
_WIDE_ROW_TILE = 2 * _ROW_TILE
_ODD_ROW_TILE = _MXU_WIDTH
_CHUNK = 64
_HGRN_BLOCK = 2 * _SUBLANES
_SHORT_ROWS = 128
_XATTN_SHORT_SEQS = 8


def _configs(b, t):
    if t >= _ROW_TILE // 2:
        nb = math.gcd(b, _SEQS_PER_PASS)
        enb = math.gcd(b, 2 * _SEQS_PER_PASS)
        return dict(tm=_WIDE_ROW_TILE, tf=_MXU_WIDTH, nb=nb, tt=_ODD_ROW_TILE // nb, enb=enb, ett=_ROW_TILE // enb,
                    blk=_HGRN_BLOCK, ch=_CHUNK, gl=_CHUNK, xnb=1, xtt=min(t, _WIDE_ROW_TILE))
    nb = min(b, _SHORT_ROWS // t)
    return dict(tm=min(b * t, _WIDE_ROW_TILE), tf=_MXU_WIDTH, nb=nb, tt=t, enb=nb, ett=t, blk=t, ch=t, gl=nb * t,
                xnb=min(b, _XATTN_SHORT_SEQS), xtt=t)


def kernel(x_prompt, x_sample, state_conv, state_hgrn, state_ret, state_rwkv, state_shift, cache_mem_k, cache_mem_v, mem_prompt, ffn1_norm, ffn1_w_gu, ffn1_w_down, mix_norm, even_w_in, even_w_out, conv_w, hgrn_lb, hgrn_gnorm, odd_w_in, odd_w_out, ret_gnorm, rwkv_mu, rwkv_w0, rwkv_w2, rwkv_a0, rwkv_a2, rwkv_g2, rwkv_k_k, rwkv_k_a, rwkv_r_k, rwkv_lnx_g, rwkv_lnx_b, xattn_norm, mem_norm, xattn_wq, xattn_wkv, xattn_wo, ffn2_norm, ffn2_w_gu, ffn2_w_down, final_norm):
    P = dict(ffn1_norm=ffn1_norm, mix_norm=mix_norm, conv_w=conv_w, hgrn_lb=hgrn_lb, hgrn_gnorm=hgrn_gnorm,
             ret_gnorm=ret_gnorm, rwkv_mu=rwkv_mu, rwkv_w0=rwkv_w0, rwkv_w2=rwkv_w2, rwkv_a0=rwkv_a0,
             rwkv_a2=rwkv_a2, rwkv_g2=rwkv_g2, rwkv_k_k=rwkv_k_k, rwkv_k_a=rwkv_k_a,
             rwkv_r_k=rwkv_r_k.reshape(rwkv_r_k.shape[0], -1), rwkv_lnx_g=rwkv_lnx_g, rwkv_lnx_b=rwkv_lnx_b,
             xattn_norm=xattn_norm, ffn2_norm=ffn2_norm, final_norm=final_norm)
    W = _prep_weights(ffn1_w_gu, ffn1_w_down, ffn2_w_gu, ffn2_w_down, even_w_in, even_w_out, odd_w_in,
                      odd_w_out, xattn_wq, xattn_wkv, xattn_wo)
    bp, tp, d = x_prompt.shape
    bs, ts, _ = x_sample.shape
    depth = ffn1_norm.shape[0]

    mem_k_p, mem_v_p, mem_k_out, mem_v_out = _mem_kv(mem_prompt, mem_norm, W["wkv"], X_HEADS)

    z = lambda ref: jnp.zeros((ref.shape[0], bp) + ref.shape[2:], F32)
    prompt = dict(x=x_prompt, pos0=0, conv=z(state_conv), hg=z(state_hgrn), ret=z(state_ret), rw=z(state_rwkv),
                  sh=z(state_shift), mem_k=mem_k_p, mem_v=mem_v_p, cfg=_configs(bp, tp))
    sample = dict(x=x_sample, pos0=PAST_LEN, conv=state_conv, hg=state_hgrn, ret=state_ret, rw=state_rwkv,
                  sh=state_shift, mem_k=cache_mem_k, mem_v=cache_mem_v, cfg=_configs(bs, ts))
    (y_p, conv_p, hg_p, ret_p, rw_p, sh_p), (y_s, conv_s, hg_s, ret_s, rw_s, sh_s) = _trunks([prompt, sample], W, P)
    return (y_p, y_s, conv_p, hg_p, ret_p, rw_p, sh_p, mem_k_out, mem_v_out,
            conv_s, hg_s, ret_s, rw_s, sh_s)
```
